```python
import math, functools
import jax, jax.numpy as jnp
from jax import lax
import numpy as np

D_MODEL = 2048
BATCH = 4
SEQ = 2048
DEPTH = 1
DEC_BATCH = 128
DEC_SEQ = 4
PAST_LEN = 8192
PAGE_SIZE = 128

N_RET_HEADS = 8
RET_DK = 128
RET_DV = 256
RET_CHUNK = 128
N_SWA_HEADS = 32
N_SWA_KV_HEADS = 4
SWA_GROUP = N_SWA_HEADS // N_SWA_KV_HEADS
SWA_HEAD_DIM = 64
WINDOW = 128
N_EXPERTS = 32
TOP_K = 4
D_FF = 2048
MOE_BLOCK = 128
SWIGLU_LIMIT = 7.0
SWIGLU_ALPHA = 1.702

LN_EPS = 1e-5
GN_EPS = 1e-6
NEG_INF = -1e30

RET_QK_W = N_RET_HEADS * RET_DK
RET_V_W = N_RET_HEADS * RET_DV
SWA_Q_W = N_SWA_HEADS * SWA_HEAD_DIM
SWA_KV_W = N_SWA_KV_HEADS * SWA_HEAD_DIM
IN_SIZES = (RET_QK_W, RET_QK_W, RET_V_W, RET_V_W, SWA_Q_W, SWA_KV_W, SWA_KV_W, D_MODEL, D_MODEL)
D_IN = 2 * RET_QK_W + 2 * RET_V_W + SWA_Q_W + 2 * SWA_KV_W + 2 * D_MODEL

kernel_name = 'hybrid_retention_swa_moe_step'


def layer_norm(x, g, b):
    xf = x.astype(jnp.float32)
    mu = jnp.mean(xf, axis=-1, keepdims=True)
    xc = xf - mu
    var = jnp.mean(xc * xc, axis=-1, keepdims=True)
    return (xc * lax.rsqrt(var + LN_EPS) * g.astype(jnp.float32) + b.astype(jnp.float32)).astype(x.dtype)


def head_group_norm(o):
    of = o.astype(jnp.float32)
    mu = jnp.mean(of, axis=-1, keepdims=True)
    oc = of - mu
    var = jnp.mean(oc * oc, axis=-1, keepdims=True)
    return (oc * lax.rsqrt(var + GN_EPS)).astype(o.dtype)


def retention_log_decay():
    return jnp.log(1.0 - jnp.exp2(-5.0 - jnp.arange(N_RET_HEADS, dtype=jnp.float32)))


def alibi_slopes():
    h = jnp.arange(1, N_SWA_HEADS + 1, dtype=jnp.float32)
    return jnp.exp2(-8.0 * h / N_SWA_HEADS).reshape(N_SWA_KV_HEADS, SWA_GROUP)


def retention_chunk(state, q, k, v, log_gamma):
    C = q.shape[2]
    idx = jnp.arange(C, dtype=jnp.float32)
    diff = idx[:, None] - idx[None, :]
    decay = jnp.where(diff >= 0, jnp.exp(log_gamma[:, None, None] * jnp.maximum(diff, 0.0)), 0.0).astype(q.dtype)
    q_decay = jnp.exp(log_gamma[:, None] * (idx + 1.0)).astype(q.dtype)
    k_decay = jnp.exp(log_gamma[:, None] * (C - 1.0 - idx)).astype(q.dtype)
    chunk_decay = jnp.exp(log_gamma * C).astype(q.dtype)
    scores = jnp.einsum('bhqd,bhkd->bhqk', q, k) * decay
    o = (jnp.einsum('bhqk,bhke->bhqe', scores, v)
         + jnp.einsum('bhqd,bhde->bhqe', q * q_decay[..., None], state))
    new_state = (chunk_decay[:, None, None] * state
                 + jnp.einsum('bhkd,bhke->bhde', k * k_decay[..., None], v))
    return new_state, o


def retention_prompt(q, k, v, log_gamma):
    B, S, H, _ = q.shape
    nc = S // RET_CHUNK

    def to_chunks(t):
        return t.reshape(B, nc, RET_CHUNK, H, t.shape[-1]).transpose(1, 0, 3, 2, 4)

    state0 = jnp.zeros((B, H, RET_DK, RET_DV), q.dtype)

    def step(state, qkv):
        qc, kc, vc = qkv
        return retention_chunk(state, qc, kc, vc, log_gamma)

    state, o = lax.scan(step, state0, (to_chunks(q), to_chunks(k), to_chunks(v)))
    o = o.transpose(1, 0, 3, 2, 4).reshape(B, S, H, RET_DV)
    return o, state


def retention_sample(q, k, v, log_gamma, state):
    new_state, o = retention_chunk(state, q.transpose(0, 2, 1, 3), k.transpose(0, 2, 1, 3),
                                   v.transpose(0, 2, 1, 3), log_gamma)
    return o.transpose(0, 2, 1, 3), new_state


def sink_attention(q, k, v, dist, valid, slopes, sinks):
    s = jnp.einsum('...qhgd,...khd->...hgqk', q, k).astype(jnp.float32) * (SWA_HEAD_DIM ** -0.5)
    s = s - slopes[:, :, None, None] * dist.astype(jnp.float32)[..., None, None, :, :]
    s = jnp.where(valid[..., None, None, :, :], s, NEG_INF)
    sink = jnp.broadcast_to(sinks.astype(jnp.float32).reshape(N_SWA_KV_HEADS, SWA_GROUP, 1, 1),
                            s.shape[:-1] + (1,))
    p = jax.nn.softmax(jnp.concatenate([s, sink], axis=-1), axis=-1)[..., :-1]
    return jnp.einsum('...hgqk,...khd->...qhgd', p.astype(v.dtype), v)


def swa_prompt(q, k, v, slopes, sinks):
    B, S = q.shape[:2]
    nb = S // WINDOW
    qb = q.reshape(B, nb, WINDOW, N_SWA_KV_HEADS, SWA_GROUP, SWA_HEAD_DIM)
    kb = k.reshape(B, nb, WINDOW, N_SWA_KV_HEADS, SWA_HEAD_DIM)
    vb = v.reshape(B, nb, WINDOW, N_SWA_KV_HEADS, SWA_HEAD_DIM)

    def with_prev(t):
        prev = jnp.concatenate([jnp.zeros_like(t[:, :1]), t[:, :-1]], axis=1)
        return jnp.concatenate([prev, t], axis=2)

    qi = jnp.arange(WINDOW)[:, None] + WINDOW
    ki = jnp.arange(2 * WINDOW)[None, :]
    dist = qi - ki
    in_window = (dist >= 0) & (dist < WINDOW)
    first = (jnp.arange(nb) == 0)[:, None, None]
    valid = in_window[None] & ~(first & (ki < WINDOW)[None])
    o = sink_attention(qb, with_prev(kb), with_prev(vb), dist, valid, slopes, sinks)
    return o.reshape(B, S, SWA_Q_W), k[:, S - WINDOW:], v[:, S - WINDOW:]


def swa_sample(q, k, v, slopes, sinks, cache_k, cache_v):
    B, T = q.shape[:2]
    W = cache_k.shape[1]
    keys = jnp.concatenate([cache_k, k], axis=1)
    vals = jnp.concatenate([cache_v, v], axis=1)
    dist = (jnp.arange(T)[:, None] + W) - jnp.arange(W + T)[None, :]
    valid = (dist >= 0) & (dist < WINDOW)
    o = sink_attention(q, keys, vals, dist, valid, slopes, sinks)
    return o.reshape(B, T, SWA_Q_W), keys[:, T:], vals[:, T:]


def token_mixer(x, w_in, w_ret_o, w_swa_o, w_out, retention_fn, swa_fn):
    B, T, _ = x.shape
    u = jnp.einsum('btd,dn->btn', x, w_in)
    offsets = np.cumsum(IN_SIZES)[:-1].tolist()
    q_r, k_r, v_r, g_r, q_s, k_s, v_s, gate_r, gate_s = jnp.split(u, offsets, axis=-1)
    o_r, ret_state = retention_fn(q_r.reshape(B, T, N_RET_HEADS, RET_DK),
                                  k_r.reshape(B, T, N_RET_HEADS, RET_DK) * (RET_DK ** -0.5),
                                  v_r.reshape(B, T, N_RET_HEADS, RET_DV))
    o_r = head_group_norm(o_r).reshape(B, T, RET_V_W) * jax.nn.silu(g_r)
    o_s, k_cache, v_cache = swa_fn(q_s.reshape(B, T, N_SWA_KV_HEADS, SWA_GROUP, SWA_HEAD_DIM),
                                   k_s.reshape(B, T, N_SWA_KV_HEADS, SWA_HEAD_DIM),
                                   v_s.reshape(B, T, N_SWA_KV_HEADS, SWA_HEAD_DIM))
    merged = jax.nn.sigmoid(gate_r) * (o_r @ w_ret_o) + jax.nn.sigmoid(gate_s) * (o_s @ w_swa_o)
    return merged @ w_out, ret_state, k_cache, v_cache


def moe_ffn(x, w_router, b_router, w_gate_up, b_gate_up, w_down, b_down):
    T, D = x.shape
    logits = x.astype(jnp.float32) @ w_router.astype(jnp.float32) + b_router.astype(jnp.float32)
    top_val, top_idx = lax.top_k(logits, TOP_K)
    gate = jax.nn.softmax(top_val, axis=-1).astype(x.dtype)
    TK = T * TOP_K
    e_flat = top_idx.reshape(TK)
    order = jnp.argsort(e_flat)
    e_sorted = e_flat[order]
    tok_sorted = (order // TOP_K).astype(jnp.int32)
    w_sorted = gate.reshape(TK)[order]
    counts = jnp.bincount(e_flat, length=N_EXPERTS)
    start = jnp.cumsum(counts) - counts
    padded = (counts + MOE_BLOCK - 1) // MOE_BLOCK * MOE_BLOCK
    pstart = jnp.cumsum(padded) - padded
    pend = pstart + padded
    dest = pstart[e_sorted] + (jnp.arange(TK) - start[e_sorted])
    n_blocks = (TK + N_EXPERTS * (MOE_BLOCK - 1) + MOE_BLOCK - 1) // MOE_BLOCK
    n_rows = n_blocks * MOE_BLOCK
    row_tok = jnp.full((n_rows,), T, jnp.int32).at[dest].set(tok_sorted)
    row_w = jnp.zeros((n_rows,), x.dtype).at[dest].set(w_sorted)
    block_start = jnp.arange(n_blocks) * MOE_BLOCK
    block_exp = jnp.minimum(jnp.sum(block_start[:, None] >= pend[None, :], axis=1), N_EXPERTS - 1)
    x_pad = jnp.concatenate([x, jnp.zeros((1, D), x.dtype)], axis=0)
    xb = x_pad[row_tok].reshape(n_blocks, MOE_BLOCK, D)

    def expert_block(args):
        e, xe = args
        h = xe @ w_gate_up[e] + b_gate_up[e]
        g, up = jnp.split(h, 2, axis=-1)
        g = jnp.minimum(g, SWIGLU_LIMIT)
        up = jnp.clip(up, -SWIGLU_LIMIT, SWIGLU_LIMIT)
        glu = g * jax.nn.sigmoid(SWIGLU_ALPHA * g)
        return ((up + 1.0) * glu) @ w_down[e] + b_down[e]

    yb = lax.map(expert_block, (block_exp, xb))
    y = jax.ops.segment_sum(yb.reshape(n_rows, D) * row_w[:, None], row_tok, num_segments=T + 1)
    return y[:T]


def decoder_layer(x, retention_fn, swa_fn, alpha, w_in, w_ret_o, w_swa_o, w_out, ln1_g, ln1_b,
                  w_router, b_router, w_gate_up, b_gate_up, w_down, b_down, ln2_g, ln2_b):
    mix, ret_state, k_cache, v_cache = token_mixer(x, w_in, w_ret_o, w_swa_o, w_out, retention_fn, swa_fn)
    h = layer_norm(alpha * x + mix, ln1_g, ln1_b)
    B, T, D = h.shape
    f = moe_ffn(h.reshape(B * T, D), w_router, b_router, w_gate_up, b_gate_up, w_down, b_down).reshape(B, T, D)
    y = layer_norm(alpha * h + f, ln2_g, ln2_b)
    return y, ret_state, k_cache, v_cache


def setup_inputs(seed: int = 0) -> dict:
    key = jax.random.key(seed)
    ks = jax.random.split(key, 20)
    f32 = jnp.float32

    def nrm(k, shape, scale):
        return jax.random.normal(k, shape, f32) * scale

    beta = (8.0 * DEPTH) ** -0.25
    return {
        'x_prompt': nrm(ks[0], (BATCH, SEQ, D_MODEL), 1.0),
        'x_sample': nrm(ks[1], (DEC_BATCH, DEC_SEQ, D_MODEL), 1.0),
        'state_ret': nrm(ks[2], (DEPTH, DEC_BATCH, N_RET_HEADS, RET_DK, RET_DV), 0.1),
        'cache_swa_k': nrm(ks[3], (DEPTH, DEC_BATCH, WINDOW, N_SWA_KV_HEADS, SWA_HEAD_DIM), 1.0),
        'cache_swa_v': nrm(ks[4], (DEPTH, DEC_BATCH, WINDOW, N_SWA_KV_HEADS, SWA_HEAD_DIM), 1.0),
        'w_in': nrm(ks[5], (DEPTH, D_MODEL, D_IN), D_MODEL ** -0.5),
        'w_ret_o': nrm(ks[6], (DEPTH, RET_V_W, D_MODEL), RET_V_W ** -0.5),
        'w_swa_o': nrm(ks[7], (DEPTH, SWA_Q_W, D_MODEL), SWA_Q_W ** -0.5),
        'w_out': nrm(ks[8], (DEPTH, D_MODEL, D_MODEL), beta * D_MODEL ** -0.5),
        'swa_sinks': nrm(ks[9], (DEPTH, N_SWA_HEADS), 0.5),
        'ln1_g': 1.0 + nrm(ks[10], (DEPTH, D_MODEL), 0.05),
        'ln1_b': nrm(ks[11], (DEPTH, D_MODEL), 0.05),
        'w_router': nrm(ks[12], (DEPTH, D_MODEL, N_EXPERTS), D_MODEL ** -0.5),
        'b_router': nrm(ks[13], (DEPTH, N_EXPERTS), 0.01),
        'w_gate_up': nrm(ks[14], (DEPTH, N_EXPERTS, D_MODEL, 2 * D_FF), D_MODEL ** -0.5),
        'b_gate_up': nrm(ks[15], (DEPTH, N_EXPERTS, 2 * D_FF), 0.02),
        'w_down': nrm(ks[16], (DEPTH, N_EXPERTS, D_FF, D_MODEL), beta * D_FF ** -0.5),
        'b_down': nrm(ks[17], (DEPTH, N_EXPERTS, D_MODEL), 0.02),
        'ln2_g': 1.0 + nrm(ks[18], (DEPTH, D_MODEL), 0.05),
        'ln2_b': nrm(ks[19], (DEPTH, D_MODEL), 0.05),
    }


def reference(x_prompt, x_sample, state_ret, cache_swa_k, cache_swa_v, w_in, w_ret_o, w_swa_o, w_out,
              swa_sinks, ln1_g, ln1_b, w_router, b_router, w_gate_up, b_gate_up, w_down, b_down,
              ln2_g, ln2_b):
    alpha = (2.0 * DEPTH) ** 0.25
    log_gamma = retention_log_decay()
    slopes = alibi_slopes()
    y_prompt, y_sample = x_prompt, x_sample
    ret_p, k_p, v_p, ret_s, k_s, v_s = [], [], [], [], [], []
    for l in range(DEPTH):
        layer_w = (w_in[l], w_ret_o[l], w_swa_o[l], w_out[l], ln1_g[l], ln1_b[l], w_router[l], b_router[l],
                   w_gate_up[l], b_gate_up[l], w_down[l], b_down[l], ln2_g[l], ln2_b[l])
        y_prompt, rs, kc, vc = decoder_layer(
            y_prompt,
            functools.partial(retention_prompt, log_gamma=log_gamma),
            functools.partial(swa_prompt, slopes=slopes, sinks=swa_sinks[l]),
            alpha, *layer_w)
        ret_p.append(rs)
        k_p.append(kc)
        v_p.append(vc)
        y_sample, rs, kc, vc = decoder_layer(
            y_sample,
            functools.partial(retention_sample, log_gamma=log_gamma, state=state_ret[l]),
            functools.partial(swa_sample, slopes=slopes, sinks=swa_sinks[l],
                              cache_k=cache_swa_k[l], cache_v=cache_swa_v[l]),
            alpha, *layer_w)
        ret_s.append(rs)
        k_s.append(kc)
        v_s.append(vc)
    state_ret_prompt = jnp.stack(ret_p)
    cache_swa_k_prompt = jnp.stack(k_p)
    cache_swa_v_prompt = jnp.stack(v_p)
    state_ret_sample = jnp.stack(ret_s)
    cache_swa_k_sample = jnp.stack(k_s)
    cache_swa_v_sample = jnp.stack(v_s)
    return (y_prompt, y_sample, state_ret_prompt, cache_swa_k_prompt, cache_swa_v_prompt,
            state_ret_sample, cache_swa_k_sample, cache_swa_v_sample)
```

```python
import functools

import jax
import jax.numpy as jnp
from jax import lax
from jax.experimental import pallas as pl
from jax.experimental.pallas import tpu as pltpu

F32 = jnp.float32
BF16 = jnp.bfloat16

D_MODEL = 2048
BATCH = 4
SEQ = 2048
DEC_BATCH = 128
DEC_SEQ = 4
N_RET_HEADS = 8
RET_DK = 128
RET_DV = 256
RET_CHUNK = 128
N_SWA_HEADS = 32
N_SWA_KV_HEADS = 4
SWA_GROUP = 8
SWA_HEAD_DIM = 64
WINDOW = 128
N_EXPERTS = 32
TOP_K = 4
D_FF = 2048
SWIGLU_LIMIT = 7.0
SWIGLU_ALPHA = 1.702
LN_EPS = 1e-5
GN_EPS = 1e-6
NEG_INF = -1e30
DEPTH = 1

T_PROMPT = BATCH * SEQ
T_SAMPLE = DEC_BATCH * DEC_SEQ
T_ALL = T_PROMPT + T_SAMPLE
TK_ALL = T_ALL * TOP_K
D_IN = 12800

COL_QR, COL_KR, COL_VR, COL_GR, COL_QS, COL_GATE_R, COL_GATE_S, COL_KVS = (
    0, 1024, 2048, 4096, 6144, 8192, 10240, 12288)

MOE_SUB = 256
MOE_CHUNK_SUBS = 5
MOE_R = MOE_SUB * MOE_CHUNK_SUBS
MOE_NSUB_MAX = TK_ALL // MOE_SUB + N_EXPERTS
MOE_XROWS = MOE_NSUB_MAX * MOE_SUB + MOE_R
MOE_NC = TK_ALL // MOE_R + N_EXPERTS
MOE_TN = 512
MOE_NJ = D_FF // MOE_TN

VMEM_LIMIT = 56 * 1024 * 1024


def _params(sem):
    return pltpu.CompilerParams(dimension_semantics=sem, vmem_limit_bytes=VMEM_LIMIT)


def _in_proj_kernel(x_ref, w_ref, o_ref):
    o_ref[...] = jnp.dot(x_ref[...].astype(BF16), w_ref[...], preferred_element_type=F32)


def in_proj(x2d, w_bf, tm, tn):
    m, k = x2d.shape
    n = w_bf.shape[1]
    return pl.pallas_call(
        _in_proj_kernel,
        grid=(m // tm, n // tn),
        in_specs=[pl.BlockSpec((tm, k), lambda i, j: (i, 0)),
                  pl.BlockSpec((k, tn), lambda i, j: (0, j))],
        out_specs=pl.BlockSpec((tm, tn), lambda i, j: (i, j)),
        out_shape=jax.ShapeDtypeStruct((m, n), F32),
        compiler_params=_params(("parallel", "parallel")),
        name="in_proj",
    )(x2d, w_bf)


def _group_norm_gate(o, g):
    mu = jnp.mean(o, axis=-1, keepdims=True)
    oc = o - mu
    var = jnp.mean(oc * oc, axis=-1, keepdims=True)
    return oc * lax.rsqrt(var + GN_EPS) * (g * jax.nn.sigmoid(g))


def _dot_nt(a, b):
    return lax.dot_general(a, b, (((1,), (1,)), ((), ())), preferred_element_type=F32)


def _layer_norm(x, g, b):
    mu = jnp.mean(x, axis=-1, keepdims=True)
    xc = x - mu
    var = jnp.mean(xc * xc, axis=-1, keepdims=True)
    return xc * lax.rsqrt(var + LN_EPS) * g + b


def _ret_prompt_kernel(cd_ref, dec_ref, qd_ref, kd_ref, q_ref, k_ref, v_ref, g_ref, o_ref, st_ref):
    c = pl.program_id(1)

    @pl.when(c == 0)
    def _():
        st_ref[...] = jnp.zeros_like(st_ref)

    for h in range(N_RET_HEADS):
        q = q_ref[:, h * RET_DK:(h + 1) * RET_DK]
        k = k_ref[:, h * RET_DK:(h + 1) * RET_DK] * (RET_DK ** -0.5)
        v = v_ref[:, h * RET_DV:(h + 1) * RET_DV].astype(BF16)
        state = st_ref[0, h]
        scores = _dot_nt(q.astype(BF16), k.astype(BF16)) * dec_ref[h]
        o = (jnp.dot(scores.astype(BF16), v, preferred_element_type=F32)
             + jnp.dot((q * qd_ref[h]).astype(BF16), state.astype(BF16), preferred_element_type=F32))
        kt = (k * kd_ref[h]).T.astype(BF16)
        st_ref[0, h] = cd_ref[h] * state + jnp.dot(kt, v, preferred_element_type=F32)
        g = g_ref[:, h * RET_DV:(h + 1) * RET_DV]
        o_ref[:, h * RET_DV:(h + 1) * RET_DV] = _group_norm_gate(o, g).astype(BF16)


def retention_prompt(u, tables):
    cd, dec, qd, kd = tables
    nc = SEQ // RET_CHUNK
    row = lambda b, c: b * nc + c
    whole3 = lambda b, c: (0, 0, 0)
    return pl.pallas_call(
        _ret_prompt_kernel,
        grid=(BATCH, nc),
        in_specs=[
            pl.BlockSpec(memory_space=pltpu.SMEM),
            pl.BlockSpec((N_RET_HEADS, RET_CHUNK, RET_CHUNK), whole3),
            pl.BlockSpec((N_RET_HEADS, RET_CHUNK, 1), whole3),
            pl.BlockSpec((N_RET_HEADS, RET_CHUNK, 1), whole3),
            pl.BlockSpec((RET_CHUNK, 1024), lambda b, c: (row(b, c), COL_QR // 1024)),
            pl.BlockSpec((RET_CHUNK, 1024), lambda b, c: (row(b, c), COL_KR // 1024)),
            pl.BlockSpec((RET_CHUNK, 2048), lambda b, c: (row(b, c), COL_VR // 2048)),
            pl.BlockSpec((RET_CHUNK, 2048), lambda b, c: (row(b, c), COL_GR // 2048)),
        ],
        out_specs=[
            pl.BlockSpec((RET_CHUNK, 2048), lambda b, c: (row(b, c), 0)),
            pl.BlockSpec((1, N_RET_HEADS, RET_DK, RET_DV), lambda b, c: (b, 0, 0, 0)),
        ],
        out_shape=[
            jax.ShapeDtypeStruct((T_PROMPT, 2048), BF16),
            jax.ShapeDtypeStruct((BATCH, N_RET_HEADS, RET_DK, RET_DV), F32),
        ],
        compiler_params=_params(("parallel", "arbitrary")),
        name="retention_prompt",
    )(cd, dec, qd, kd, u, u, u, u)


RS_BB = 32


def _ret_sample_kernel(cd_ref, dec_ref, qd_ref, kd_ref, q_ref, k_ref, v_ref, g_ref, st_ref,
                       o_ref, sto_ref):
    h = pl.program_id(1)
    q = q_ref[...]
    k = k_ref[...] * (RET_DK ** -0.5)
    v = v_ref[...].astype(BF16)
    scores = _dot_nt(q.astype(BF16), k.astype(BF16)) * dec_ref[0]
    o_intra = jnp.dot(scores.astype(BF16), v, preferred_element_type=F32)
    qq = (q * qd_ref[0]).astype(BF16)
    kt = (k * kd_ref[0]).T
    col_batch = lax.broadcasted_iota(jnp.int32, kt.shape, 1) // DEC_SEQ
    first_of_pair = lax.broadcasted_iota(jnp.int32, (8, RET_DV), 0) < DEC_SEQ
    cd = cd_ref[h]
    pieces = []
    for p in range(RS_BB // 2):
        q8 = qq[8 * p:8 * p + 8]
        s0 = st_ref[2 * p].astype(BF16)
        s1 = st_ref[2 * p + 1].astype(BF16)
        r0 = jnp.dot(q8, s0, preferred_element_type=F32)
        r1 = jnp.dot(q8, s1, preferred_element_type=F32)
        pieces.append(jnp.where(first_of_pair, r0, r1))
    for b in range(RS_BB):
        ktm = jnp.where(col_batch == b, kt, 0.0).astype(BF16)
        sto_ref[b] = cd * st_ref[b] + jnp.dot(ktm, v, preferred_element_type=F32)
    o = o_intra + jnp.concatenate(pieces, axis=0)
    o_ref[...] = _group_norm_gate(o, g_ref[...]).astype(BF16)


def retention_sample(u_s, state, tables):
    cd, dec, qd, kd = tables
    rows = RS_BB * DEC_SEQ
    per_head = lambda i, h: (h, 0, 0)
    st_spec = pl.BlockSpec((RS_BB, None, RET_DK, RET_DV), lambda i, h: (i, h, 0, 0))
    return pl.pallas_call(
        _ret_sample_kernel,
        grid=(DEC_BATCH // RS_BB, N_RET_HEADS),
        in_specs=[
            pl.BlockSpec(memory_space=pltpu.SMEM),
            pl.BlockSpec((1, rows, rows), per_head),
            pl.BlockSpec((1, rows, 1), per_head),
            pl.BlockSpec((1, rows, 1), per_head),
            pl.BlockSpec((rows, RET_DK), lambda i, h: (i, COL_QR // RET_DK + h)),
            pl.BlockSpec((rows, RET_DK), lambda i, h: (i, COL_KR // RET_DK + h)),
            pl.BlockSpec((rows, RET_DV), lambda i, h: (i, COL_VR // RET_DV + h)),
            pl.BlockSpec((rows, RET_DV), lambda i, h: (i, COL_GR // RET_DV + h)),
            st_spec,
        ],
        out_specs=[pl.BlockSpec((rows, RET_DV), lambda i, h: (i, h)), st_spec],
        out_shape=[
            jax.ShapeDtypeStruct((T_SAMPLE, 2048), BF16),
            jax.ShapeDtypeStruct((DEC_BATCH, N_RET_HEADS, RET_DK, RET_DV), F32),
        ],
        compiler_params=_params(("parallel", "parallel")),
        name="retention_sample",
    )(cd, dec, qd, kd, u_s, u_s, u_s, u_s, state)


def _swa_prompt_kernel(slope_ref, sink_ref, q_ref, kvc_ref, kvp_ref, o_ref):
    j = pl.program_id(1)
    shape = (WINDOW, 2 * WINDOW)
    qi = lax.broadcasted_iota(jnp.int32, shape, 0) + WINDOW
    ki = lax.broadcasted_iota(jnp.int32, shape, 1)
    dist = qi - ki
    valid = (dist >= 0) & (dist < WINDOW) & jnp.logical_not((j == 0) & (ki < WINDOW))
    dist_f = dist.astype(F32)
    kv_w = N_SWA_KV_HEADS * SWA_HEAD_DIM
    for kvh in range(N_SWA_KV_HEADS):
        lo = kvh * SWA_HEAD_DIM
        k_cat = jnp.concatenate([kvp_ref[:, lo:lo + SWA_HEAD_DIM],
                                 kvc_ref[:, lo:lo + SWA_HEAD_DIM]], axis=0).astype(BF16)
        v_cat = jnp.concatenate([kvp_ref[:, kv_w + lo:kv_w + lo + SWA_HEAD_DIM],
                                 kvc_ref[:, kv_w + lo:kv_w + lo + SWA_HEAD_DIM]], axis=0).astype(BF16)
        for g in range(SWA_GROUP):
            hh = kvh * SWA_GROUP + g
            q = q_ref[:, hh * SWA_HEAD_DIM:(hh + 1) * SWA_HEAD_DIM].astype(BF16)
            s = _dot_nt(q, k_cat) * (SWA_HEAD_DIM ** -0.5)
            s = jnp.where(valid, s - slope_ref[hh] * dist_f, NEG_INF)
            sink = sink_ref[hh]
            m = jnp.maximum(jnp.max(s, axis=-1, keepdims=True), sink)
            e = jnp.exp(s - m)
            den = jnp.sum(e, axis=-1, keepdims=True) + jnp.exp(sink - m)
            p = (e / den).astype(BF16)
            o = jnp.dot(p, v_cat, preferred_element_type=F32)
            o_ref[:, hh * SWA_HEAD_DIM:(hh + 1) * SWA_HEAD_DIM] = o.astype(BF16)


def swa_prompt(u, slopes, sinks):
    nb = SEQ // WINDOW
    kv_blk = COL_KVS // 512
    return pl.pallas_call(
        _swa_prompt_kernel,
        grid=(BATCH, nb),
        in_specs=[
            pl.BlockSpec(memory_space=pltpu.SMEM),
            pl.BlockSpec(memory_space=pltpu.SMEM),
            pl.BlockSpec((WINDOW, 2048), lambda b, j: (b * nb + j, COL_QS // 2048)),
            pl.BlockSpec((WINDOW, 512), lambda b, j: (b * nb + j, kv_blk)),
            pl.BlockSpec((WINDOW, 512), lambda b, j: (b * nb + jnp.maximum(j - 1, 0), kv_blk)),
        ],
        out_specs=pl.BlockSpec((WINDOW, 2048), lambda b, j: (b * nb + j, 0)),
        out_shape=jax.ShapeDtypeStruct((T_PROMPT, 2048), BF16),
        compiler_params=_params(("parallel", "parallel")),
        name="swa_prompt",
    )(slopes, sinks, u, u, u)


SS_BB = 8
SS_ROWS = DEC_SEQ * SWA_GROUP


def _bf16_round(x):
    return x.astype(BF16).astype(F32)


def _swa_sample_kernel(slope_ref, sink_ref, q_ref, kvn_ref, ck_ref, cv_ref, o_ref, cko_ref, cvo_ref):
    kv_w = N_SWA_KV_HEADS * SWA_HEAD_DIM
    t_row = lax.broadcasted_iota(jnp.int32, (SS_ROWS, WINDOW), 0) // SWA_GROUP
    jcol = lax.broadcasted_iota(jnp.int32, (SS_ROWS, WINDOW), 1)
    dist1 = (t_row + WINDOW - jcol).astype(F32)
    valid1 = jcol > t_row
    t_col = lax.broadcasted_iota(jnp.int32, (SS_ROWS, 1), 0) // SWA_GROUP
    scale = SWA_HEAD_DIM ** -0.5
    for b in range(SS_BB):
        for kvh in range(N_SWA_KV_HEADS):
            lo = kvh * SWA_HEAD_DIM
            slope = slope_ref[kvh]
            sink = sink_ref[kvh]
            q = q_ref[b, kvh]
            ck = ck_ref[b, :, lo:lo + SWA_HEAD_DIM].astype(BF16)
            cv = cv_ref[b, :, lo:lo + SWA_HEAD_DIM].astype(BF16)
            kn = _bf16_round(kvn_ref[b, :, lo:lo + SWA_HEAD_DIM])
            vn = _bf16_round(kvn_ref[b, :, kv_w + lo:kv_w + lo + SWA_HEAD_DIM])
            s1 = _dot_nt(q.astype(BF16), ck) * scale
            s1 = jnp.where(valid1, s1 - slope * dist1, NEG_INF)
            qr = _bf16_round(q)
            s2 = []
            for t2 in range(DEC_SEQ):
                st = jnp.sum(qr * kn[t2:t2 + 1], axis=-1, keepdims=True) * scale
                d2 = (t_col - t2).astype(F32)
                s2.append(jnp.where(t_col >= t2, st - slope * d2, NEG_INF))
            m = jnp.maximum(jnp.max(s1, axis=-1, keepdims=True), sink)
            for st in s2:
                m = jnp.maximum(m, st)
            e1 = jnp.exp(s1 - m)
            e2 = [jnp.exp(st - m) for st in s2]
            den = jnp.sum(e1, axis=-1, keepdims=True) + jnp.exp(sink - m)
            for et in e2:
                den = den + et
            inv = 1.0 / den
            o = jnp.dot((e1 * inv).astype(BF16), cv, preferred_element_type=F32)
            for t2 in range(DEC_SEQ):
                o = o + _bf16_round(e2[t2] * inv) * vn[t2:t2 + 1]
            o_ref[b, kvh] = o
        keep = WINDOW - DEC_SEQ
        cko_ref[b, 0:keep, :] = ck_ref[b, DEC_SEQ:WINDOW, :]
        cko_ref[b, keep:WINDOW, :] = kvn_ref[b, :, 0:kv_w]
        cvo_ref[b, 0:keep, :] = cv_ref[b, DEC_SEQ:WINDOW, :]
        cvo_ref[b, keep:WINDOW, :] = kvn_ref[b, :, kv_w:2 * kv_w]


def swa_sample(q_t, kv_new, cache_k, cache_v, slope_rows, sink_rows):
    kv_w = N_SWA_KV_HEADS * SWA_HEAD_DIM
    whole3 = lambda i: (0, 0, 0)
    cache_spec = pl.BlockSpec((SS_BB, WINDOW, kv_w), lambda i: (i, 0, 0))
    return pl.pallas_call(
        _swa_sample_kernel,
        grid=(DEC_BATCH // SS_BB,),
        in_specs=[
            pl.BlockSpec((N_SWA_KV_HEADS, SS_ROWS, 1), whole3),
            pl.BlockSpec((N_SWA_KV_HEADS, SS_ROWS, 1), whole3),
            pl.BlockSpec((SS_BB, N_SWA_KV_HEADS, SS_ROWS, SWA_HEAD_DIM), lambda i: (i, 0, 0, 0)),
            pl.BlockSpec((SS_BB, DEC_SEQ, 2 * kv_w), lambda i: (i, 0, 0)),
            cache_spec, cache_spec,
        ],
        out_specs=[
            pl.BlockSpec((SS_BB, N_SWA_KV_HEADS, SS_ROWS, SWA_HEAD_DIM), lambda i: (i, 0, 0, 0)),
            cache_spec, cache_spec,
        ],
        out_shape=[
            jax.ShapeDtypeStruct((DEC_BATCH, N_SWA_KV_HEADS, SS_ROWS, SWA_HEAD_DIM), F32),
            jax.ShapeDtypeStruct((DEC_BATCH, WINDOW, kv_w), F32),
            jax.ShapeDtypeStruct((DEC_BATCH, WINDOW, kv_w), F32),
        ],
        compiler_params=_params(("parallel",)),
        name="swa_sample",
    )(slope_rows, sink_rows, q_t, kv_new, cache_k, cache_v)


def _merge_kernel(or_ref, os_ref, gr_ref, gs_ref, wr_ref, ws_ref, o_ref):
    a = jnp.dot(or_ref[...], wr_ref[...], preferred_element_type=F32)
    b = jnp.dot(os_ref[...], ws_ref[...], preferred_element_type=F32)
    merged = jax.nn.sigmoid(gr_ref[...]) * a + jax.nn.sigmoid(gs_ref[...]) * b
    o_ref[...] = merged.astype(BF16)


def merge_branches(o_r, o_s, u, w_ret_o_bf, w_swa_o_bf, tm, tn):
    m = o_r.shape[0]
    nj = D_MODEL // tn
    return pl.pallas_call(
        _merge_kernel,
        grid=(m // tm, nj),
        in_specs=[
            pl.BlockSpec((tm, 2048), lambda i, j: (i, 0)),
            pl.BlockSpec((tm, 2048), lambda i, j: (i, 0)),
            pl.BlockSpec((tm, tn), lambda i, j: (i, COL_GATE_R // tn + j)),
            pl.BlockSpec((tm, tn), lambda i, j: (i, COL_GATE_S // tn + j)),
            pl.BlockSpec((2048, tn), lambda i, j: (0, j)),
            pl.BlockSpec((2048, tn), lambda i, j: (0, j)),
        ],
        out_specs=pl.BlockSpec((tm, tn), lambda i, j: (i, j)),
        out_shape=jax.ShapeDtypeStruct((m, D_MODEL), BF16),
        compiler_params=_params(("parallel", "parallel")),
        name="merge_branches",
    )(o_r, o_s, u, u, w_ret_o_bf, w_swa_o_bf)


ROUTER_PAD = 128


def _post_mix_kernel(alpha, m_ref, x_ref, w_ref, g_ref, b_ref, wr_ref, br_ref, *refs):
    h_ref, hb_ref, lg_ref = refs[-3:]
    mix = jnp.dot(m_ref[...], w_ref[...], preferred_element_type=F32)
    h = _layer_norm(alpha * x_ref[...] + mix, g_ref[...], b_ref[...])
    h_ref[...] = h
    hb_ref[...] = h.astype(BF16)
    lg_ref[...] = jnp.dot(h, wr_ref[...], preferred_element_type=F32,
                          precision=lax.Precision.HIGHEST) + br_ref[...]


def post_mix(merged, x2d, w_out_bf, ln_g, ln_b, w_router_pad, b_router_pad, alpha, row_off, prev, tm):
    m = merged.shape[0]
    blk_off = row_off // tm
    row = lambda i: (i, 0)
    whole = lambda i: (0, 0)
    out_row = lambda i: (blk_off + i, 0)
    in_specs = [
        pl.BlockSpec((tm, 2048), row),
        pl.BlockSpec((tm, 2048), row),
        pl.BlockSpec((2048, 2048), whole),
        pl.BlockSpec((1, 2048), whole),
        pl.BlockSpec((1, 2048), whole),
        pl.BlockSpec((2048, ROUTER_PAD), whole),
        pl.BlockSpec((1, ROUTER_PAD), whole),
    ]
    args = [merged, x2d, w_out_bf, ln_g, ln_b, w_router_pad, b_router_pad]
    aliases = {}
    if prev is not None:
        in_specs += [pl.BlockSpec(memory_space=pl.ANY)] * 3
        aliases = {len(args): 0, len(args) + 1: 1, len(args) + 2: 2}
        args += list(prev)
    return pl.pallas_call(
        functools.partial(_post_mix_kernel, alpha),
        grid=(m // tm,),
        in_specs=in_specs,
        out_specs=[pl.BlockSpec((tm, 2048), out_row), pl.BlockSpec((tm, 2048), out_row),
                   pl.BlockSpec((tm, ROUTER_PAD), out_row)],
        out_shape=[jax.ShapeDtypeStruct((T_ALL, D_MODEL), F32),
                   jax.ShapeDtypeStruct((T_ALL, D_MODEL), BF16),
                   jax.ShapeDtypeStruct((T_ALL, ROUTER_PAD), F32)],
        input_output_aliases=aliases,
        compiler_params=_params(("parallel",)),
        name="post_mix",
    )(*args)


def _moe_used(c, j, nu_ref, nj):
    return jnp.where(c < nu_ref[0], j, nj - 1)


def _moe_up_kernel(ce_ref, cs_ref, cn_ref, cb_ref, nu_ref, x_ref, wg_ref, wu_ref, bg_ref, bu_ref,
                   act_ref, wg_bf, wu_bf):
    c = pl.program_id(0)
    nblk = cn_ref[c]

    @pl.when(nblk > 0)
    def _():
        wg_bf[...] = wg_ref[0].astype(BF16)
        wu_bf[...] = wu_ref[0].astype(BF16)

        def body(r, carry):
            rows = pl.ds(pl.multiple_of(r * MOE_SUB, MOE_SUB), MOE_SUB)
            xr = x_ref[rows, :]
            g = jnp.dot(xr, wg_bf[...], preferred_element_type=F32) + bg_ref[0]
            up = jnp.dot(xr, wu_bf[...], preferred_element_type=F32) + bu_ref[0]
            g = jnp.minimum(g, SWIGLU_LIMIT)
            up = jnp.clip(up, -SWIGLU_LIMIT, SWIGLU_LIMIT)
            glu = g * jax.nn.sigmoid(SWIGLU_ALPHA * g)
            act_ref[rows, :] = ((up + 1.0) * glu).astype(BF16)
            return carry

        lax.fori_loop(0, nblk, body, 0)

        def zero(r, carry):
            rows = pl.ds(pl.multiple_of(r * MOE_SUB, MOE_SUB), MOE_SUB)
            act_ref[rows, :] = jnp.zeros((MOE_SUB, MOE_TN), BF16)
            return carry

        lax.fori_loop(nblk, MOE_CHUNK_SUBS, zero, 0)


def _moe_down_kernel(ce_ref, cs_ref, cn_ref, cb_ref, nu_ref, a_ref, wd_ref, bd_ref, y_ref, wd_bf):
    c = pl.program_id(0)
    nblk = cn_ref[c]

    @pl.when(nblk > 0)
    def _():
        wd_bf[...] = wd_ref[0].astype(BF16)

        def body(r, carry):
            rows = pl.ds(pl.multiple_of(r * MOE_SUB, MOE_SUB), MOE_SUB)
            y_ref[rows, :] = jnp.dot(a_ref[rows, :], wd_bf[...], preferred_element_type=F32) + bd_ref[0]
            return carry

        lax.fori_loop(0, nblk, body, 0)

        def zero(r, carry):
            rows = pl.ds(pl.multiple_of(r * MOE_SUB, MOE_SUB), MOE_SUB)
            y_ref[rows, :] = jnp.zeros((MOE_SUB, MOE_TN), F32)
            return carry

        lax.fori_loop(nblk, MOE_CHUNK_SUBS, zero, 0)


def moe_experts(x_sorted, chunk_tables, w_gate_up, b_gate_up, w_down, b_down):
    nj = MOE_NJ

    def used_j(c, j, nu):
        return jnp.where(c < nu[0], j, nj - 1)

    act = pl.pallas_call(
        _moe_up_kernel,
        grid_spec=pltpu.PrefetchScalarGridSpec(
            num_scalar_prefetch=5,
            grid=(MOE_NC, nj),
            in_specs=[
                pl.BlockSpec((pl.Element(MOE_R), pl.Element(D_MODEL)),
                             lambda c, j, ce, cs, cn, cb, nu: (cs[c] * MOE_SUB, 0)),
                pl.BlockSpec((1, D_MODEL, MOE_TN),
                             lambda c, j, ce, cs, cn, cb, nu: (ce[c], 0, used_j(c, j, nu))),
                pl.BlockSpec((1, D_MODEL, MOE_TN),
                             lambda c, j, ce, cs, cn, cb, nu: (ce[c], 0, nj + used_j(c, j, nu))),
                pl.BlockSpec((1, 1, MOE_TN),
                             lambda c, j, ce, cs, cn, cb, nu: (ce[c], 0, used_j(c, j, nu))),
                pl.BlockSpec((1, 1, MOE_TN),
                             lambda c, j, ce, cs, cn, cb, nu: (ce[c], 0, nj + used_j(c, j, nu))),
            ],
            out_specs=pl.BlockSpec((MOE_R, MOE_TN),
                                   lambda c, j, ce, cs, cn, cb, nu: (cb[c], used_j(c, j, nu))),
            scratch_shapes=[pltpu.VMEM((D_MODEL, MOE_TN), BF16), pltpu.VMEM((D_MODEL, MOE_TN), BF16)],
        ),
        out_shape=jax.ShapeDtypeStruct((MOE_NC * MOE_R, D_FF), BF16),
        compiler_params=_params(("arbitrary", "arbitrary")),
        name="moe_up",
    )(*chunk_tables, x_sorted, w_gate_up, w_gate_up, b_gate_up, b_gate_up)

    y = pl.pallas_call(
        _moe_down_kernel,
        grid_spec=pltpu.PrefetchScalarGridSpec(
            num_scalar_prefetch=5,
            grid=(MOE_NC, nj),
            in_specs=[
                pl.BlockSpec((MOE_R, D_FF), lambda c, j, ce, cs, cn, cb, nu: (cb[c], 0)),
                pl.BlockSpec((1, D_FF, MOE_TN),
                             lambda c, j, ce, cs, cn, cb, nu: (ce[c], 0, used_j(c, j, nu))),
                pl.BlockSpec((1, 1, MOE_TN),
                             lambda c, j, ce, cs, cn, cb, nu: (ce[c], 0, used_j(c, j, nu))),
            ],
            out_specs=pl.BlockSpec((MOE_R, MOE_TN),
                                   lambda c, j, ce, cs, cn, cb, nu: (cb[c], used_j(c, j, nu))),
            scratch_shapes=[pltpu.VMEM((D_FF, MOE_TN), BF16)],
        ),
        out_shape=jax.ShapeDtypeStruct((MOE_NC * MOE_R, D_MODEL), F32),
        compiler_params=_params(("arbitrary", "arbitrary")),
        name="moe_down",
    )(*chunk_tables, act, w_down, b_down)
    return y


def route(logits):
    top_val, top_idx = lax.top_k(logits, TOP_K)
    gate = jax.nn.softmax(top_val, axis=-1)
    e_flat = top_idx.reshape(TK_ALL)
    onehot = (e_flat[:, None] == jnp.arange(N_EXPERTS, dtype=e_flat.dtype)[None, :]).astype(jnp.int32)
    csum = jnp.cumsum(onehot, axis=0)
    rank = jnp.sum(csum * onehot, axis=1) - 1
    counts = csum[-1]
    nsub = (counts + MOE_SUB - 1) // MOE_SUB
    sub_base = jnp.cumsum(nsub) - nsub
    nch = (counts + MOE_R - 1) // MOE_R
    ch_end = jnp.cumsum(nch)
    ch_base = ch_end - nch
    n_used = ch_end[-1]
    xpos = MOE_SUB * sub_base[e_flat] + rank
    ypos = MOE_R * (ch_base[e_flat] + rank // MOE_R) + rank % MOE_R
    tok = jnp.arange(TK_ALL, dtype=jnp.int32) // TOP_K
    row_tok = jnp.zeros((MOE_XROWS,), jnp.int32).at[xpos].set(tok)
    c = jnp.arange(MOE_NC, dtype=jnp.int32)
    cc = jnp.minimum(c, n_used - 1)
    ce = jnp.minimum(jnp.sum(cc[:, None] >= ch_end[None, :], axis=1), N_EXPERTS - 1).astype(jnp.int32)
    kk = cc - ch_base[ce]
    cs = (sub_base[ce] + MOE_CHUNK_SUBS * kk).astype(jnp.int32)
    cn = jnp.where(c < n_used, jnp.minimum(MOE_CHUNK_SUBS, nsub[ce] - MOE_CHUNK_SUBS * kk), 0).astype(jnp.int32)
    tables = (ce, cs, cn, cc.astype(jnp.int32), n_used.reshape(1).astype(jnp.int32))
    return gate, row_tok, ypos.reshape(T_ALL, TOP_K), tables


def _combine_kernel(alpha, h_ref, yg_ref, gate_ref, g_ref, b_ref, o_ref):
    gate = gate_ref[...]
    f = yg_ref[0] * gate[:, 0:1]
    for k in range(1, TOP_K):
        f = f + yg_ref[k] * gate[:, k:k + 1]
    o_ref[...] = _layer_norm(alpha * h_ref[...] + f, g_ref[...], b_ref[...])


def combine(h, yg, gate, ln_g, ln_b, alpha, row_off, m, tm):
    blk_off = row_off // tm
    whole = lambda i: (0, 0)
    return pl.pallas_call(
        functools.partial(_combine_kernel, alpha),
        grid=(m // tm,),
        in_specs=[
            pl.BlockSpec((tm, D_MODEL), lambda i: (blk_off + i, 0)),
            pl.BlockSpec((TOP_K, tm, D_MODEL), lambda i: (0, blk_off + i, 0)),
            pl.BlockSpec((tm, TOP_K), lambda i: (blk_off + i, 0)),
            pl.BlockSpec((1, D_MODEL), whole),
            pl.BlockSpec((1, D_MODEL), whole),
        ],
        out_specs=pl.BlockSpec((tm, D_MODEL), lambda i: (i, 0)),
        out_shape=jax.ShapeDtypeStruct((m, D_MODEL), F32),
        compiler_params=_params(("parallel",)),
        name="combine",
    )(h, yg, gate, ln_g, ln_b)


def _retention_tables(chunk, group):
    log_gamma = jnp.log(1.0 - jnp.exp2(-5.0 - jnp.arange(N_RET_HEADS, dtype=F32)))
    n = chunk * group
    pos = jnp.arange(n) % chunk
    seq = jnp.arange(n) // chunk
    diff = (pos[:, None] - pos[None, :]).astype(F32)
    same = seq[:, None] == seq[None, :]
    lg = log_gamma[:, None, None]
    dec = jnp.where(same[None] & (diff >= 0)[None], jnp.exp(lg * jnp.maximum(diff, 0.0)[None]), 0.0)
    posf = pos.astype(F32)
    qd = jnp.exp(log_gamma[:, None] * (posf + 1.0))[..., None]
    kd = jnp.exp(log_gamma[:, None] * (chunk - 1.0 - posf))[..., None]
    cd = jnp.exp(log_gamma * chunk)
    return cd, dec, qd, kd


def _alibi_slopes():
    h = jnp.arange(1, N_SWA_HEADS + 1, dtype=F32)
    return jnp.exp2(-8.0 * h / N_SWA_HEADS)


def kernel(x_prompt, x_sample, state_ret, cache_swa_k, cache_swa_v, w_in, w_ret_o, w_swa_o, w_out, swa_sinks,
           ln1_g, ln1_b, w_router, b_router, w_gate_up, b_gate_up, w_down, b_down, ln2_g, ln2_b):
    alpha = (2.0 * DEPTH) ** 0.25
    kv_w = N_SWA_KV_HEADS * SWA_HEAD_DIM
    w = w_in[0]
    w_in_bf = jnp.concatenate([w[:, :8192], w[:, 8704:], w[:, 8192:8704]], axis=1).astype(BF16)
    w_ret_o_bf = w_ret_o[0].astype(BF16)
    w_swa_o_bf = w_swa_o[0].astype(BF16)
    w_out_bf = w_out[0].astype(BF16)
    w_router_pad = jnp.pad(w_router[0], ((0, 0), (0, ROUTER_PAD - N_EXPERTS)))
    b_router_pad = jnp.pad(b_router[0], (0, ROUTER_PAD - N_EXPERTS)).reshape(1, ROUTER_PAD)
    ln1g, ln1b = ln1_g[0].reshape(1, D_MODEL), ln1_b[0].reshape(1, D_MODEL)
    ln2g, ln2b = ln2_g[0].reshape(1, D_MODEL), ln2_b[0].reshape(1, D_MODEL)
    slopes = _alibi_slopes()
    sinks = swa_sinks[0].astype(F32)

    xp = x_prompt.reshape(T_PROMPT, D_MODEL)
    xs = x_sample.reshape(T_SAMPLE, D_MODEL)

    u_p = in_proj(xp, w_in_bf, 1024, 1280)
    o_r_p, state_p = retention_prompt(u_p, _retention_tables(RET_CHUNK, 1))
    o_s_p = swa_prompt(u_p, slopes, sinks)
    kv_p = u_p.reshape(BATCH, SEQ, D_IN)[:, SEQ - WINDOW:, COL_KVS:]
    k_cache_p = kv_p[..., :kv_w].reshape(1, BATCH, WINDOW, N_SWA_KV_HEADS, SWA_HEAD_DIM)
    v_cache_p = kv_p[..., kv_w:].reshape(1, BATCH, WINDOW, N_SWA_KV_HEADS, SWA_HEAD_DIM)
    merged_p = merge_branches(o_r_p, o_s_p, u_p, w_ret_o_bf, w_swa_o_bf, 512, 1024)

    u_s = in_proj(xs, w_in_bf, 512, 1280)
    o_r_s, state_s = retention_sample(u_s, state_ret[0], _retention_tables(DEC_SEQ, RS_BB))
    q_t = (u_s[:, COL_QS:COL_QS + 2048]
           .reshape(DEC_BATCH, DEC_SEQ, N_SWA_KV_HEADS, SWA_GROUP, SWA_HEAD_DIM)
           .transpose(0, 2, 1, 3, 4).reshape(DEC_BATCH, N_SWA_KV_HEADS, SS_ROWS, SWA_HEAD_DIM))
    kv_new = u_s[:, COL_KVS:].reshape(DEC_BATCH, DEC_SEQ, 2 * kv_w)
    slope_rows = jnp.tile(slopes.reshape(N_SWA_KV_HEADS, 1, SWA_GROUP), (1, DEC_SEQ, 1)).reshape(
        N_SWA_KV_HEADS, SS_ROWS, 1)
    sink_rows = jnp.tile(sinks.reshape(N_SWA_KV_HEADS, 1, SWA_GROUP), (1, DEC_SEQ, 1)).reshape(
        N_SWA_KV_HEADS, SS_ROWS, 1)
    o_s_t, k_cache_s, v_cache_s = swa_sample(
        q_t, kv_new, cache_swa_k[0].reshape(DEC_BATCH, WINDOW, kv_w),
        cache_swa_v[0].reshape(DEC_BATCH, WINDOW, kv_w), slope_rows, sink_rows)
    o_s_s = (o_s_t.reshape(DEC_BATCH, N_SWA_KV_HEADS, DEC_SEQ, SWA_GROUP, SWA_HEAD_DIM)
             .transpose(0, 2, 1, 3, 4).reshape(T_SAMPLE, 2048).astype(BF16))
    merged_s = merge_branches(o_r_s, o_s_s, u_s, w_ret_o_bf, w_swa_o_bf, 512, 1024)

    outs = post_mix(merged_p, xp, w_out_bf, ln1g, ln1b, w_router_pad, b_router_pad, alpha, 0, None, 256)
    h, h_bf, logits = post_mix(merged_s, xs, w_out_bf, ln1g, ln1b, w_router_pad, b_router_pad, alpha,
                               T_PROMPT, outs, 256)

    gate, row_tok, ypos, tables = route(logits[:, :N_EXPERTS])
    x_sorted = h_bf[row_tok]
    y_rows = moe_experts(x_sorted, tables, w_gate_up[0], b_gate_up[0].reshape(N_EXPERTS, 1, 2 * D_FF),
                         w_down[0], b_down[0].reshape(N_EXPERTS, 1, D_MODEL))
    yg = y_rows[ypos.T]
    y_p = combine(h, yg, gate, ln2g, ln2b, alpha, 0, T_PROMPT, 256)
    y_s = combine(h, yg, gate, ln2g, ln2b, alpha, T_PROMPT, T_SAMPLE, 256)

    return (y_p.reshape(BATCH, SEQ, D_MODEL), y_s.reshape(DEC_BATCH, DEC_SEQ, D_MODEL),
            state_p[None], k_cache_p, v_cache_p,
            state_s[None], k_cache_s.reshape(1, DEC_BATCH, WINDOW, N_SWA_KV_HEADS, SWA_HEAD_DIM),
            v_cache_s.reshape(1, DEC_BATCH, WINDOW, N_SWA_KV_HEADS, SWA_HEAD_DIM))
```

```python
import functools

import jax
import jax.numpy as jnp
from jax import lax
from jax.experimental import pallas as pl
from jax.experimental.pallas import tpu as pltpu

F32 = jnp.float32
BF16 = jnp.bfloat16

D_MODEL = 2048
BATCH = 4
SEQ = 2048
DEC_BATCH = 128
DEC_SEQ = 4
N_RET_HEADS = 8
RET_DK = 128
RET_DV = 256
RET_CHUNK = 128
N_SWA_HEADS = 32
N_SWA_KV_HEADS = 4
SWA_GROUP = 8
SWA_HEAD_DIM = 64
WINDOW = 128
N_EXPERTS = 32
TOP_K = 4
D_FF = 2048
SWIGLU_LIMIT = 7.0
SWIGLU_ALPHA = 1.702
LN_EPS = 1e-5
GN_EPS = 1e-6
NEG_INF = -1e30
DEPTH = 1

T_PROMPT = BATCH * SEQ
T_SAMPLE = DEC_BATCH * DEC_SEQ
T_ALL = T_PROMPT + T_SAMPLE
TK_ALL = T_ALL * TOP_K
D_IN = 12800

COL_QR, COL_KR, COL_VR, COL_GR, COL_QS, COL_GATE_R, COL_GATE_S, COL_KVS = (
    0, 1024, 2048, 4096, 6144, 8192, 10240, 12288)

MOE_SUB = 256
MOE_CHUNK_SUBS = 5
MOE_R = MOE_SUB * MOE_CHUNK_SUBS
MOE_NSUB_MAX = TK_ALL // MOE_SUB + N_EXPERTS
MOE_XROWS = MOE_NSUB_MAX * MOE_SUB + MOE_R
MOE_NC = TK_ALL // MOE_R + N_EXPERTS
MOE_TN = 512
MOE_NJ = D_FF // MOE_TN

VMEM_LIMIT = 56 * 1024 * 1024


def _params(sem):
    return pltpu.CompilerParams(dimension_semantics=sem, vmem_limit_bytes=VMEM_LIMIT)


def _in_proj_kernel(x_ref, w_ref, o_ref):
    o_ref[...] = jnp.dot(x_ref[...].astype(BF16), w_ref[...], preferred_element_type=F32)


def in_proj(x2d, w_bf, tm, tn):
    m, k = x2d.shape
    n = w_bf.shape[1]
    return pl.pallas_call(
        _in_proj_kernel,
        grid=(m // tm, n // tn),
        in_specs=[pl.BlockSpec((tm, k), lambda i, j: (i, 0)),
                  pl.BlockSpec((k, tn), lambda i, j: (0, j))],
        out_specs=pl.BlockSpec((tm, tn), lambda i, j: (i, j)),
        out_shape=jax.ShapeDtypeStruct((m, n), F32),
        compiler_params=_params(("parallel", "parallel")),
        name="in_proj",
    )(x2d, w_bf)


def _group_norm_gate(o, g):
    mu = jnp.mean(o, axis=-1, keepdims=True)
    oc = o - mu
    var = jnp.mean(oc * oc, axis=-1, keepdims=True)
    return oc * lax.rsqrt(var + GN_EPS) * (g * jax.nn.sigmoid(g))


def _dot_nt(a, b):
    return lax.dot_general(a, b, (((1,), (1,)), ((), ())), preferred_element_type=F32)


def _layer_norm(x, g, b):
    mu = jnp.mean(x, axis=-1, keepdims=True)
    xc = x - mu
    var = jnp.mean(xc * xc, axis=-1, keepdims=True)
    return xc * lax.rsqrt(var + LN_EPS) * g + b


def _ret_prompt_kernel(cd_ref, dec_ref, qd_ref, kd_ref, q_ref, k_ref, v_ref, g_ref, o_ref, st_ref):
    c = pl.program_id(1)

    @pl.when(c == 0)
    def _():
        st_ref[...] = jnp.zeros_like(st_ref)

    for h in range(N_RET_HEADS):
        q = q_ref[:, h * RET_DK:(h + 1) * RET_DK]
        k = k_ref[:, h * RET_DK:(h + 1) * RET_DK] * (RET_DK ** -0.5)
        v = v_ref[:, h * RET_DV:(h + 1) * RET_DV].astype(BF16)
        state = st_ref[0, h]
        scores = _dot_nt(q.astype(BF16), k.astype(BF16)) * dec_ref[h]
        o = (jnp.dot(scores.astype(BF16), v, preferred_element_type=F32)
             + jnp.dot((q * qd_ref[h]).astype(BF16), state.astype(BF16), preferred_element_type=F32))
        kt = (k * kd_ref[h]).T.astype(BF16)
        st_ref[0, h] = cd_ref[h] * state + jnp.dot(kt, v, preferred_element_type=F32)
        g = g_ref[:, h * RET_DV:(h + 1) * RET_DV]
        o_ref[:, h * RET_DV:(h + 1) * RET_DV] = _group_norm_gate(o, g).astype(BF16)


def retention_prompt(u, tables):
    cd, dec, qd, kd = tables
    nc = SEQ // RET_CHUNK
    row = lambda b, c: b * nc + c
    whole3 = lambda b, c: (0, 0, 0)
    return pl.pallas_call(
        _ret_prompt_kernel,
        grid=(BATCH, nc),
        in_specs=[
            pl.BlockSpec(memory_space=pltpu.SMEM),
            pl.BlockSpec((N_RET_HEADS, RET_CHUNK, RET_CHUNK), whole3),
            pl.BlockSpec((N_RET_HEADS, RET_CHUNK, 1), whole3),
            pl.BlockSpec((N_RET_HEADS, RET_CHUNK, 1), whole3),
            pl.BlockSpec((RET_CHUNK, 1024), lambda b, c: (row(b, c), COL_QR // 1024)),
            pl.BlockSpec((RET_CHUNK, 1024), lambda b, c: (row(b, c), COL_KR // 1024)),
            pl.BlockSpec((RET_CHUNK, 2048), lambda b, c: (row(b, c), COL_VR // 2048)),
            pl.BlockSpec((RET_CHUNK, 2048), lambda b, c: (row(b, c), COL_GR // 2048)),
        ],
        out_specs=[
            pl.BlockSpec((RET_CHUNK, 2048), lambda b, c: (row(b, c), 0)),
            pl.BlockSpec((1, N_RET_HEADS, RET_DK, RET_DV), lambda b, c: (b, 0, 0, 0)),
        ],
        out_shape=[
            jax.ShapeDtypeStruct((T_PROMPT, 2048), BF16),
            jax.ShapeDtypeStruct((BATCH, N_RET_HEADS, RET_DK, RET_DV), F32),
        ],
        compiler_params=_params(("parallel", "arbitrary")),
        name="retention_prompt",
    )(cd, dec, qd, kd, u, u, u, u)


RS_BB = 32


def _ret_sample_kernel(cd_ref, dec_ref, qd_ref, kd_ref, q_ref, k_ref, v_ref, g_ref, st_ref,
                       o_ref, sto_ref):
    h = pl.program_id(1)
    q = q_ref[...]
    k = k_ref[...] * (RET_DK ** -0.5)
    v = v_ref[...].astype(BF16)
    scores = _dot_nt(q.astype(BF16), k.astype(BF16)) * dec_ref[0]
    o_intra = jnp.dot(scores.astype(BF16), v, preferred_element_type=F32)
    qq = (q * qd_ref[0]).astype(BF16)
    kt = (k * kd_ref[0]).T
    col_batch = lax.broadcasted_iota(jnp.int32, kt.shape, 1) // DEC_SEQ
    first_of_pair = lax.broadcasted_iota(jnp.int32, (8, RET_DV), 0) < DEC_SEQ
    cd = cd_ref[h]
    pieces = []
    for p in range(RS_BB // 2):
        q8 = qq[8 * p:8 * p + 8]
        s0 = st_ref[2 * p].astype(BF16)
        s1 = st_ref[2 * p + 1].astype(BF16)
        r0 = jnp.dot(q8, s0, preferred_element_type=F32)
        r1 = jnp.dot(q8, s1, preferred_element_type=F32)
        pieces.append(jnp.where(first_of_pair, r0, r1))
    for b in range(RS_BB):
        ktm = jnp.where(col_batch == b, kt, 0.0).astype(BF16)
        sto_ref[b] = cd * st_ref[b] + jnp.dot(ktm, v, preferred_element_type=F32)
    o = o_intra + jnp.concatenate(pieces, axis=0)
    o_ref[...] = _group_norm_gate(o, g_ref[...]).astype(BF16)


def retention_sample(u_s, state, tables):
    cd, dec, qd, kd = tables
    rows = RS_BB * DEC_SEQ
    per_head = lambda i, h: (h, 0, 0)
    st_spec = pl.BlockSpec((RS_BB, None, RET_DK, RET_DV), lambda i, h: (i, h, 0, 0))
    return pl.pallas_call(
        _ret_sample_kernel,
        grid=(DEC_BATCH // RS_BB, N_RET_HEADS),
        in_specs=[
            pl.BlockSpec(memory_space=pltpu.SMEM),
            pl.BlockSpec((1, rows, rows), per_head),
            pl.BlockSpec((1, rows, 1), per_head),
            pl.BlockSpec((1, rows, 1), per_head),
            pl.BlockSpec((rows, RET_DK), lambda i, h: (i, COL_QR // RET_DK + h)),
            pl.BlockSpec((rows, RET_DK), lambda i, h: (i, COL_KR // RET_DK + h)),
            pl.BlockSpec((rows, RET_DV), lambda i, h: (i, COL_VR // RET_DV + h)),
            pl.BlockSpec((rows, RET_DV), lambda i, h: (i, COL_GR // RET_DV + h)),
            st_spec,
        ],
        out_specs=[pl.BlockSpec((rows, RET_DV), lambda i, h: (i, h)), st_spec],
        out_shape=[
            jax.ShapeDtypeStruct((T_SAMPLE, 2048), BF16),
            jax.ShapeDtypeStruct((DEC_BATCH, N_RET_HEADS, RET_DK, RET_DV), F32),
        ],
        compiler_params=_params(("parallel", "parallel")),
        name="retention_sample",
    )(cd, dec, qd, kd, u_s, u_s, u_s, u_s, state)


def _swa_prompt_kernel(sink_ref, bias_ref, q_ref, kvc_ref, kvp_ref, o_ref):
    kv_w = N_SWA_KV_HEADS * SWA_HEAD_DIM
    hd = SWA_HEAD_DIM
    qt = q_ref[...].T
    outs = []
    for kvh in range(N_SWA_KV_HEADS):
        lo = kvh * hd
        k_cat = jnp.concatenate([kvp_ref[:, lo:lo + hd], kvc_ref[:, lo:lo + hd]], axis=0).astype(BF16)
        v_cat = jnp.concatenate([kvp_ref[:, kv_w + lo:kv_w + lo + hd],
                                 kvc_ref[:, kv_w + lo:kv_w + lo + hd]], axis=0)
        vt = v_cat.T.astype(BF16)
        heads = range(kvh * SWA_GROUP, (kvh + 1) * SWA_GROUP)
        q8 = jnp.concatenate([qt[h * hd:(h + 1) * hd, :] for h in heads], axis=1).astype(BF16)
        s = jnp.dot(k_cat, q8, preferred_element_type=F32) * (hd ** -0.5) + bias_ref[0, kvh]
        sink = sink_ref[kvh]
        m = jnp.maximum(jnp.max(s, axis=0, keepdims=True), sink)
        e = jnp.exp(s - m)
        den = jnp.sum(e, axis=0, keepdims=True) + jnp.exp(sink - m)
        p = (e * (1.0 / den)).astype(BF16)
        ot = jnp.dot(vt, p, preferred_element_type=F32)
        outs += [ot[:, g * WINDOW:(g + 1) * WINDOW] for g in range(SWA_GROUP)]
    o_ref[...] = jnp.concatenate(outs, axis=0).T.astype(BF16)


def _swa_prompt_tables(slopes, sinks):
    qi = jnp.arange(WINDOW)[:, None] + WINDOW
    ki = jnp.arange(2 * WINDOW)[None, :]
    dist = qi - ki
    in_window = (dist >= 0) & (dist < WINDOW)
    valid = jnp.stack([in_window & (ki >= WINDOW), in_window])
    pen = -(slopes[:, None, None] * dist.astype(F32)[None])
    bias = jnp.where(valid[:, None], pen[None], NEG_INF)
    bias = bias.reshape(2, N_SWA_KV_HEADS, SWA_GROUP, WINDOW, 2 * WINDOW).transpose(0, 1, 4, 2, 3)
    bias = bias.reshape(2, N_SWA_KV_HEADS, 2 * WINDOW, SWA_GROUP * WINDOW)
    sink_l = jnp.repeat(sinks.reshape(N_SWA_KV_HEADS, SWA_GROUP), WINDOW, axis=1)
    return sink_l.reshape(N_SWA_KV_HEADS, 1, SWA_GROUP * WINDOW), bias


def swa_prompt(u, slopes, sinks):
    nb = SEQ // WINDOW
    kv_blk = COL_KVS // 512
    sink_l, bias = _swa_prompt_tables(slopes, sinks)
    lanes = SWA_GROUP * WINDOW
    return pl.pallas_call(
        _swa_prompt_kernel,
        grid=(BATCH, nb),
        in_specs=[
            pl.BlockSpec((N_SWA_KV_HEADS, 1, lanes), lambda b, j: (0, 0, 0)),
            pl.BlockSpec((1, N_SWA_KV_HEADS, 2 * WINDOW, lanes), lambda b, j: (jnp.minimum(j, 1), 0, 0, 0)),
            pl.BlockSpec((WINDOW, 2048), lambda b, j: (b * nb + j, COL_QS // 2048)),
            pl.BlockSpec((WINDOW, 512), lambda b, j: (b * nb + j, kv_blk)),
            pl.BlockSpec((WINDOW, 512), lambda b, j: (b * nb + jnp.maximum(j - 1, 0), kv_blk)),
        ],
        out_specs=pl.BlockSpec((WINDOW, 2048), lambda b, j: (b * nb + j, 0)),
        out_shape=jax.ShapeDtypeStruct((T_PROMPT, 2048), BF16),
        compiler_params=_params(("parallel", "parallel")),
        name="swa_prompt",
    )(sink_l, bias, u, u, u)


SS_BB = 8
SS_ROWS = DEC_SEQ * SWA_GROUP


def _bf16_round(x):
    return x.astype(BF16).astype(F32)


def _swa_sample_kernel(sink_ref, bias1_ref, bias2_ref, q_ref, kvn_ref, ck_ref, cv_ref,
                       o_ref, cko_ref, cvo_ref):
    kv_w = N_SWA_KV_HEADS * SWA_HEAD_DIM
    hd = SWA_HEAD_DIM
    scale = hd ** -0.5
    for kvh in range(N_SWA_KV_HEADS):
        lo = kvh * hd
        sink = sink_ref[kvh]
        q = q_ref[:, kvh]
        ck = ck_ref[:, :, lo:lo + hd].astype(BF16)
        cv = cv_ref[:, :, lo:lo + hd].astype(BF16)
        kn = _bf16_round(kvn_ref[:, :, lo:lo + hd])
        vn = _bf16_round(kvn_ref[:, :, kv_w + lo:kv_w + lo + hd])
        s1 = lax.dot_general(q.astype(BF16), ck, (((2,), (2,)), ((0,), (0,))),
                             preferred_element_type=F32) * scale + bias1_ref[kvh]
        qr = _bf16_round(q)
        s2 = [jnp.sum(qr * kn[:, t2:t2 + 1, :], axis=-1, keepdims=True) * scale + bias2_ref[kvh, t2]
              for t2 in range(DEC_SEQ)]
        m = jnp.maximum(jnp.max(s1, axis=-1, keepdims=True), sink)
        for st in s2:
            m = jnp.maximum(m, st)
        e1 = jnp.exp(s1 - m)
        e2 = [jnp.exp(st - m) for st in s2]
        den = jnp.sum(e1, axis=-1, keepdims=True) + jnp.exp(sink - m)
        for et in e2:
            den = den + et
        inv = 1.0 / den
        o = lax.dot_general((e1 * inv).astype(BF16), cv, (((2,), (1,)), ((0,), (0,))),
                            preferred_element_type=F32)
        for t2 in range(DEC_SEQ):
            o = o + _bf16_round(e2[t2] * inv) * vn[:, t2:t2 + 1, :]
        o_ref[:, kvh] = o
    keep = WINDOW - DEC_SEQ
    cko_ref[:, 0:keep, :] = ck_ref[:, DEC_SEQ:WINDOW, :]
    cko_ref[:, keep:WINDOW, :] = kvn_ref[:, :, 0:kv_w]
    cvo_ref[:, 0:keep, :] = cv_ref[:, DEC_SEQ:WINDOW, :]
    cvo_ref[:, keep:WINDOW, :] = kvn_ref[:, :, kv_w:2 * kv_w]


def _swa_sample_tables(slopes, sinks):
    g_slopes = jnp.tile(slopes.reshape(N_SWA_KV_HEADS, 1, SWA_GROUP), (1, DEC_SEQ, 1)).reshape(
        N_SWA_KV_HEADS, SS_ROWS)
    sink_rows = jnp.tile(sinks.reshape(N_SWA_KV_HEADS, 1, SWA_GROUP), (1, DEC_SEQ, 1)).reshape(
        N_SWA_KV_HEADS, SS_ROWS, 1)
    t_row = jnp.arange(SS_ROWS) // SWA_GROUP
    jcol = jnp.arange(WINDOW)
    dist1 = (t_row[:, None] + WINDOW - jcol[None, :])
    bias1 = jnp.where((dist1 < WINDOW)[None], -(g_slopes[:, :, None] * dist1.astype(F32)[None]), NEG_INF)
    t2 = jnp.arange(DEC_SEQ)
    dist2 = t_row[None, :] - t2[:, None]
    bias2 = jnp.where((dist2 >= 0)[None], -(g_slopes[:, None, :] * dist2.astype(F32)[None]), NEG_INF)
    return sink_rows, bias1, bias2[..., None]


def swa_sample(q_t, kv_new, cache_k, cache_v, tables):
    sink_rows, bias1, bias2 = tables
    kv_w = N_SWA_KV_HEADS * SWA_HEAD_DIM
    whole3 = lambda i: (0, 0, 0)
    cache_spec = pl.BlockSpec((SS_BB, WINDOW, kv_w), lambda i: (i, 0, 0))
    return pl.pallas_call(
        _swa_sample_kernel,
        grid=(DEC_BATCH // SS_BB,),
        in_specs=[
            pl.BlockSpec((N_SWA_KV_HEADS, SS_ROWS, 1), whole3),
            pl.BlockSpec((N_SWA_KV_HEADS, SS_ROWS, WINDOW), whole3),
            pl.BlockSpec((N_SWA_KV_HEADS, DEC_SEQ, SS_ROWS, 1), lambda i: (0, 0, 0, 0)),
            pl.BlockSpec((SS_BB, N_SWA_KV_HEADS, SS_ROWS, SWA_HEAD_DIM), lambda i: (i, 0, 0, 0)),
            pl.BlockSpec((SS_BB, DEC_SEQ, 2 * kv_w), lambda i: (i, 0, 0)),
            cache_spec, cache_spec,
        ],
        out_specs=[
            pl.BlockSpec((SS_BB, N_SWA_KV_HEADS, SS_ROWS, SWA_HEAD_DIM), lambda i: (i, 0, 0, 0)),
            cache_spec, cache_spec,
        ],
        out_shape=[
            jax.ShapeDtypeStruct((DEC_BATCH, N_SWA_KV_HEADS, SS_ROWS, SWA_HEAD_DIM), F32),
            jax.ShapeDtypeStruct((DEC_BATCH, WINDOW, kv_w), F32),
            jax.ShapeDtypeStruct((DEC_BATCH, WINDOW, kv_w), F32),
        ],
        compiler_params=_params(("parallel",)),
        name="swa_sample",
    )(sink_rows, bias1, bias2, q_t, kv_new, cache_k, cache_v)


def _merge_kernel(or_ref, os_ref, gr_ref, gs_ref, wr_ref, ws_ref, o_ref):
    a = jnp.dot(or_ref[...], wr_ref[...], preferred_element_type=F32)
    b = jnp.dot(os_ref[...], ws_ref[...], preferred_element_type=F32)
    merged = jax.nn.sigmoid(gr_ref[...]) * a + jax.nn.sigmoid(gs_ref[...]) * b
    o_ref[...] = merged.astype(BF16)


def merge_branches(o_r, o_s, u, w_ret_o_bf, w_swa_o_bf, tm, tn):
    m = o_r.shape[0]
    nj = D_MODEL // tn
    return pl.pallas_call(
        _merge_kernel,
        grid=(m // tm, nj),
        in_specs=[
            pl.BlockSpec((tm, 2048), lambda i, j: (i, 0)),
            pl.BlockSpec((tm, 2048), lambda i, j: (i, 0)),
            pl.BlockSpec((tm, tn), lambda i, j: (i, COL_GATE_R // tn + j)),
            pl.BlockSpec((tm, tn), lambda i, j: (i, COL_GATE_S // tn + j)),
            pl.BlockSpec((2048, tn), lambda i, j: (0, j)),
            pl.BlockSpec((2048, tn), lambda i, j: (0, j)),
        ],
        out_specs=pl.BlockSpec((tm, tn), lambda i, j: (i, j)),
        out_shape=jax.ShapeDtypeStruct((m, D_MODEL), BF16),
        compiler_params=_params(("parallel", "parallel")),
        name="merge_branches",
    )(o_r, o_s, u, u, w_ret_o_bf, w_swa_o_bf)


ROUTER_PAD = 128


def _post_mix_kernel(alpha, m_ref, x_ref, w_ref, g_ref, b_ref, wr_ref, br_ref, *refs):
    h_ref, lg_ref = refs[-2:]
    mix = jnp.dot(m_ref[...], w_ref[...], preferred_element_type=F32)
    h = _layer_norm(alpha * x_ref[...] + mix, g_ref[...], b_ref[...])
    h_ref[...] = h
    h_hi = h.astype(BF16)
    h_lo = (h - h_hi.astype(F32)).astype(BF16)
    hh = jnp.dot(h_hi, wr_ref[...], preferred_element_type=F32)
    lh = jnp.dot(h_lo, wr_ref[:, :ROUTER_PAD], preferred_element_type=F32)
    lg_ref[...] = hh[:, :ROUTER_PAD] + hh[:, ROUTER_PAD:] + lh + br_ref[...]


def _router_split(w_router, b_router):
    w = jnp.pad(w_router, ((0, 0), (0, ROUTER_PAD - N_EXPERTS)))
    w_hi = w.astype(BF16)
    w_lo = (w - w_hi.astype(F32)).astype(BF16)
    b = jnp.pad(b_router, (0, ROUTER_PAD - N_EXPERTS)).reshape(1, ROUTER_PAD)
    return jnp.concatenate([w_hi, w_lo], axis=1), b


def post_mix(merged, x2d, w_out_bf, ln_g, ln_b, w_router_pad, b_router_pad, alpha, row_off, prev, tm):
    m = merged.shape[0]
    blk_off = row_off // tm
    row = lambda i: (i, 0)
    whole = lambda i: (0, 0)
    out_row = lambda i: (blk_off + i, 0)
    in_specs = [
        pl.BlockSpec((tm, 2048), row),
        pl.BlockSpec((tm, 2048), row),
        pl.BlockSpec((2048, 2048), whole),
        pl.BlockSpec((1, 2048), whole),
        pl.BlockSpec((1, 2048), whole),
        pl.BlockSpec((2048, 2 * ROUTER_PAD), whole),
        pl.BlockSpec((1, ROUTER_PAD), whole),
    ]
    args = [merged, x2d, w_out_bf, ln_g, ln_b, w_router_pad, b_router_pad]
    aliases = {}
    if prev is not None:
        in_specs += [pl.BlockSpec(memory_space=pl.ANY)] * 2
        aliases = {len(args): 0, len(args) + 1: 1}
        args += list(prev)
    return pl.pallas_call(
        functools.partial(_post_mix_kernel, alpha),
        grid=(m // tm,),
        in_specs=in_specs,
        out_specs=[pl.BlockSpec((tm, 2048), out_row), pl.BlockSpec((tm, ROUTER_PAD), out_row)],
        out_shape=[jax.ShapeDtypeStruct((T_ALL, D_MODEL), F32),
                   jax.ShapeDtypeStruct((T_ALL, ROUTER_PAD), F32)],
        input_output_aliases=aliases,
        compiler_params=_params(("parallel",)),
        name="post_mix",
    )(*args)


def _moe_up_kernel(ce_ref, cs_ref, cn_ref, cb_ref, nu_ref, x_ref, wg_ref, wu_ref, bg_ref, bu_ref,
                   act_ref, x_bf, wg_bf, wu_bf):
    c = pl.program_id(0)
    nblk = cn_ref[c]

    @pl.when((nblk > 0) & (pl.program_id(1) == 0))
    def _():
        def cast(r, carry):
            rows = pl.ds(pl.multiple_of(r * MOE_SUB, MOE_SUB), MOE_SUB)
            x_bf[rows, :] = x_ref[rows, :].astype(BF16)
            return carry

        lax.fori_loop(0, nblk, cast, 0)

    @pl.when(nblk > 0)
    def _():
        wg_bf[...] = wg_ref[0].astype(BF16)
        wu_bf[...] = wu_ref[0].astype(BF16)

        def body(r, carry):
            rows = pl.ds(pl.multiple_of(r * MOE_SUB, MOE_SUB), MOE_SUB)
            xr = x_bf[rows, :]
            g = jnp.dot(xr, wg_bf[...], preferred_element_type=F32) + bg_ref[0]
            up = jnp.dot(xr, wu_bf[...], preferred_element_type=F32) + bu_ref[0]
            g = jnp.minimum(g, SWIGLU_LIMIT)
            up = jnp.clip(up, -SWIGLU_LIMIT, SWIGLU_LIMIT)
            glu = g * jax.nn.sigmoid(SWIGLU_ALPHA * g)
            act_ref[rows, :] = ((up + 1.0) * glu).astype(BF16)
            return carry

        lax.fori_loop(0, nblk, body, 0)

        def zero(r, carry):
            rows = pl.ds(pl.multiple_of(r * MOE_SUB, MOE_SUB), MOE_SUB)
            act_ref[rows, :] = jnp.zeros((MOE_SUB, MOE_TN), BF16)
            return carry

        lax.fori_loop(nblk, MOE_CHUNK_SUBS, zero, 0)


def _moe_down_kernel(ce_ref, cs_ref, cn_ref, cb_ref, nu_ref, a_ref, wd_ref, bd_ref, y_ref, wd_bf):
    c = pl.program_id(0)
    nblk = cn_ref[c]

    @pl.when(nblk > 0)
    def _():
        wd_bf[...] = wd_ref[0].astype(BF16)

        def body(r, carry):
            rows = pl.ds(pl.multiple_of(r * MOE_SUB, MOE_SUB), MOE_SUB)
            y_ref[rows, :] = jnp.dot(a_ref[rows, :], wd_bf[...], preferred_element_type=F32) + bd_ref[0]
            return carry

        lax.fori_loop(0, nblk, body, 0)

        def zero(r, carry):
            rows = pl.ds(pl.multiple_of(r * MOE_SUB, MOE_SUB), MOE_SUB)
            y_ref[rows, :] = jnp.zeros((MOE_SUB, MOE_TN), F32)
            return carry

        lax.fori_loop(nblk, MOE_CHUNK_SUBS, zero, 0)


def moe_experts(x_sorted, chunk_tables, w_gate_up, b_gate_up, w_down, b_down):
    nj = MOE_NJ

    def used_j(c, j, nu):
        return jnp.where(c < nu[0], j, nj - 1)

    act = pl.pallas_call(
        _moe_up_kernel,
        grid_spec=pltpu.PrefetchScalarGridSpec(
            num_scalar_prefetch=5,
            grid=(MOE_NC, nj),
            in_specs=[
                pl.BlockSpec((pl.Element(MOE_R), pl.Element(D_MODEL)),
                             lambda c, j, ce, cs, cn, cb, nu: (cs[c] * MOE_SUB, 0)),
                pl.BlockSpec((1, D_MODEL, MOE_TN),
                             lambda c, j, ce, cs, cn, cb, nu: (ce[c], 0, used_j(c, j, nu))),
                pl.BlockSpec((1, D_MODEL, MOE_TN),
                             lambda c, j, ce, cs, cn, cb, nu: (ce[c], 0, nj + used_j(c, j, nu))),
                pl.BlockSpec((1, 1, MOE_TN),
                             lambda c, j, ce, cs, cn, cb, nu: (ce[c], 0, used_j(c, j, nu))),
                pl.BlockSpec((1, 1, MOE_TN),
                             lambda c, j, ce, cs, cn, cb, nu: (ce[c], 0, nj + used_j(c, j, nu))),
            ],
            out_specs=pl.BlockSpec((MOE_R, MOE_TN),
                                   lambda c, j, ce, cs, cn, cb, nu: (cb[c], used_j(c, j, nu))),
            scratch_shapes=[pltpu.VMEM((MOE_R, D_MODEL), BF16),
                            pltpu.VMEM((D_MODEL, MOE_TN), BF16), pltpu.VMEM((D_MODEL, MOE_TN), BF16)],
        ),
        out_shape=jax.ShapeDtypeStruct((MOE_NC * MOE_R, D_FF), BF16),
        compiler_params=_params(("arbitrary", "arbitrary")),
        name="moe_up",
    )(*chunk_tables, x_sorted, w_gate_up, w_gate_up, b_gate_up, b_gate_up)

    y = pl.pallas_call(
        _moe_down_kernel,
        grid_spec=pltpu.PrefetchScalarGridSpec(
            num_scalar_prefetch=5,
            grid=(MOE_NC, nj),
            in_specs=[
                pl.BlockSpec((MOE_R, D_FF), lambda c, j, ce, cs, cn, cb, nu: (cb[c], 0)),
                pl.BlockSpec((1, D_FF, MOE_TN),
                             lambda c, j, ce, cs, cn, cb, nu: (ce[c], 0, used_j(c, j, nu))),
                pl.BlockSpec((1, 1, MOE_TN),
                             lambda c, j, ce, cs, cn, cb, nu: (ce[c], 0, used_j(c, j, nu))),
            ],
            out_specs=pl.BlockSpec((MOE_R, MOE_TN),
                                   lambda c, j, ce, cs, cn, cb, nu: (cb[c], used_j(c, j, nu))),
            scratch_shapes=[pltpu.VMEM((D_FF, MOE_TN), BF16)],
        ),
        out_shape=jax.ShapeDtypeStruct((MOE_NC * MOE_R, D_MODEL), F32),
        compiler_params=_params(("arbitrary", "arbitrary")),
        name="moe_down",
    )(*chunk_tables, act, w_down, b_down)
    return y


def route(logits):
    top_val, top_idx = lax.top_k(logits, TOP_K)
    gate = jax.nn.softmax(top_val, axis=-1)
    e_flat = top_idx.reshape(TK_ALL)
    onehot = (e_flat[:, None] == jnp.arange(N_EXPERTS, dtype=e_flat.dtype)[None, :]).astype(jnp.int32)
    csum = jnp.cumsum(onehot, axis=0)
    rank = jnp.sum(csum * onehot, axis=1) - 1
    counts = csum[-1]
    nsub = (counts + MOE_SUB - 1) // MOE_SUB
    sub_base = jnp.cumsum(nsub) - nsub
    nch = (counts + MOE_R - 1) // MOE_R
    ch_end = jnp.cumsum(nch)
    ch_base = ch_end - nch
    n_used = ch_end[-1]
    xpos = MOE_SUB * sub_base[e_flat] + rank
    ypos = MOE_R * (ch_base[e_flat] + rank // MOE_R) + rank % MOE_R
    tok = jnp.arange(TK_ALL, dtype=jnp.int32) // TOP_K
    row_tok = jnp.zeros((MOE_XROWS,), jnp.int32).at[xpos].set(tok)
    c = jnp.arange(MOE_NC, dtype=jnp.int32)
    cc = jnp.minimum(c, n_used - 1)
    ce = jnp.minimum(jnp.sum(cc[:, None] >= ch_end[None, :], axis=1), N_EXPERTS - 1).astype(jnp.int32)
    kk = cc - ch_base[ce]
    cs = (sub_base[ce] + MOE_CHUNK_SUBS * kk).astype(jnp.int32)
    cn = jnp.where(c < n_used, jnp.minimum(MOE_CHUNK_SUBS, nsub[ce] - MOE_CHUNK_SUBS * kk), 0).astype(jnp.int32)
    tables = (ce, cs, cn, cc.astype(jnp.int32), n_used.reshape(1).astype(jnp.int32))
    return gate, row_tok, ypos.reshape(T_ALL, TOP_K), tables


def _combine_kernel(alpha, h_ref, yg_ref, gate_ref, g_ref, b_ref, o_ref):
    gate = gate_ref[...]
    f = yg_ref[0] * gate[:, 0:1]
    for k in range(1, TOP_K):
        f = f + yg_ref[k] * gate[:, k:k + 1]
    o_ref[...] = _layer_norm(alpha * h_ref[...] + f, g_ref[...], b_ref[...])


def combine(h, yg, gate, ln_g, ln_b, alpha, row_off, m, tm):
    blk_off = row_off // tm
    whole = lambda i: (0, 0)
    return pl.pallas_call(
        functools.partial(_combine_kernel, alpha),
        grid=(m // tm,),
        in_specs=[
            pl.BlockSpec((tm, D_MODEL), lambda i: (blk_off + i, 0)),
            pl.BlockSpec((TOP_K, tm, D_MODEL), lambda i: (0, blk_off + i, 0)),
            pl.BlockSpec((tm, TOP_K), lambda i: (blk_off + i, 0)),
            pl.BlockSpec((1, D_MODEL), whole),
            pl.BlockSpec((1, D_MODEL), whole),
        ],
        out_specs=pl.BlockSpec((tm, D_MODEL), lambda i: (i, 0)),
        out_shape=jax.ShapeDtypeStruct((m, D_MODEL), F32),
        compiler_params=_params(("parallel",)),
        name="combine",
    )(h, yg, gate, ln_g, ln_b)


def _retention_tables(chunk, group):
    log_gamma = jnp.log(1.0 - jnp.exp2(-5.0 - jnp.arange(N_RET_HEADS, dtype=F32)))
    n = chunk * group
    pos = jnp.arange(n) % chunk
    seq = jnp.arange(n) // chunk
    diff = (pos[:, None] - pos[None, :]).astype(F32)
    same = seq[:, None] == seq[None, :]
    lg = log_gamma[:, None, None]
    dec = jnp.where(same[None] & (diff >= 0)[None], jnp.exp(lg * jnp.maximum(diff, 0.0)[None]), 0.0)
    posf = pos.astype(F32)
    qd = jnp.exp(log_gamma[:, None] * (posf + 1.0))[..., None]
    kd = jnp.exp(log_gamma[:, None] * (chunk - 1.0 - posf))[..., None]
    cd = jnp.exp(log_gamma * chunk)
    return cd, dec, qd, kd


def _alibi_slopes():
    h = jnp.arange(1, N_SWA_HEADS + 1, dtype=F32)
    return jnp.exp2(-8.0 * h / N_SWA_HEADS)


def kernel(x_prompt, x_sample, state_ret, cache_swa_k, cache_swa_v, w_in, w_ret_o, w_swa_o, w_out, swa_sinks,
           ln1_g, ln1_b, w_router, b_router, w_gate_up, b_gate_up, w_down, b_down, ln2_g, ln2_b):
    alpha = (2.0 * DEPTH) ** 0.25
    kv_w = N_SWA_KV_HEADS * SWA_HEAD_DIM
    w = w_in[0]
    w_in_bf = jnp.concatenate([w[:, :8192], w[:, 8704:], w[:, 8192:8704]], axis=1).astype(BF16)
    w_ret_o_bf = w_ret_o[0].astype(BF16)
    w_swa_o_bf = w_swa_o[0].astype(BF16)
    w_out_bf = w_out[0].astype(BF16)
    w_router_pad, b_router_pad = _router_split(w_router[0], b_router[0])
    ln1g, ln1b = ln1_g[0].reshape(1, D_MODEL), ln1_b[0].reshape(1, D_MODEL)
    ln2g, ln2b = ln2_g[0].reshape(1, D_MODEL), ln2_b[0].reshape(1, D_MODEL)
    slopes = _alibi_slopes()
    sinks = swa_sinks[0].astype(F32)

    xp = x_prompt.reshape(T_PROMPT, D_MODEL)
    xs = x_sample.reshape(T_SAMPLE, D_MODEL)

    u_p = in_proj(xp, w_in_bf, 1024, 1280)
    o_r_p, state_p = retention_prompt(u_p, _retention_tables(RET_CHUNK, 1))
    o_s_p = swa_prompt(u_p, slopes, sinks)
    kv_p = u_p.reshape(BATCH, SEQ, D_IN)[:, SEQ - WINDOW:, COL_KVS:]
    k_cache_p = kv_p[..., :kv_w].reshape(1, BATCH, WINDOW, N_SWA_KV_HEADS, SWA_HEAD_DIM)
    v_cache_p = kv_p[..., kv_w:].reshape(1, BATCH, WINDOW, N_SWA_KV_HEADS, SWA_HEAD_DIM)
    merged_p = merge_branches(o_r_p, o_s_p, u_p, w_ret_o_bf, w_swa_o_bf, 512, 1024)

    u_s = in_proj(xs, w_in_bf, 512, 1280)
    o_r_s, state_s = retention_sample(u_s, state_ret[0], _retention_tables(DEC_SEQ, RS_BB))
    q_t = (u_s[:, COL_QS:COL_QS + 2048]
           .reshape(DEC_BATCH, DEC_SEQ, N_SWA_KV_HEADS, SWA_GROUP, SWA_HEAD_DIM)
           .transpose(0, 2, 1, 3, 4).reshape(DEC_BATCH, N_SWA_KV_HEADS, SS_ROWS, SWA_HEAD_DIM))
    kv_new = u_s[:, COL_KVS:].reshape(DEC_BATCH, DEC_SEQ, 2 * kv_w)
    o_s_t, k_cache_s, v_cache_s = swa_sample(
        q_t, kv_new, cache_swa_k[0].reshape(DEC_BATCH, WINDOW, kv_w),
        cache_swa_v[0].reshape(DEC_BATCH, WINDOW, kv_w), _swa_sample_tables(slopes, sinks))
    o_s_s = (o_s_t.reshape(DEC_BATCH, N_SWA_KV_HEADS, DEC_SEQ, SWA_GROUP, SWA_HEAD_DIM)
             .transpose(0, 2, 1, 3, 4).reshape(T_SAMPLE, 2048).astype(BF16))
    merged_s = merge_branches(o_r_s, o_s_s, u_s, w_ret_o_bf, w_swa_o_bf, 512, 1024)

    outs = post_mix(merged_p, xp, w_out_bf, ln1g, ln1b, w_router_pad, b_router_pad, alpha, 0, None, 512)
    h, logits = post_mix(merged_s, xs, w_out_bf, ln1g, ln1b, w_router_pad, b_router_pad, alpha,
                         T_PROMPT, outs, 512)

    gate, row_tok, ypos, tables = route(logits[:, :N_EXPERTS])
    x_sorted = h[row_tok]
    y_rows = moe_experts(x_sorted, tables, w_gate_up[0], b_gate_up[0].reshape(N_EXPERTS, 1, 2 * D_FF),
                         w_down[0], b_down[0].reshape(N_EXPERTS, 1, D_MODEL))
    yg = y_rows[ypos.T]
    y_p = combine(h, yg, gate, ln2g, ln2b, alpha, 0, T_PROMPT, 256)
    y_s = combine(h, yg, gate, ln2g, ln2b, alpha, T_PROMPT, T_SAMPLE, 256)

    return (y_p.reshape(BATCH, SEQ, D_MODEL), y_s.reshape(DEC_BATCH, DEC_SEQ, D_MODEL),
            state_p[None], k_cache_p, v_cache_p,
            state_s[None], k_cache_s.reshape(1, DEC_BATCH, WINDOW, N_SWA_KV_HEADS, SWA_HEAD_DIM),
            v_cache_s.reshape(1, DEC_BATCH, WINDOW, N_SWA_KV_HEADS, SWA_HEAD_DIM))
```

```python
import functools

import jax
import jax.numpy as jnp
from jax import lax
from jax.experimental import pallas as pl
from jax.experimental.pallas import tpu as pltpu

F32 = jnp.float32
BF16 = jnp.bfloat16

D_MODEL = 2048
BATCH = 4
SEQ = 2048
DEC_BATCH = 128
DEC_SEQ = 4
N_RET_HEADS = 8
RET_DK = 128
RET_DV = 256
RET_CHUNK = 128
N_SWA_HEADS = 32
N_SWA_KV_HEADS = 4
SWA_GROUP = 8
SWA_HEAD_DIM = 64
WINDOW = 128
N_EXPERTS = 32
TOP_K = 4
D_FF = 2048
SWIGLU_LIMIT = 7.0
SWIGLU_ALPHA = 1.702
LN_EPS = 1e-5
GN_EPS = 1e-6
NEG_INF = -1e30
DEPTH = 1

T_PROMPT = BATCH * SEQ
T_SAMPLE = DEC_BATCH * DEC_SEQ
T_ALL = T_PROMPT + T_SAMPLE
TK_ALL = T_ALL * TOP_K
D_IN = 12800

COL_QR, COL_KR, COL_VR, COL_GR, COL_QS, COL_GATE_R, COL_GATE_S, COL_KVS = (
    0, 1024, 2048, 4096, 6144, 8192, 10240, 12288)

MOE_SUB = 256
MOE_CHUNK_SUBS = 5
MOE_R = MOE_SUB * MOE_CHUNK_SUBS
MOE_NSUB_MAX = TK_ALL // MOE_SUB + N_EXPERTS
MOE_XROWS = (MOE_NSUB_MAX + MOE_CHUNK_SUBS - 1) * MOE_SUB
MOE_NC = TK_ALL // MOE_R + N_EXPERTS + 2
MOE_TF = 256
MOE_NJ = D_FF // MOE_TF
MOE_VMEM_LIMIT = 60 * 1024 * 1024

VMEM_LIMIT = 56 * 1024 * 1024


def _params(sem):
    return pltpu.CompilerParams(dimension_semantics=sem, vmem_limit_bytes=VMEM_LIMIT)


def _in_proj_kernel(x_ref, w_ref, o_ref):
    o_ref[...] = jnp.dot(x_ref[...].astype(BF16), w_ref[...], preferred_element_type=F32)


def in_proj(x2d, w_bf, tm, tn):
    m, k = x2d.shape
    n = w_bf.shape[1]
    return pl.pallas_call(
        _in_proj_kernel,
        grid=(m // tm, n // tn),
        in_specs=[pl.BlockSpec((tm, k), lambda i, j: (i, 0)),
                  pl.BlockSpec((k, tn), lambda i, j: (0, j))],
        out_specs=pl.BlockSpec((tm, tn), lambda i, j: (i, j)),
        out_shape=jax.ShapeDtypeStruct((m, n), F32),
        compiler_params=_params(("parallel", "parallel")),
        name="in_proj",
    )(x2d, w_bf)


def _group_norm_gate(o, g):
    mu = jnp.mean(o, axis=-1, keepdims=True)
    oc = o - mu
    var = jnp.mean(oc * oc, axis=-1, keepdims=True)
    return oc * lax.rsqrt(var + GN_EPS) * (g * jax.nn.sigmoid(g))


def _dot_nt(a, b):
    return lax.dot_general(a, b, (((1,), (1,)), ((), ())), preferred_element_type=F32)


def _layer_norm(x, g, b):
    mu = jnp.mean(x, axis=-1, keepdims=True)
    xc = x - mu
    var = jnp.mean(xc * xc, axis=-1, keepdims=True)
    return xc * lax.rsqrt(var + LN_EPS) * g + b


def _ret_prompt_kernel(cd_ref, dec_ref, qd_ref, kd_ref, q_ref, k_ref, v_ref, g_ref, o_ref, st_ref):
    c = pl.program_id(1)

    @pl.when(c == 0)
    def _():
        st_ref[...] = jnp.zeros_like(st_ref)

    for h in range(N_RET_HEADS):
        q = q_ref[:, h * RET_DK:(h + 1) * RET_DK]
        k = k_ref[:, h * RET_DK:(h + 1) * RET_DK] * (RET_DK ** -0.5)
        v = v_ref[:, h * RET_DV:(h + 1) * RET_DV].astype(BF16)
        state = st_ref[0, h]
        scores = _dot_nt(q.astype(BF16), k.astype(BF16)) * dec_ref[h]
        o = (jnp.dot(scores.astype(BF16), v, preferred_element_type=F32)
             + jnp.dot((q * qd_ref[h]).astype(BF16), state.astype(BF16), preferred_element_type=F32))
        kt = (k * kd_ref[h]).T.astype(BF16)
        st_ref[0, h] = cd_ref[h] * state + jnp.dot(kt, v, preferred_element_type=F32)
        g = g_ref[:, h * RET_DV:(h + 1) * RET_DV]
        o_ref[:, h * RET_DV:(h + 1) * RET_DV] = _group_norm_gate(o, g).astype(BF16)


def retention_prompt(u, tables):
    cd, dec, qd, kd = tables
    nc = SEQ // RET_CHUNK
    row = lambda b, c: b * nc + c
    whole3 = lambda b, c: (0, 0, 0)
    return pl.pallas_call(
        _ret_prompt_kernel,
        grid=(BATCH, nc),
        in_specs=[
            pl.BlockSpec(memory_space=pltpu.SMEM),
            pl.BlockSpec((N_RET_HEADS, RET_CHUNK, RET_CHUNK), whole3),
            pl.BlockSpec((N_RET_HEADS, RET_CHUNK, 1), whole3),
            pl.BlockSpec((N_RET_HEADS, RET_CHUNK, 1), whole3),
            pl.BlockSpec((RET_CHUNK, 1024), lambda b, c: (row(b, c), COL_QR // 1024)),
            pl.BlockSpec((RET_CHUNK, 1024), lambda b, c: (row(b, c), COL_KR // 1024)),
            pl.BlockSpec((RET_CHUNK, 2048), lambda b, c: (row(b, c), COL_VR // 2048)),
            pl.BlockSpec((RET_CHUNK, 2048), lambda b, c: (row(b, c), COL_GR // 2048)),
        ],
        out_specs=[
            pl.BlockSpec((RET_CHUNK, 2048), lambda b, c: (row(b, c), 0)),
            pl.BlockSpec((1, N_RET_HEADS, RET_DK, RET_DV), lambda b, c: (b, 0, 0, 0)),
        ],
        out_shape=[
            jax.ShapeDtypeStruct((T_PROMPT, 2048), BF16),
            jax.ShapeDtypeStruct((BATCH, N_RET_HEADS, RET_DK, RET_DV), F32),
        ],
        compiler_params=_params(("parallel", "arbitrary")),
        name="retention_prompt",
    )(cd, dec, qd, kd, u, u, u, u)


RS_BB = 32


def _ret_sample_kernel(cd_ref, dec_ref, qd_ref, kd_ref, q_ref, k_ref, v_ref, g_ref, st_ref,
                       o_ref, sto_ref):
    h = pl.program_id(1)
    q = q_ref[...]
    k = k_ref[...] * (RET_DK ** -0.5)
    v = v_ref[...].astype(BF16)
    scores = _dot_nt(q.astype(BF16), k.astype(BF16)) * dec_ref[0]
    o_intra = jnp.dot(scores.astype(BF16), v, preferred_element_type=F32)
    qq = (q * qd_ref[0]).astype(BF16)
    kt = (k * kd_ref[0]).T
    col_batch = lax.broadcasted_iota(jnp.int32, kt.shape, 1) // DEC_SEQ
    first_of_pair = lax.broadcasted_iota(jnp.int32, (8, RET_DV), 0) < DEC_SEQ
    cd = cd_ref[h]
    pieces = []
    for p in range(RS_BB // 2):
        q8 = qq[8 * p:8 * p + 8]
        s0 = st_ref[2 * p].astype(BF16)
        s1 = st_ref[2 * p + 1].astype(BF16)
        r0 = jnp.dot(q8, s0, preferred_element_type=F32)
        r1 = jnp.dot(q8, s1, preferred_element_type=F32)
        pieces.append(jnp.where(first_of_pair, r0, r1))
    for b in range(RS_BB):
        ktm = jnp.where(col_batch == b, kt, 0.0).astype(BF16)
        sto_ref[b] = cd * st_ref[b] + jnp.dot(ktm, v, preferred_element_type=F32)
    o = o_intra + jnp.concatenate(pieces, axis=0)
    o_ref[...] = _group_norm_gate(o, g_ref[...]).astype(BF16)


def retention_sample(u_s, state, tables):
    cd, dec, qd, kd = tables
    rows = RS_BB * DEC_SEQ
    per_head = lambda i, h: (h, 0, 0)
    st_spec = pl.BlockSpec((RS_BB, None, RET_DK, RET_DV), lambda i, h: (i, h, 0, 0))
    return pl.pallas_call(
        _ret_sample_kernel,
        grid=(DEC_BATCH // RS_BB, N_RET_HEADS),
        in_specs=[
            pl.BlockSpec(memory_space=pltpu.SMEM),
            pl.BlockSpec((1, rows, rows), per_head),
            pl.BlockSpec((1, rows, 1), per_head),
            pl.BlockSpec((1, rows, 1), per_head),
            pl.BlockSpec((rows, RET_DK), lambda i, h: (i, COL_QR // RET_DK + h)),
            pl.BlockSpec((rows, RET_DK), lambda i, h: (i, COL_KR // RET_DK + h)),
            pl.BlockSpec((rows, RET_DV), lambda i, h: (i, COL_VR // RET_DV + h)),
            pl.BlockSpec((rows, RET_DV), lambda i, h: (i, COL_GR // RET_DV + h)),
            st_spec,
        ],
        out_specs=[pl.BlockSpec((rows, RET_DV), lambda i, h: (i, h)), st_spec],
        out_shape=[
            jax.ShapeDtypeStruct((T_SAMPLE, 2048), BF16),
            jax.ShapeDtypeStruct((DEC_BATCH, N_RET_HEADS, RET_DK, RET_DV), F32),
        ],
        compiler_params=_params(("parallel", "parallel")),
        name="retention_sample",
    )(cd, dec, qd, kd, u_s, u_s, u_s, u_s, state)


def _swa_prompt_kernel(sink_ref, bias_ref, q_ref, kvc_ref, kvp_ref, o_ref):
    kv_w = N_SWA_KV_HEADS * SWA_HEAD_DIM
    hd = SWA_HEAD_DIM
    qt = q_ref[...].T
    outs = []
    for kvh in range(N_SWA_KV_HEADS):
        lo = kvh * hd
        k_cat = jnp.concatenate([kvp_ref[:, lo:lo + hd], kvc_ref[:, lo:lo + hd]], axis=0).astype(BF16)
        v_cat = jnp.concatenate([kvp_ref[:, kv_w + lo:kv_w + lo + hd],
                                 kvc_ref[:, kv_w + lo:kv_w + lo + hd]], axis=0)
        vt = v_cat.T.astype(BF16)
        heads = range(kvh * SWA_GROUP, (kvh + 1) * SWA_GROUP)
        q8 = jnp.concatenate([qt[h * hd:(h + 1) * hd, :] for h in heads], axis=1).astype(BF16)
        s = jnp.dot(k_cat, q8, preferred_element_type=F32) * (hd ** -0.5) + bias_ref[0, kvh]
        sink = sink_ref[kvh]
        m = jnp.maximum(jnp.max(s, axis=0, keepdims=True), sink)
        e = jnp.exp(s - m)
        den = jnp.sum(e, axis=0, keepdims=True) + jnp.exp(sink - m)
        p = (e * (1.0 / den)).astype(BF16)
        ot = jnp.dot(vt, p, preferred_element_type=F32)
        outs += [ot[:, g * WINDOW:(g + 1) * WINDOW] for g in range(SWA_GROUP)]
    o_ref[...] = jnp.concatenate(outs, axis=0).T.astype(BF16)


def _swa_prompt_tables(slopes, sinks):
    qi = jnp.arange(WINDOW)[:, None] + WINDOW
    ki = jnp.arange(2 * WINDOW)[None, :]
    dist = qi - ki
    in_window = (dist >= 0) & (dist < WINDOW)
    valid = jnp.stack([in_window & (ki >= WINDOW), in_window])
    pen = -(slopes[:, None, None] * dist.astype(F32)[None])
    bias = jnp.where(valid[:, None], pen[None], NEG_INF)
    bias = bias.reshape(2, N_SWA_KV_HEADS, SWA_GROUP, WINDOW, 2 * WINDOW).transpose(0, 1, 4, 2, 3)
    bias = bias.reshape(2, N_SWA_KV_HEADS, 2 * WINDOW, SWA_GROUP * WINDOW)
    sink_l = jnp.repeat(sinks.reshape(N_SWA_KV_HEADS, SWA_GROUP), WINDOW, axis=1)
    return sink_l.reshape(N_SWA_KV_HEADS, 1, SWA_GROUP * WINDOW), bias


def swa_prompt(u, slopes, sinks):
    nb = SEQ // WINDOW
    kv_blk = COL_KVS // 512
    sink_l, bias = _swa_prompt_tables(slopes, sinks)
    lanes = SWA_GROUP * WINDOW
    return pl.pallas_call(
        _swa_prompt_kernel,
        grid=(BATCH, nb),
        in_specs=[
            pl.BlockSpec((N_SWA_KV_HEADS, 1, lanes), lambda b, j: (0, 0, 0)),
            pl.BlockSpec((1, N_SWA_KV_HEADS, 2 * WINDOW, lanes), lambda b, j: (jnp.minimum(j, 1), 0, 0, 0)),
            pl.BlockSpec((WINDOW, 2048), lambda b, j: (b * nb + j, COL_QS // 2048)),
            pl.BlockSpec((WINDOW, 512), lambda b, j: (b * nb + j, kv_blk)),
            pl.BlockSpec((WINDOW, 512), lambda b, j: (b * nb + jnp.maximum(j - 1, 0), kv_blk)),
        ],
        out_specs=pl.BlockSpec((WINDOW, 2048), lambda b, j: (b * nb + j, 0)),
        out_shape=jax.ShapeDtypeStruct((T_PROMPT, 2048), BF16),
        compiler_params=_params(("parallel", "parallel")),
        name="swa_prompt",
    )(sink_l, bias, u, u, u)


SS_BB = 8
SS_ROWS = DEC_SEQ * SWA_GROUP


def _bf16_round(x):
    return x.astype(BF16).astype(F32)


def _swa_sample_kernel(sink_ref, bias1_ref, bias2_ref, q_ref, kvn_ref, ck_ref, cv_ref,
                       o_ref, cko_ref, cvo_ref):
    kv_w = N_SWA_KV_HEADS * SWA_HEAD_DIM
    hd = SWA_HEAD_DIM
    scale = hd ** -0.5
    for kvh in range(N_SWA_KV_HEADS):
        lo = kvh * hd
        sink = sink_ref[kvh]
        q = q_ref[:, kvh]
        ck = ck_ref[:, :, lo:lo + hd].astype(BF16)
        cv = cv_ref[:, :, lo:lo + hd].astype(BF16)
        kn = _bf16_round(kvn_ref[:, :, lo:lo + hd])
        vn = _bf16_round(kvn_ref[:, :, kv_w + lo:kv_w + lo + hd])
        s1 = lax.dot_general(q.astype(BF16), ck, (((2,), (2,)), ((0,), (0,))),
                             preferred_element_type=F32) * scale + bias1_ref[kvh]
        qr = _bf16_round(q)
        s2 = [jnp.sum(qr * kn[:, t2:t2 + 1, :], axis=-1, keepdims=True) * scale + bias2_ref[kvh, t2]
              for t2 in range(DEC_SEQ)]
        m = jnp.maximum(jnp.max(s1, axis=-1, keepdims=True), sink)
        for st in s2:
            m = jnp.maximum(m, st)
        e1 = jnp.exp(s1 - m)
        e2 = [jnp.exp(st - m) for st in s2]
        den = jnp.sum(e1, axis=-1, keepdims=True) + jnp.exp(sink - m)
        for et in e2:
            den = den + et
        inv = 1.0 / den
        o = lax.dot_general((e1 * inv).astype(BF16), cv, (((2,), (1,)), ((0,), (0,))),
                            preferred_element_type=F32)
        for t2 in range(DEC_SEQ):
            o = o + _bf16_round(e2[t2] * inv) * vn[:, t2:t2 + 1, :]
        o_ref[:, kvh] = o
    keep = WINDOW - DEC_SEQ
    cko_ref[:, 0:keep, :] = ck_ref[:, DEC_SEQ:WINDOW, :]
    cko_ref[:, keep:WINDOW, :] = kvn_ref[:, :, 0:kv_w]
    cvo_ref[:, 0:keep, :] = cv_ref[:, DEC_SEQ:WINDOW, :]
    cvo_ref[:, keep:WINDOW, :] = kvn_ref[:, :, kv_w:2 * kv_w]


def _swa_sample_tables(slopes, sinks):
    g_slopes = jnp.tile(slopes.reshape(N_SWA_KV_HEADS, 1, SWA_GROUP), (1, DEC_SEQ, 1)).reshape(
        N_SWA_KV_HEADS, SS_ROWS)
    sink_rows = jnp.tile(sinks.reshape(N_SWA_KV_HEADS, 1, SWA_GROUP), (1, DEC_SEQ, 1)).reshape(
        N_SWA_KV_HEADS, SS_ROWS, 1)
    t_row = jnp.arange(SS_ROWS) // SWA_GROUP
    jcol = jnp.arange(WINDOW)
    dist1 = (t_row[:, None] + WINDOW - jcol[None, :])
    bias1 = jnp.where((dist1 < WINDOW)[None], -(g_slopes[:, :, None] * dist1.astype(F32)[None]), NEG_INF)
    t2 = jnp.arange(DEC_SEQ)
    dist2 = t_row[None, :] - t2[:, None]
    bias2 = jnp.where((dist2 >= 0)[None], -(g_slopes[:, None, :] * dist2.astype(F32)[None]), NEG_INF)
    return sink_rows, bias1, bias2[..., None]


def swa_sample(q_t, kv_new, cache_k, cache_v, tables):
    sink_rows, bias1, bias2 = tables
    kv_w = N_SWA_KV_HEADS * SWA_HEAD_DIM
    whole3 = lambda i: (0, 0, 0)
    cache_spec = pl.BlockSpec((SS_BB, WINDOW, kv_w), lambda i: (i, 0, 0))
    return pl.pallas_call(
        _swa_sample_kernel,
        grid=(DEC_BATCH // SS_BB,),
        in_specs=[
            pl.BlockSpec((N_SWA_KV_HEADS, SS_ROWS, 1), whole3),
            pl.BlockSpec((N_SWA_KV_HEADS, SS_ROWS, WINDOW), whole3),
            pl.BlockSpec((N_SWA_KV_HEADS, DEC_SEQ, SS_ROWS, 1), lambda i: (0, 0, 0, 0)),
            pl.BlockSpec((SS_BB, N_SWA_KV_HEADS, SS_ROWS, SWA_HEAD_DIM), lambda i: (i, 0, 0, 0)),
            pl.BlockSpec((SS_BB, DEC_SEQ, 2 * kv_w), lambda i: (i, 0, 0)),
            cache_spec, cache_spec,
        ],
        out_specs=[
            pl.BlockSpec((SS_BB, N_SWA_KV_HEADS, SS_ROWS, SWA_HEAD_DIM), lambda i: (i, 0, 0, 0)),
            cache_spec, cache_spec,
        ],
        out_shape=[
            jax.ShapeDtypeStruct((DEC_BATCH, N_SWA_KV_HEADS, SS_ROWS, SWA_HEAD_DIM), F32),
            jax.ShapeDtypeStruct((DEC_BATCH, WINDOW, kv_w), F32),
            jax.ShapeDtypeStruct((DEC_BATCH, WINDOW, kv_w), F32),
        ],
        compiler_params=_params(("parallel",)),
        name="swa_sample",
    )(sink_rows, bias1, bias2, q_t, kv_new, cache_k, cache_v)


def _merge_kernel(or_ref, os_ref, gr_ref, gs_ref, wr_ref, ws_ref, o_ref):
    a = jnp.dot(or_ref[...], wr_ref[...], preferred_element_type=F32)
    b = jnp.dot(os_ref[...], ws_ref[...], preferred_element_type=F32)
    merged = jax.nn.sigmoid(gr_ref[...]) * a + jax.nn.sigmoid(gs_ref[...]) * b
    o_ref[...] = merged.astype(BF16)


def merge_branches(o_r, o_s, u, w_ret_o_bf, w_swa_o_bf, tm):
    m = o_r.shape[0]
    resident = pl.BlockSpec((2048, D_MODEL), lambda i: (0, 0), pipeline_mode=pl.Buffered(1))
    return pl.pallas_call(
        _merge_kernel,
        grid=(m // tm,),
        in_specs=[
            pl.BlockSpec((tm, 2048), lambda i: (i, 0)),
            pl.BlockSpec((tm, 2048), lambda i: (i, 0)),
            pl.BlockSpec((tm, D_MODEL), lambda i: (i, COL_GATE_R // D_MODEL)),
            pl.BlockSpec((tm, D_MODEL), lambda i: (i, COL_GATE_S // D_MODEL)),
            resident, resident,
        ],
        out_specs=pl.BlockSpec((tm, D_MODEL), lambda i: (i, 0)),
        out_shape=jax.ShapeDtypeStruct((m, D_MODEL), BF16),
        compiler_params=_params(("parallel",)),
        name="merge_branches",
    )(o_r, o_s, u, u, w_ret_o_bf, w_swa_o_bf)


ROUTER_PAD = 128


def _post_mix_kernel(alpha, n_prompt_blocks, mp_ref, ms_ref, xp_ref, xs_ref, w_ref, g_ref, b_ref, wr_ref, br_ref,
                     h_ref, lg_ref):
    from_sample = pl.program_id(0) >= n_prompt_blocks
    merged = jnp.where(from_sample, ms_ref[...], mp_ref[...])
    x = jnp.where(from_sample, xs_ref[...], xp_ref[...])
    mix = jnp.dot(merged, w_ref[...], preferred_element_type=F32)
    h = _layer_norm(alpha * x + mix, g_ref[...], b_ref[...])
    h_ref[...] = h
    h_hi = h.astype(BF16)
    h_lo = (h - h_hi.astype(F32)).astype(BF16)
    hh = jnp.dot(h_hi, wr_ref[...], preferred_element_type=F32)
    lh = jnp.dot(h_lo, wr_ref[:, :ROUTER_PAD], preferred_element_type=F32)
    lg_ref[...] = hh[:, :ROUTER_PAD] + hh[:, ROUTER_PAD:] + lh + br_ref[...]


def _router_split(w_router, b_router):
    w = jnp.pad(w_router, ((0, 0), (0, ROUTER_PAD - N_EXPERTS)))
    w_hi = w.astype(BF16)
    w_lo = (w - w_hi.astype(F32)).astype(BF16)
    b = jnp.pad(b_router, (0, ROUTER_PAD - N_EXPERTS)).reshape(1, ROUTER_PAD)
    return jnp.concatenate([w_hi, w_lo], axis=1), b


def post_mix(merged_p, merged_s, xp, xs, w_out_bf, ln_g, ln_b, w_router_pad, b_router_pad, alpha, tm):
    n_p = T_PROMPT // tm
    n_s = T_SAMPLE // tm
    prompt_row = lambda i: (jnp.minimum(i, n_p - 1), 0)
    sample_row = lambda i: (jnp.clip(i - n_p, 0, n_s - 1), 0)
    whole = lambda i: (0, 0)
    row = lambda i: (i, 0)
    return pl.pallas_call(
        functools.partial(_post_mix_kernel, alpha, n_p),
        grid=(n_p + n_s,),
        in_specs=[
            pl.BlockSpec((tm, 2048), prompt_row),
            pl.BlockSpec((tm, 2048), sample_row),
            pl.BlockSpec((tm, 2048), prompt_row),
            pl.BlockSpec((tm, 2048), sample_row),
            pl.BlockSpec((2048, 2048), whole),
            pl.BlockSpec((1, 2048), whole),
            pl.BlockSpec((1, 2048), whole),
            pl.BlockSpec((2048, 2 * ROUTER_PAD), whole),
            pl.BlockSpec((1, ROUTER_PAD), whole),
        ],
        out_specs=[pl.BlockSpec((tm, 2048), row), pl.BlockSpec((tm, ROUTER_PAD), row)],
        out_shape=[jax.ShapeDtypeStruct((T_ALL, D_MODEL), F32),
                   jax.ShapeDtypeStruct((T_ALL, ROUTER_PAD), F32)],
        compiler_params=_params(("parallel",)),
        name="post_mix",
    )(merged_p, merged_s, xp, xs, w_out_bf, ln_g, ln_b, w_router_pad, b_router_pad)


def _moe_kernel(ce_ref, cs_ref, cn_ref, nu_ref, tail_ref, x_hbm, wg_ref, wu_ref, wd_ref, bg_ref, bu_ref, bd_ref,
                y_hbm, x_f32, x_bf, y_acc, zero_buf, x_sem, y_sem, z_sem):
    c = pl.program_id(0)
    j = pl.program_id(1)
    nblk = cn_ref[c]
    slot = c % 2
    y_cur = y_acc.at[slot]

    def sub_rows(r):
        return pl.ds(pl.multiple_of(r * MOE_SUB, MOE_SUB), MOE_SUB)

    def x_copy(chunk):
        start = pl.multiple_of(cs_ref[chunk] * MOE_SUB, MOE_SUB)
        return pltpu.make_async_copy(x_hbm.at[pl.ds(start, MOE_R), :], x_f32, x_sem)

    def y_copy(r, sub):
        return pltpu.make_async_copy(y_cur.at[sub_rows(r), :], y_hbm.at[sub_rows(sub), :], y_sem.at[slot])

    def zero_copy(sub):
        return pltpu.make_async_copy(zero_buf, y_hbm.at[sub_rows(sub), :], z_sem)

    @pl.when(j == 0)
    def _():
        def drain(i, carry):
            y_copy(0, 0).wait()
            return carry

        lax.fori_loop(0, cn_ref[jnp.maximum(c - 2, 0)] * (c >= 2).astype(jnp.int32), drain, 0)

    @pl.when((nblk > 0) & (j == 0))
    def _():
        @pl.when(c == 0)
        def _():
            x_copy(0).start()

        x_copy(c).wait()

        def stage(r, carry):
            rows = sub_rows(r)
            x_bf[rows, :] = x_f32[rows, :].astype(BF16)
            y_cur[rows, :] = jnp.broadcast_to(bd_ref[0], (MOE_SUB, D_MODEL))
            return carry

        lax.fori_loop(0, nblk, stage, 0)

        @pl.when(c + 1 < nu_ref[0])
        def _():
            x_copy(c + 1).start()

    @pl.when(nblk > 0)
    def _():
        def sub_block(r):
            rows = sub_rows(r)
            xr = x_bf[rows, :]
            g = jnp.dot(xr, wg_ref[0], preferred_element_type=F32) + bg_ref[0]
            up = jnp.dot(xr, wu_ref[0], preferred_element_type=F32) + bu_ref[0]
            g = jnp.minimum(g, SWIGLU_LIMIT)
            up = jnp.clip(up, -SWIGLU_LIMIT, SWIGLU_LIMIT)
            glu = g * jax.nn.sigmoid(SWIGLU_ALPHA * g)
            act = ((up + 1.0) * glu).astype(BF16)
            y_cur[rows, :] += jnp.dot(act, wd_ref[0], preferred_element_type=F32)

        def pair(p, carry):
            sub_block(2 * p)
            sub_block(2 * p + 1)
            return carry

        lax.fori_loop(0, nblk // 2, pair, 0)

        @pl.when(nblk % 2 == 1)
        def _():
            sub_block(nblk - 1)

    @pl.when((nblk > 0) & (j == MOE_NJ - 1))
    def _():
        def write_back(r, carry):
            y_copy(r, cs_ref[c] + r).start()
            return carry

        lax.fori_loop(0, nblk, write_back, 0)

    @pl.when((nblk == 0) & (j == 0))
    def _():
        first = tail_ref[0] + MOE_CHUNK_SUBS * (c - nu_ref[0])
        count = jnp.clip(MOE_XROWS // MOE_SUB - first, 0, MOE_CHUNK_SUBS)

        @pl.when(count > 0)
        def _():
            zero_buf[...] = jnp.zeros_like(zero_buf)

            def start(i, carry):
                zero_copy(first + i).start()
                return carry

            def wait(i, carry):
                zero_copy(first + i).wait()
                return carry

            lax.fori_loop(0, count, start, 0)
            lax.fori_loop(0, count, wait, 0)


def moe_experts(x_sorted, chunk_tables, w_gate_up, b_gate_up, w_down, b_down):
    nj = MOE_NJ

    def used_j(c, j, nu):
        return jnp.where(c < nu[0], j, nj - 1)

    return pl.pallas_call(
        _moe_kernel,
        grid_spec=pltpu.PrefetchScalarGridSpec(
            num_scalar_prefetch=5,
            grid=(MOE_NC, nj),
            in_specs=[
                pl.BlockSpec(memory_space=pl.ANY),
                pl.BlockSpec((1, D_MODEL, MOE_TF),
                             lambda c, j, ce, cs, cn, nu, tl: (ce[c], 0, used_j(c, j, nu))),
                pl.BlockSpec((1, D_MODEL, MOE_TF),
                             lambda c, j, ce, cs, cn, nu, tl: (ce[c], 0, nj + used_j(c, j, nu))),
                pl.BlockSpec((1, MOE_TF, D_MODEL),
                             lambda c, j, ce, cs, cn, nu, tl: (ce[c], used_j(c, j, nu), 0)),
                pl.BlockSpec((1, 1, MOE_TF),
                             lambda c, j, ce, cs, cn, nu, tl: (ce[c], 0, used_j(c, j, nu))),
                pl.BlockSpec((1, 1, MOE_TF),
                             lambda c, j, ce, cs, cn, nu, tl: (ce[c], 0, nj + used_j(c, j, nu))),
                pl.BlockSpec((1, 1, D_MODEL), lambda c, j, ce, cs, cn, nu, tl: (ce[c], 0, 0)),
            ],
            out_specs=pl.BlockSpec(memory_space=pl.ANY),
            scratch_shapes=[
                pltpu.VMEM((MOE_R, D_MODEL), F32),
                pltpu.VMEM((MOE_R, D_MODEL), BF16),
                pltpu.VMEM((2, MOE_R, D_MODEL), F32),
                pltpu.VMEM((MOE_SUB, D_MODEL), F32),
                pltpu.SemaphoreType.DMA(()),
                pltpu.SemaphoreType.DMA((2,)),
                pltpu.SemaphoreType.DMA(()),
            ],
        ),
        out_shape=jax.ShapeDtypeStruct((MOE_XROWS, D_MODEL), F32),
        compiler_params=pltpu.CompilerParams(dimension_semantics=("arbitrary", "arbitrary"),
                                             vmem_limit_bytes=MOE_VMEM_LIMIT),
        name="moe_experts",
    )(*chunk_tables, x_sorted, w_gate_up, w_gate_up, w_down, b_gate_up, b_gate_up, b_down)


def route(logits):
    top_val, top_idx = lax.top_k(logits, TOP_K)
    gate = jax.nn.softmax(top_val, axis=-1)
    e_flat = top_idx.reshape(TK_ALL)
    onehot = (e_flat[:, None] == jnp.arange(N_EXPERTS, dtype=e_flat.dtype)[None, :]).astype(jnp.int32)
    csum = jnp.cumsum(onehot, axis=0)
    rank = jnp.sum(csum * onehot, axis=1) - 1
    counts = csum[-1]
    nsub = (counts + MOE_SUB - 1) // MOE_SUB
    sub_base = jnp.cumsum(nsub) - nsub
    nch = (counts + MOE_R - 1) // MOE_R
    ch_end = jnp.cumsum(nch)
    ch_base = ch_end - nch
    n_used = ch_end[-1]
    pos = MOE_SUB * sub_base[e_flat] + rank
    tok = jnp.arange(TK_ALL, dtype=jnp.int32) // TOP_K
    row_tok = jnp.zeros((MOE_XROWS,), jnp.int32).at[pos].set(tok)
    c = jnp.arange(MOE_NC, dtype=jnp.int32)
    cc = jnp.minimum(c, n_used - 1)
    ce = jnp.minimum(jnp.sum(cc[:, None] >= ch_end[None, :], axis=1), N_EXPERTS - 1).astype(jnp.int32)
    kk = cc - ch_base[ce]
    cs = (sub_base[ce] + MOE_CHUNK_SUBS * kk).astype(jnp.int32)
    cn = jnp.where(c < n_used, jnp.minimum(MOE_CHUNK_SUBS, nsub[ce] - MOE_CHUNK_SUBS * kk), 0).astype(jnp.int32)
    tables = (ce, cs, cn, n_used.reshape(1).astype(jnp.int32), jnp.sum(nsub).reshape(1).astype(jnp.int32))
    return gate, row_tok, pos.reshape(T_ALL, TOP_K), tables


def _combine_kernel(alpha, h_ref, yg_ref, gate_ref, g_ref, b_ref, o_ref):
    gate = gate_ref[...]
    f = yg_ref[0] * gate[:, 0:1]
    for k in range(1, TOP_K):
        f = f + yg_ref[k] * gate[:, k:k + 1]
    o_ref[...] = _layer_norm(alpha * h_ref[...] + f, g_ref[...], b_ref[...])


def combine(h, yg, gate, ln_g, ln_b, alpha, row_off, m, tm):
    blk_off = row_off // tm
    whole = lambda i: (0, 0)
    return pl.pallas_call(
        functools.partial(_combine_kernel, alpha),
        grid=(m // tm,),
        in_specs=[
            pl.BlockSpec((tm, D_MODEL), lambda i: (blk_off + i, 0)),
            pl.BlockSpec((TOP_K, tm, D_MODEL), lambda i: (0, blk_off + i, 0)),
            pl.BlockSpec((tm, TOP_K), lambda i: (blk_off + i, 0)),
            pl.BlockSpec((1, D_MODEL), whole),
            pl.BlockSpec((1, D_MODEL), whole),
        ],
        out_specs=pl.BlockSpec((tm, D_MODEL), lambda i: (i, 0)),
        out_shape=jax.ShapeDtypeStruct((m, D_MODEL), F32),
        compiler_params=_params(("parallel",)),
        name="combine",
    )(h, yg, gate, ln_g, ln_b)


def _retention_tables(chunk, group):
    log_gamma = jnp.log(1.0 - jnp.exp2(-5.0 - jnp.arange(N_RET_HEADS, dtype=F32)))
    n = chunk * group
    pos = jnp.arange(n) % chunk
    seq = jnp.arange(n) // chunk
    diff = (pos[:, None] - pos[None, :]).astype(F32)
    same = seq[:, None] == seq[None, :]
    lg = log_gamma[:, None, None]
    dec = jnp.where(same[None] & (diff >= 0)[None], jnp.exp(lg * jnp.maximum(diff, 0.0)[None]), 0.0)
    posf = pos.astype(F32)
    qd = jnp.exp(log_gamma[:, None] * (posf + 1.0))[..., None]
    kd = jnp.exp(log_gamma[:, None] * (chunk - 1.0 - posf))[..., None]
    cd = jnp.exp(log_gamma * chunk)
    return cd, dec, qd, kd


def _alibi_slopes():
    h = jnp.arange(1, N_SWA_HEADS + 1, dtype=F32)
    return jnp.exp2(-8.0 * h / N_SWA_HEADS)


def kernel(x_prompt, x_sample, state_ret, cache_swa_k, cache_swa_v, w_in, w_ret_o, w_swa_o, w_out, swa_sinks,
           ln1_g, ln1_b, w_router, b_router, w_gate_up, b_gate_up, w_down, b_down, ln2_g, ln2_b):
    alpha = (2.0 * DEPTH) ** 0.25
    kv_w = N_SWA_KV_HEADS * SWA_HEAD_DIM
    w = w_in[0]
    w_in_bf = jnp.concatenate([w[:, :8192], w[:, 8704:], w[:, 8192:8704]], axis=1).astype(BF16)
    w_ret_o_bf = w_ret_o[0].astype(BF16)
    w_swa_o_bf = w_swa_o[0].astype(BF16)
    w_out_bf = w_out[0].astype(BF16)
    w_router_pad, b_router_pad = _router_split(w_router[0], b_router[0])
    ln1g, ln1b = ln1_g[0].reshape(1, D_MODEL), ln1_b[0].reshape(1, D_MODEL)
    ln2g, ln2b = ln2_g[0].reshape(1, D_MODEL), ln2_b[0].reshape(1, D_MODEL)
    slopes = _alibi_slopes()
    sinks = swa_sinks[0].astype(F32)

    xp = x_prompt.reshape(T_PROMPT, D_MODEL)
    xs = x_sample.reshape(T_SAMPLE, D_MODEL)

    u_p = in_proj(xp, w_in_bf, 1024, 1280)
    o_r_p, state_p = retention_prompt(u_p, _retention_tables(RET_CHUNK, 1))
    o_s_p = swa_prompt(u_p, slopes, sinks)
    kv_p = u_p.reshape(BATCH, SEQ, D_IN)[:, SEQ - WINDOW:, COL_KVS:]
    k_cache_p = kv_p[..., :kv_w].reshape(1, BATCH, WINDOW, N_SWA_KV_HEADS, SWA_HEAD_DIM)
    v_cache_p = kv_p[..., kv_w:].reshape(1, BATCH, WINDOW, N_SWA_KV_HEADS, SWA_HEAD_DIM)
    merged_p = merge_branches(o_r_p, o_s_p, u_p, w_ret_o_bf, w_swa_o_bf, 512)

    u_s = in_proj(xs, w_in_bf, 512, 1280)
    o_r_s, state_s = retention_sample(u_s, state_ret[0], _retention_tables(DEC_SEQ, RS_BB))
    q_t = (u_s[:, COL_QS:COL_QS + 2048]
           .reshape(DEC_BATCH, DEC_SEQ, N_SWA_KV_HEADS, SWA_GROUP, SWA_HEAD_DIM)
           .transpose(0, 2, 1, 3, 4).reshape(DEC_BATCH, N_SWA_KV_HEADS, SS_ROWS, SWA_HEAD_DIM))
    kv_new = u_s[:, COL_KVS:].reshape(DEC_BATCH, DEC_SEQ, 2 * kv_w)
    o_s_t, k_cache_s, v_cache_s = swa_sample(
        q_t, kv_new, cache_swa_k[0].reshape(DEC_BATCH, WINDOW, kv_w),
        cache_swa_v[0].reshape(DEC_BATCH, WINDOW, kv_w), _swa_sample_tables(slopes, sinks))
    o_s_s = (o_s_t.reshape(DEC_BATCH, N_SWA_KV_HEADS, DEC_SEQ, SWA_GROUP, SWA_HEAD_DIM)
             .transpose(0, 2, 1, 3, 4).reshape(T_SAMPLE, 2048).astype(BF16))
    merged_s = merge_branches(o_r_s, o_s_s, u_s, w_ret_o_bf, w_swa_o_bf, 512)

    h, logits = post_mix(merged_p, merged_s, xp, xs, w_out_bf, ln1g, ln1b, w_router_pad, b_router_pad,
                         alpha, 512)

    gate, row_tok, pos, tables = route(logits[:, :N_EXPERTS])
    x_sorted = h[row_tok]
    y_rows = moe_experts(x_sorted, tables, w_gate_up[0], b_gate_up[0].reshape(N_EXPERTS, 1, 2 * D_FF),
                         w_down[0], b_down[0].reshape(N_EXPERTS, 1, D_MODEL))
    yg = y_rows[pos.T]
    y_p = combine(h, yg, gate, ln2g, ln2b, alpha, 0, T_PROMPT, 256)
    y_s = combine(h, yg, gate, ln2g, ln2b, alpha, T_PROMPT, T_SAMPLE, 256)

    return (y_p.reshape(BATCH, SEQ, D_MODEL), y_s.reshape(DEC_BATCH, DEC_SEQ, D_MODEL),
            state_p[None], k_cache_p, v_cache_p,
            state_s[None], k_cache_s.reshape(1, DEC_BATCH, WINDOW, N_SWA_KV_HEADS, SWA_HEAD_DIM),
            v_cache_s.reshape(1, DEC_BATCH, WINDOW, N_SWA_KV_HEADS, SWA_HEAD_DIM))
```

```python
import functools

import jax
import jax.numpy as jnp
from jax import lax
from jax.experimental import pallas as pl
from jax.experimental.pallas import tpu as pltpu

F32 = jnp.float32
BF16 = jnp.bfloat16

D_MODEL = 2048
BATCH = 4
SEQ = 2048
DEC_BATCH = 128
DEC_SEQ = 4
N_RET_HEADS = 8
RET_DK = 128
RET_DV = 256
RET_CHUNK = 128
N_SWA_HEADS = 32
N_SWA_KV_HEADS = 4
SWA_GROUP = 8
SWA_HEAD_DIM = 64
WINDOW = 128
N_EXPERTS = 32
TOP_K = 4
D_FF = 2048
SWIGLU_LIMIT = 7.0
SWIGLU_ALPHA = 1.702
LN_EPS = 1e-5
GN_EPS = 1e-6
NEG_INF = -1e30
DEPTH = 1

T_PROMPT = BATCH * SEQ
T_SAMPLE = DEC_BATCH * DEC_SEQ
T_ALL = T_PROMPT + T_SAMPLE
TK_ALL = T_ALL * TOP_K
D_IN = 12800

COL_QR, COL_KR, COL_VR, COL_GR, COL_QS, COL_GATE_R, COL_GATE_S, COL_KVS = (
    0, 1024, 2048, 4096, 6144, 8192, 10240, 12288)

MOE_SUB = 128
MOE_CHUNK_SUBS = 10
MOE_STATIC_SIZES = (7, 8, 9, 10)
MOE_R = MOE_SUB * MOE_CHUNK_SUBS
MOE_NSUB_MAX = TK_ALL // MOE_SUB + N_EXPERTS
MOE_XROWS = (MOE_NSUB_MAX + MOE_CHUNK_SUBS - 1) * MOE_SUB
MOE_NC = TK_ALL // MOE_R + N_EXPERTS + 2
MOE_TF = 256
MOE_NJ = D_FF // MOE_TF
MOE_VMEM_LIMIT = 60 * 1024 * 1024

VMEM_LIMIT = 56 * 1024 * 1024


def _params(sem):
    return pltpu.CompilerParams(dimension_semantics=sem, vmem_limit_bytes=VMEM_LIMIT)


def _in_proj_kernel(x_ref, w_ref, o_ref):
    o_ref[...] = jnp.dot(x_ref[...].astype(BF16), w_ref[...], preferred_element_type=F32)


def in_proj(x2d, w_bf, tm, tn):
    m, k = x2d.shape
    n = w_bf.shape[1]
    return pl.pallas_call(
        _in_proj_kernel,
        grid=(m // tm, n // tn),
        in_specs=[pl.BlockSpec((tm, k), lambda i, j: (i, 0)),
                  pl.BlockSpec((k, tn), lambda i, j: (0, j))],
        out_specs=pl.BlockSpec((tm, tn), lambda i, j: (i, j)),
        out_shape=jax.ShapeDtypeStruct((m, n), F32),
        compiler_params=_params(("parallel", "parallel")),
        name="in_proj",
    )(x2d, w_bf)


def _group_norm_gate(o, g):
    mu = jnp.mean(o, axis=-1, keepdims=True)
    oc = o - mu
    var = jnp.mean(oc * oc, axis=-1, keepdims=True)
    return oc * lax.rsqrt(var + GN_EPS) * (g * jax.nn.sigmoid(g))


def _dot_nt(a, b):
    return lax.dot_general(a, b, (((1,), (1,)), ((), ())), preferred_element_type=F32)


def _layer_norm(x, g, b):
    mu = jnp.mean(x, axis=-1, keepdims=True)
    xc = x - mu
    var = jnp.mean(xc * xc, axis=-1, keepdims=True)
    return xc * lax.rsqrt(var + LN_EPS) * g + b


def _ret_prompt_kernel(cd_ref, dec_ref, qd_ref, kd_ref, q_ref, k_ref, v_ref, g_ref, o_ref, st_ref):
    c = pl.program_id(1)

    @pl.when(c == 0)
    def _():
        st_ref[...] = jnp.zeros_like(st_ref)

    for h in range(N_RET_HEADS):
        q = q_ref[:, h * RET_DK:(h + 1) * RET_DK]
        k = k_ref[:, h * RET_DK:(h + 1) * RET_DK] * (RET_DK ** -0.5)
        v = v_ref[:, h * RET_DV:(h + 1) * RET_DV].astype(BF16)
        state = st_ref[0, h]
        scores = _dot_nt(q.astype(BF16), k.astype(BF16)) * dec_ref[h]
        o = (jnp.dot(scores.astype(BF16), v, preferred_element_type=F32)
             + jnp.dot((q * qd_ref[h]).astype(BF16), state.astype(BF16), preferred_element_type=F32))
        kt = (k * kd_ref[h]).T.astype(BF16)
        st_ref[0, h] = cd_ref[h] * state + jnp.dot(kt, v, preferred_element_type=F32)
        g = g_ref[:, h * RET_DV:(h + 1) * RET_DV]
        o_ref[:, h * RET_DV:(h + 1) * RET_DV] = _group_norm_gate(o, g).astype(BF16)


def retention_prompt(u, tables):
    cd, dec, qd, kd = tables
    nc = SEQ // RET_CHUNK
    row = lambda b, c: b * nc + c
    whole3 = lambda b, c: (0, 0, 0)
    return pl.pallas_call(
        _ret_prompt_kernel,
        grid=(BATCH, nc),
        in_specs=[
            pl.BlockSpec(memory_space=pltpu.SMEM),
            pl.BlockSpec((N_RET_HEADS, RET_CHUNK, RET_CHUNK), whole3),
            pl.BlockSpec((N_RET_HEADS, RET_CHUNK, 1), whole3),
            pl.BlockSpec((N_RET_HEADS, RET_CHUNK, 1), whole3),
            pl.BlockSpec((RET_CHUNK, 1024), lambda b, c: (row(b, c), COL_QR // 1024)),
            pl.BlockSpec((RET_CHUNK, 1024), lambda b, c: (row(b, c), COL_KR // 1024)),
            pl.BlockSpec((RET_CHUNK, 2048), lambda b, c: (row(b, c), COL_VR // 2048)),
            pl.BlockSpec((RET_CHUNK, 2048), lambda b, c: (row(b, c), COL_GR // 2048)),
        ],
        out_specs=[
            pl.BlockSpec((RET_CHUNK, 2048), lambda b, c: (row(b, c), 0)),
            pl.BlockSpec((1, N_RET_HEADS, RET_DK, RET_DV), lambda b, c: (b, 0, 0, 0)),
        ],
        out_shape=[
            jax.ShapeDtypeStruct((T_PROMPT, 2048), BF16),
            jax.ShapeDtypeStruct((BATCH, N_RET_HEADS, RET_DK, RET_DV), F32),
        ],
        compiler_params=_params(("parallel", "arbitrary")),
        name="retention_prompt",
    )(cd, dec, qd, kd, u, u, u, u)


RS_BB = 32


def _ret_sample_kernel(cd_ref, dec_ref, qd_ref, kd_ref, q_ref, k_ref, v_ref, g_ref, st_ref,
                       o_ref, sto_ref):
    h = pl.program_id(1)
    q = q_ref[...]
    k = k_ref[...] * (RET_DK ** -0.5)
    v = v_ref[...].astype(BF16)
    scores = _dot_nt(q.astype(BF16), k.astype(BF16)) * dec_ref[0]
    o_intra = jnp.dot(scores.astype(BF16), v, preferred_element_type=F32)
    qq = (q * qd_ref[0]).astype(BF16)
    kt = (k * kd_ref[0]).T
    col_batch = lax.broadcasted_iota(jnp.int32, kt.shape, 1) // DEC_SEQ
    first_of_pair = lax.broadcasted_iota(jnp.int32, (8, RET_DV), 0) < DEC_SEQ
    cd = cd_ref[h]
    pieces = []
    for p in range(RS_BB // 2):
        q8 = qq[8 * p:8 * p + 8]
        s0 = st_ref[2 * p].astype(BF16)
        s1 = st_ref[2 * p + 1].astype(BF16)
        r0 = jnp.dot(q8, s0, preferred_element_type=F32)
        r1 = jnp.dot(q8, s1, preferred_element_type=F32)
        pieces.append(jnp.where(first_of_pair, r0, r1))
    for b in range(RS_BB):
        ktm = jnp.where(col_batch == b, kt, 0.0).astype(BF16)
        sto_ref[b] = cd * st_ref[b] + jnp.dot(ktm, v, preferred_element_type=F32)
    o = o_intra + jnp.concatenate(pieces, axis=0)
    o_ref[...] = _group_norm_gate(o, g_ref[...]).astype(BF16)


def retention_sample(u_s, state, tables):
    cd, dec, qd, kd = tables
    rows = RS_BB * DEC_SEQ
    per_head = lambda i, h: (h, 0, 0)
    st_spec = pl.BlockSpec((RS_BB, None, RET_DK, RET_DV), lambda i, h: (i, h, 0, 0))
    return pl.pallas_call(
        _ret_sample_kernel,
        grid=(DEC_BATCH // RS_BB, N_RET_HEADS),
        in_specs=[
            pl.BlockSpec(memory_space=pltpu.SMEM),
            pl.BlockSpec((1, rows, rows), per_head),
            pl.BlockSpec((1, rows, 1), per_head),
            pl.BlockSpec((1, rows, 1), per_head),
            pl.BlockSpec((rows, RET_DK), lambda i, h: (i, COL_QR // RET_DK + h)),
            pl.BlockSpec((rows, RET_DK), lambda i, h: (i, COL_KR // RET_DK + h)),
            pl.BlockSpec((rows, RET_DV), lambda i, h: (i, COL_VR // RET_DV + h)),
            pl.BlockSpec((rows, RET_DV), lambda i, h: (i, COL_GR // RET_DV + h)),
            st_spec,
        ],
        out_specs=[pl.BlockSpec((rows, RET_DV), lambda i, h: (i, h)), st_spec],
        out_shape=[
            jax.ShapeDtypeStruct((T_SAMPLE, 2048), BF16),
            jax.ShapeDtypeStruct((DEC_BATCH, N_RET_HEADS, RET_DK, RET_DV), F32),
        ],
        compiler_params=_params(("parallel", "parallel")),
        name="retention_sample",
    )(cd, dec, qd, kd, u_s, u_s, u_s, u_s, state)


def _swa_prompt_kernel(sink_ref, bias_ref, q_ref, kvc_ref, kvp_ref, o_ref):
    kv_w = N_SWA_KV_HEADS * SWA_HEAD_DIM
    hd = SWA_HEAD_DIM
    qt = q_ref[...].T
    outs = []
    for kvh in range(N_SWA_KV_HEADS):
        lo = kvh * hd
        k_cat = jnp.concatenate([kvp_ref[:, lo:lo + hd], kvc_ref[:, lo:lo + hd]], axis=0).astype(BF16)
        v_cat = jnp.concatenate([kvp_ref[:, kv_w + lo:kv_w + lo + hd],
                                 kvc_ref[:, kv_w + lo:kv_w + lo + hd]], axis=0)
        vt = v_cat.T.astype(BF16)
        heads = range(kvh * SWA_GROUP, (kvh + 1) * SWA_GROUP)
        q8 = jnp.concatenate([qt[h * hd:(h + 1) * hd, :] for h in heads], axis=1).astype(BF16)
        s = jnp.dot(k_cat, q8, preferred_element_type=F32) * (hd ** -0.5) + bias_ref[0, kvh]
        sink = sink_ref[kvh]
        m = jnp.maximum(jnp.max(s, axis=0, keepdims=True), sink)
        e = jnp.exp(s - m)
        den = jnp.sum(e, axis=0, keepdims=True) + jnp.exp(sink - m)
        p = (e * (1.0 / den)).astype(BF16)
        ot = jnp.dot(vt, p, preferred_element_type=F32)
        outs += [ot[:, g * WINDOW:(g + 1) * WINDOW] for g in range(SWA_GROUP)]
    o_ref[...] = jnp.concatenate(outs, axis=0).T.astype(BF16)


def _swa_prompt_tables(slopes, sinks):
    qi = jnp.arange(WINDOW)[:, None] + WINDOW
    ki = jnp.arange(2 * WINDOW)[None, :]
    dist = qi - ki
    in_window = (dist >= 0) & (dist < WINDOW)
    valid = jnp.stack([in_window & (ki >= WINDOW), in_window])
    pen = -(slopes[:, None, None] * dist.astype(F32)[None])
    bias = jnp.where(valid[:, None], pen[None], NEG_INF)
    bias = bias.reshape(2, N_SWA_KV_HEADS, SWA_GROUP, WINDOW, 2 * WINDOW).transpose(0, 1, 4, 2, 3)
    bias = bias.reshape(2, N_SWA_KV_HEADS, 2 * WINDOW, SWA_GROUP * WINDOW)
    sink_l = jnp.repeat(sinks.reshape(N_SWA_KV_HEADS, SWA_GROUP), WINDOW, axis=1)
    return sink_l.reshape(N_SWA_KV_HEADS, 1, SWA_GROUP * WINDOW), bias


def swa_prompt(u, slopes, sinks):
    nb = SEQ // WINDOW
    kv_blk = COL_KVS // 512
    sink_l, bias = _swa_prompt_tables(slopes, sinks)
    lanes = SWA_GROUP * WINDOW
    return pl.pallas_call(
        _swa_prompt_kernel,
        grid=(BATCH, nb),
        in_specs=[
            pl.BlockSpec((N_SWA_KV_HEADS, 1, lanes), lambda b, j: (0, 0, 0)),
            pl.BlockSpec((1, N_SWA_KV_HEADS, 2 * WINDOW, lanes), lambda b, j: (jnp.minimum(j, 1), 0, 0, 0)),
            pl.BlockSpec((WINDOW, 2048), lambda b, j: (b * nb + j, COL_QS // 2048)),
            pl.BlockSpec((WINDOW, 512), lambda b, j: (b * nb + j, kv_blk)),
            pl.BlockSpec((WINDOW, 512), lambda b, j: (b * nb + jnp.maximum(j - 1, 0), kv_blk)),
        ],
        out_specs=pl.BlockSpec((WINDOW, 2048), lambda b, j: (b * nb + j, 0)),
        out_shape=jax.ShapeDtypeStruct((T_PROMPT, 2048), BF16),
        compiler_params=_params(("parallel", "parallel")),
        name="swa_prompt",
    )(sink_l, bias, u, u, u)


SS_BB = 8
SS_ROWS = DEC_SEQ * SWA_GROUP


def _bf16_round(x):
    return x.astype(BF16).astype(F32)


def _swa_sample_kernel(sink_ref, bias1_ref, bias2_ref, q_ref, kvn_ref, ck_ref, cv_ref,
                       o_ref, cko_ref, cvo_ref):
    kv_w = N_SWA_KV_HEADS * SWA_HEAD_DIM
    hd = SWA_HEAD_DIM
    scale = hd ** -0.5
    for kvh in range(N_SWA_KV_HEADS):
        lo = kvh * hd
        sink = sink_ref[kvh]
        q = q_ref[:, kvh]
        ck = ck_ref[:, :, lo:lo + hd].astype(BF16)
        cv = cv_ref[:, :, lo:lo + hd].astype(BF16)
        kn = _bf16_round(kvn_ref[:, :, lo:lo + hd])
        vn = _bf16_round(kvn_ref[:, :, kv_w + lo:kv_w + lo + hd])
        s1 = lax.dot_general(q.astype(BF16), ck, (((2,), (2,)), ((0,), (0,))),
                             preferred_element_type=F32) * scale + bias1_ref[kvh]
        qr = _bf16_round(q)
        s2 = [jnp.sum(qr * kn[:, t2:t2 + 1, :], axis=-1, keepdims=True) * scale + bias2_ref[kvh, t2]
              for t2 in range(DEC_SEQ)]
        m = jnp.maximum(jnp.max(s1, axis=-1, keepdims=True), sink)
        for st in s2:
            m = jnp.maximum(m, st)
        e1 = jnp.exp(s1 - m)
        e2 = [jnp.exp(st - m) for st in s2]
        den = jnp.sum(e1, axis=-1, keepdims=True) + jnp.exp(sink - m)
        for et in e2:
            den = den + et
        inv = 1.0 / den
        o = lax.dot_general((e1 * inv).astype(BF16), cv, (((2,), (1,)), ((0,), (0,))),
                            preferred_element_type=F32)
        for t2 in range(DEC_SEQ):
            o = o + _bf16_round(e2[t2] * inv) * vn[:, t2:t2 + 1, :]
        o_ref[:, kvh] = o
    keep = WINDOW - DEC_SEQ
    cko_ref[:, 0:keep, :] = ck_ref[:, DEC_SEQ:WINDOW, :]
    cko_ref[:, keep:WINDOW, :] = kvn_ref[:, :, 0:kv_w]
    cvo_ref[:, 0:keep, :] = cv_ref[:, DEC_SEQ:WINDOW, :]
    cvo_ref[:, keep:WINDOW, :] = kvn_ref[:, :, kv_w:2 * kv_w]


def _swa_sample_tables(slopes, sinks):
    g_slopes = jnp.tile(slopes.reshape(N_SWA_KV_HEADS, 1, SWA_GROUP), (1, DEC_SEQ, 1)).reshape(
        N_SWA_KV_HEADS, SS_ROWS)
    sink_rows = jnp.tile(sinks.reshape(N_SWA_KV_HEADS, 1, SWA_GROUP), (1, DEC_SEQ, 1)).reshape(
        N_SWA_KV_HEADS, SS_ROWS, 1)
    t_row = jnp.arange(SS_ROWS) // SWA_GROUP
    jcol = jnp.arange(WINDOW)
    dist1 = (t_row[:, None] + WINDOW - jcol[None, :])
    bias1 = jnp.where((dist1 < WINDOW)[None], -(g_slopes[:, :, None] * dist1.astype(F32)[None]), NEG_INF)
    t2 = jnp.arange(DEC_SEQ)
    dist2 = t_row[None, :] - t2[:, None]
    bias2 = jnp.where((dist2 >= 0)[None], -(g_slopes[:, None, :] * dist2.astype(F32)[None]), NEG_INF)
    return sink_rows, bias1, bias2[..., None]


def swa_sample(q_t, kv_new, cache_k, cache_v, tables):
    sink_rows, bias1, bias2 = tables
    kv_w = N_SWA_KV_HEADS * SWA_HEAD_DIM
    whole3 = lambda i: (0, 0, 0)
    cache_spec = pl.BlockSpec((SS_BB, WINDOW, kv_w), lambda i: (i, 0, 0))
    return pl.pallas_call(
        _swa_sample_kernel,
        grid=(DEC_BATCH // SS_BB,),
        in_specs=[
            pl.BlockSpec((N_SWA_KV_HEADS, SS_ROWS, 1), whole3),
            pl.BlockSpec((N_SWA_KV_HEADS, SS_ROWS, WINDOW), whole3),
            pl.BlockSpec((N_SWA_KV_HEADS, DEC_SEQ, SS_ROWS, 1), lambda i: (0, 0, 0, 0)),
            pl.BlockSpec((SS_BB, N_SWA_KV_HEADS, SS_ROWS, SWA_HEAD_DIM), lambda i: (i, 0, 0, 0)),
            pl.BlockSpec((SS_BB, DEC_SEQ, 2 * kv_w), lambda i: (i, 0, 0)),
            cache_spec, cache_spec,
        ],
        out_specs=[
            pl.BlockSpec((SS_BB, N_SWA_KV_HEADS, SS_ROWS, SWA_HEAD_DIM), lambda i: (i, 0, 0, 0)),
            cache_spec, cache_spec,
        ],
        out_shape=[
            jax.ShapeDtypeStruct((DEC_BATCH, N_SWA_KV_HEADS, SS_ROWS, SWA_HEAD_DIM), F32),
            jax.ShapeDtypeStruct((DEC_BATCH, WINDOW, kv_w), F32),
            jax.ShapeDtypeStruct((DEC_BATCH, WINDOW, kv_w), F32),
        ],
        compiler_params=_params(("parallel",)),
        name="swa_sample",
    )(sink_rows, bias1, bias2, q_t, kv_new, cache_k, cache_v)


def _merge_kernel(or_ref, os_ref, gr_ref, gs_ref, wr_ref, ws_ref, o_ref):
    a = jnp.dot(or_ref[...], wr_ref[...], preferred_element_type=F32)
    b = jnp.dot(os_ref[...], ws_ref[...], preferred_element_type=F32)
    merged = jax.nn.sigmoid(gr_ref[...]) * a + jax.nn.sigmoid(gs_ref[...]) * b
    o_ref[...] = merged.astype(BF16)


def merge_branches(o_r, o_s, u, w_ret_o_bf, w_swa_o_bf, tm):
    m = o_r.shape[0]
    resident = pl.BlockSpec((2048, D_MODEL), lambda i: (0, 0), pipeline_mode=pl.Buffered(1))
    return pl.pallas_call(
        _merge_kernel,
        grid=(m // tm,),
        in_specs=[
            pl.BlockSpec((tm, 2048), lambda i: (i, 0)),
            pl.BlockSpec((tm, 2048), lambda i: (i, 0)),
            pl.BlockSpec((tm, D_MODEL), lambda i: (i, COL_GATE_R // D_MODEL)),
            pl.BlockSpec((tm, D_MODEL), lambda i: (i, COL_GATE_S // D_MODEL)),
            resident, resident,
        ],
        out_specs=pl.BlockSpec((tm, D_MODEL), lambda i: (i, 0)),
        out_shape=jax.ShapeDtypeStruct((m, D_MODEL), BF16),
        compiler_params=_params(("parallel",)),
        name="merge_branches",
    )(o_r, o_s, u, u, w_ret_o_bf, w_swa_o_bf)


ROUTER_PAD = 128


def _post_mix_kernel(alpha, n_prompt_blocks, mp_ref, ms_ref, xp_ref, xs_ref, w_ref, g_ref, b_ref, wr_ref, br_ref,
                     h_ref, lg_ref):
    from_sample = pl.program_id(0) >= n_prompt_blocks
    merged = jnp.where(from_sample, ms_ref[...], mp_ref[...])
    x = jnp.where(from_sample, xs_ref[...], xp_ref[...])
    mix = jnp.dot(merged, w_ref[...], preferred_element_type=F32)
    h = _layer_norm(alpha * x + mix, g_ref[...], b_ref[...])
    h_ref[...] = h
    h_hi = h.astype(BF16)
    h_lo = (h - h_hi.astype(F32)).astype(BF16)
    hh = jnp.dot(h_hi, wr_ref[...], preferred_element_type=F32)
    lh = jnp.dot(h_lo, wr_ref[:, :ROUTER_PAD], preferred_element_type=F32)
    lg_ref[...] = hh[:, :ROUTER_PAD] + hh[:, ROUTER_PAD:] + lh + br_ref[...]


def _router_split(w_router, b_router):
    w = jnp.pad(w_router, ((0, 0), (0, ROUTER_PAD - N_EXPERTS)))
    w_hi = w.astype(BF16)
    w_lo = (w - w_hi.astype(F32)).astype(BF16)
    b = jnp.pad(b_router, (0, ROUTER_PAD - N_EXPERTS)).reshape(1, ROUTER_PAD)
    return jnp.concatenate([w_hi, w_lo], axis=1), b


def post_mix(merged_p, merged_s, xp, xs, w_out_bf, ln_g, ln_b, w_router_pad, b_router_pad, alpha, tm):
    n_p = T_PROMPT // tm
    n_s = T_SAMPLE // tm
    prompt_row = lambda i: (jnp.minimum(i, n_p - 1), 0)
    sample_row = lambda i: (jnp.clip(i - n_p, 0, n_s - 1), 0)
    whole = lambda i: (0, 0)
    row = lambda i: (i, 0)
    return pl.pallas_call(
        functools.partial(_post_mix_kernel, alpha, n_p),
        grid=(n_p + n_s,),
        in_specs=[
            pl.BlockSpec((tm, 2048), prompt_row),
            pl.BlockSpec((tm, 2048), sample_row),
            pl.BlockSpec((tm, 2048), prompt_row),
            pl.BlockSpec((tm, 2048), sample_row),
            pl.BlockSpec((2048, 2048), whole),
            pl.BlockSpec((1, 2048), whole),
            pl.BlockSpec((1, 2048), whole),
            pl.BlockSpec((2048, 2 * ROUTER_PAD), whole),
            pl.BlockSpec((1, ROUTER_PAD), whole),
        ],
        out_specs=[pl.BlockSpec((tm, 2048), row), pl.BlockSpec((tm, ROUTER_PAD), row)],
        out_shape=[jax.ShapeDtypeStruct((T_ALL, D_MODEL), F32),
                   jax.ShapeDtypeStruct((T_ALL, ROUTER_PAD), F32)],
        compiler_params=_params(("parallel",)),
        name="post_mix",
    )(merged_p, merged_s, xp, xs, w_out_bf, ln_g, ln_b, w_router_pad, b_router_pad)


def _moe_kernel(ce_ref, cs_ref, cn_ref, nu_ref, tail_ref, x_hbm, wg_ref, wu_ref, wd_ref, bg_ref, bu_ref, bd_ref,
                y_hbm, x_f32, x_bf, y_acc, zero_buf, x_sem, y_sem, z_sem):
    c = pl.program_id(0)
    j = pl.program_id(1)
    nblk = cn_ref[c]
    slot = c % 2
    y_cur = y_acc.at[slot]

    def sub_rows(r):
        return pl.ds(pl.multiple_of(r * MOE_SUB, MOE_SUB), MOE_SUB)

    def x_copy(chunk):
        start = pl.multiple_of(cs_ref[chunk] * MOE_SUB, MOE_SUB)
        return pltpu.make_async_copy(x_hbm.at[pl.ds(start, MOE_R), :], x_f32, x_sem)

    def y_copy(r, sub):
        return pltpu.make_async_copy(y_cur.at[sub_rows(r), :], y_hbm.at[sub_rows(sub), :], y_sem.at[slot])

    def zero_copy(sub):
        return pltpu.make_async_copy(zero_buf, y_hbm.at[sub_rows(sub), :], z_sem)

    @pl.when(j == 0)
    def _():
        def drain(i, carry):
            y_copy(0, 0).wait()
            return carry

        lax.fori_loop(0, cn_ref[jnp.maximum(c - 2, 0)] * (c >= 2).astype(jnp.int32), drain, 0)

    @pl.when((nblk > 0) & (j == 0))
    def _():
        @pl.when(c == 0)
        def _():
            x_copy(0).start()

        x_copy(c).wait()

        def stage(r, carry):
            rows = sub_rows(r)
            x_bf[rows, :] = x_f32[rows, :].astype(BF16)
            y_cur[rows, :] = jnp.broadcast_to(bd_ref[0], (MOE_SUB, D_MODEL))
            return carry

        lax.fori_loop(0, nblk, stage, 0)

        @pl.when(c + 1 < nu_ref[0])
        def _():
            x_copy(c + 1).start()

    @pl.when(nblk > 0)
    def _():
        def rows_block(rows):
            xr = x_bf[rows, :]
            g = jnp.dot(xr, wg_ref[0], preferred_element_type=F32) + bg_ref[0]
            up = jnp.dot(xr, wu_ref[0], preferred_element_type=F32) + bu_ref[0]
            g = jnp.minimum(g, SWIGLU_LIMIT)
            up = jnp.clip(up, -SWIGLU_LIMIT, SWIGLU_LIMIT)
            glu = g * jax.nn.sigmoid(SWIGLU_ALPHA * g)
            act = ((up + 1.0) * glu).astype(BF16)
            y_cur[rows, :] += jnp.dot(act, wd_ref[0], preferred_element_type=F32)

        for n_static in MOE_STATIC_SIZES:
            @pl.when(nblk == n_static)
            def _(n_static=n_static):
                rows_block(pl.ds(0, n_static * MOE_SUB))

        @pl.when(nblk < MOE_STATIC_SIZES[0])
        def _():
            def one(r, carry):
                rows_block(sub_rows(r))
                return carry

            lax.fori_loop(0, nblk, one, 0)

    @pl.when((nblk > 0) & (j == MOE_NJ - 1))
    def _():
        def write_back(r, carry):
            y_copy(r, cs_ref[c] + r).start()
            return carry

        lax.fori_loop(0, nblk, write_back, 0)

    @pl.when((nblk == 0) & (j == 0))
    def _():
        first = tail_ref[0] + MOE_CHUNK_SUBS * (c - nu_ref[0])
        count = jnp.clip(MOE_XROWS // MOE_SUB - first, 0, MOE_CHUNK_SUBS)

        @pl.when(count > 0)
        def _():
            zero_buf[...] = jnp.zeros_like(zero_buf)

            def start(i, carry):
                zero_copy(first + i).start()
                return carry

            def wait(i, carry):
                zero_copy(first + i).wait()
                return carry

            lax.fori_loop(0, count, start, 0)
            lax.fori_loop(0, count, wait, 0)


def moe_experts(x_sorted, chunk_tables, w_gate_up, b_gate_up, w_down, b_down):
    nj = MOE_NJ

    def used_j(c, j, nu):
        return jnp.where(c < nu[0], j, nj - 1)

    return pl.pallas_call(
        _moe_kernel,
        grid_spec=pltpu.PrefetchScalarGridSpec(
            num_scalar_prefetch=5,
            grid=(MOE_NC, nj),
            in_specs=[
                pl.BlockSpec(memory_space=pl.ANY),
                pl.BlockSpec((1, D_MODEL, MOE_TF),
                             lambda c, j, ce, cs, cn, nu, tl: (ce[c], 0, used_j(c, j, nu))),
                pl.BlockSpec((1, D_MODEL, MOE_TF),
                             lambda c, j, ce, cs, cn, nu, tl: (ce[c], 0, nj + used_j(c, j, nu))),
                pl.BlockSpec((1, MOE_TF, D_MODEL),
                             lambda c, j, ce, cs, cn, nu, tl: (ce[c], used_j(c, j, nu), 0)),
                pl.BlockSpec((1, 1, MOE_TF),
                             lambda c, j, ce, cs, cn, nu, tl: (ce[c], 0, used_j(c, j, nu))),
                pl.BlockSpec((1, 1, MOE_TF),
                             lambda c, j, ce, cs, cn, nu, tl: (ce[c], 0, nj + used_j(c, j, nu))),
                pl.BlockSpec((1, 1, D_MODEL), lambda c, j, ce, cs, cn, nu, tl: (ce[c], 0, 0)),
            ],
            out_specs=pl.BlockSpec(memory_space=pl.ANY),
            scratch_shapes=[
                pltpu.VMEM((MOE_R, D_MODEL), F32),
                pltpu.VMEM((MOE_R, D_MODEL), BF16),
                pltpu.VMEM((2, MOE_R, D_MODEL), F32),
                pltpu.VMEM((MOE_SUB, D_MODEL), F32),
                pltpu.SemaphoreType.DMA(()),
                pltpu.SemaphoreType.DMA((2,)),
                pltpu.SemaphoreType.DMA(()),
            ],
        ),
        out_shape=jax.ShapeDtypeStruct((MOE_XROWS, D_MODEL), F32),
        compiler_params=pltpu.CompilerParams(dimension_semantics=("arbitrary", "arbitrary"),
                                             vmem_limit_bytes=MOE_VMEM_LIMIT),
        name="moe_experts",
    )(*chunk_tables, x_sorted, w_gate_up, w_gate_up, w_down, b_gate_up, b_gate_up, b_down)


def route(logits):
    top_val, top_idx = lax.top_k(logits, TOP_K)
    gate = jax.nn.softmax(top_val, axis=-1)
    e_flat = top_idx.reshape(TK_ALL)
    onehot = (e_flat[:, None] == jnp.arange(N_EXPERTS, dtype=e_flat.dtype)[None, :]).astype(jnp.int32)
    csum = jnp.cumsum(onehot, axis=0)
    rank = jnp.sum(csum * onehot, axis=1) - 1
    counts = csum[-1]
    nsub = (counts + MOE_SUB - 1) // MOE_SUB
    sub_base = jnp.cumsum(nsub) - nsub
    nch = (counts + MOE_R - 1) // MOE_R
    ch_end = jnp.cumsum(nch)
    ch_base = ch_end - nch
    n_used = ch_end[-1]
    pos = MOE_SUB * sub_base[e_flat] + rank
    tok = jnp.arange(TK_ALL, dtype=jnp.int32) // TOP_K
    row_tok = jnp.zeros((MOE_XROWS,), jnp.int32).at[pos].set(tok)
    c = jnp.arange(MOE_NC, dtype=jnp.int32)
    cc = jnp.minimum(c, n_used - 1)
    ce = jnp.minimum(jnp.sum(cc[:, None] >= ch_end[None, :], axis=1), N_EXPERTS - 1).astype(jnp.int32)
    kk = cc - ch_base[ce]
    cs = (sub_base[ce] + MOE_CHUNK_SUBS * kk).astype(jnp.int32)
    cn = jnp.where(c < n_used, jnp.minimum(MOE_CHUNK_SUBS, nsub[ce] - MOE_CHUNK_SUBS * kk), 0).astype(jnp.int32)
    tables = (ce, cs, cn, n_used.reshape(1).astype(jnp.int32), jnp.sum(nsub).reshape(1).astype(jnp.int32))
    return gate, row_tok, pos.reshape(T_ALL, TOP_K), tables


def _combine_kernel(alpha, h_ref, yg_ref, gate_ref, g_ref, b_ref, o_ref):
    gate = gate_ref[...]
    f = yg_ref[0] * gate[:, 0:1]
    for k in range(1, TOP_K):
        f = f + yg_ref[k] * gate[:, k:k + 1]
    o_ref[...] = _layer_norm(alpha * h_ref[...] + f, g_ref[...], b_ref[...])


def combine(h, yg, gate, ln_g, ln_b, alpha, row_off, m, tm):
    blk_off = row_off // tm
    whole = lambda i: (0, 0)
    return pl.pallas_call(
        functools.partial(_combine_kernel, alpha),
        grid=(m // tm,),
        in_specs=[
            pl.BlockSpec((tm, D_MODEL), lambda i: (blk_off + i, 0)),
            pl.BlockSpec((TOP_K, tm, D_MODEL), lambda i: (0, blk_off + i, 0)),
            pl.BlockSpec((tm, TOP_K), lambda i: (blk_off + i, 0)),
            pl.BlockSpec((1, D_MODEL), whole),
            pl.BlockSpec((1, D_MODEL), whole),
        ],
        out_specs=pl.BlockSpec((tm, D_MODEL), lambda i: (i, 0)),
        out_shape=jax.ShapeDtypeStruct((m, D_MODEL), F32),
        compiler_params=_params(("parallel",)),
        name="combine",
    )(h, yg, gate, ln_g, ln_b)


def _retention_tables(chunk, group):
    log_gamma = jnp.log(1.0 - jnp.exp2(-5.0 - jnp.arange(N_RET_HEADS, dtype=F32)))
    n = chunk * group
    pos = jnp.arange(n) % chunk
    seq = jnp.arange(n) // chunk
    diff = (pos[:, None] - pos[None, :]).astype(F32)
    same = seq[:, None] == seq[None, :]
    lg = log_gamma[:, None, None]
    dec = jnp.where(same[None] & (diff >= 0)[None], jnp.exp(lg * jnp.maximum(diff, 0.0)[None]), 0.0)
    posf = pos.astype(F32)
    qd = jnp.exp(log_gamma[:, None] * (posf + 1.0))[..., None]
    kd = jnp.exp(log_gamma[:, None] * (chunk - 1.0 - posf))[..., None]
    cd = jnp.exp(log_gamma * chunk)
    return cd, dec, qd, kd


def _alibi_slopes():
    h = jnp.arange(1, N_SWA_HEADS + 1, dtype=F32)
    return jnp.exp2(-8.0 * h / N_SWA_HEADS)


def kernel(x_prompt, x_sample, state_ret, cache_swa_k, cache_swa_v, w_in, w_ret_o, w_swa_o, w_out, swa_sinks,
           ln1_g, ln1_b, w_router, b_router, w_gate_up, b_gate_up, w_down, b_down, ln2_g, ln2_b):
    alpha = (2.0 * DEPTH) ** 0.25
    kv_w = N_SWA_KV_HEADS * SWA_HEAD_DIM
    w = w_in[0]
    w_in_bf = jnp.concatenate([w[:, :8192], w[:, 8704:], w[:, 8192:8704]], axis=1).astype(BF16)
    w_ret_o_bf = w_ret_o[0].astype(BF16)
    w_swa_o_bf = w_swa_o[0].astype(BF16)
    w_out_bf = w_out[0].astype(BF16)
    w_router_pad, b_router_pad = _router_split(w_router[0], b_router[0])
    ln1g, ln1b = ln1_g[0].reshape(1, D_MODEL), ln1_b[0].reshape(1, D_MODEL)
    ln2g, ln2b = ln2_g[0].reshape(1, D_MODEL), ln2_b[0].reshape(1, D_MODEL)
    slopes = _alibi_slopes()
    sinks = swa_sinks[0].astype(F32)

    xp = x_prompt.reshape(T_PROMPT, D_MODEL)
    xs = x_sample.reshape(T_SAMPLE, D_MODEL)

    u_p = in_proj(xp, w_in_bf, 1024, 1280)
    o_r_p, state_p = retention_prompt(u_p, _retention_tables(RET_CHUNK, 1))
    o_s_p = swa_prompt(u_p, slopes, sinks)
    kv_p = u_p.reshape(BATCH, SEQ, D_IN)[:, SEQ - WINDOW:, COL_KVS:]
    k_cache_p = kv_p[..., :kv_w].reshape(1, BATCH, WINDOW, N_SWA_KV_HEADS, SWA_HEAD_DIM)
    v_cache_p = kv_p[..., kv_w:].reshape(1, BATCH, WINDOW, N_SWA_KV_HEADS, SWA_HEAD_DIM)
    merged_p = merge_branches(o_r_p, o_s_p, u_p, w_ret_o_bf, w_swa_o_bf, 512)

    u_s = in_proj(xs, w_in_bf, 512, 1280)
    o_r_s, state_s = retention_sample(u_s, state_ret[0], _retention_tables(DEC_SEQ, RS_BB))
    q_t = (u_s[:, COL_QS:COL_QS + 2048]
           .reshape(DEC_BATCH, DEC_SEQ, N_SWA_KV_HEADS, SWA_GROUP, SWA_HEAD_DIM)
           .transpose(0, 2, 1, 3, 4).reshape(DEC_BATCH, N_SWA_KV_HEADS, SS_ROWS, SWA_HEAD_DIM))
    kv_new = u_s[:, COL_KVS:].reshape(DEC_BATCH, DEC_SEQ, 2 * kv_w)
    o_s_t, k_cache_s, v_cache_s = swa_sample(
        q_t, kv_new, cache_swa_k[0].reshape(DEC_BATCH, WINDOW, kv_w),
        cache_swa_v[0].reshape(DEC_BATCH, WINDOW, kv_w), _swa_sample_tables(slopes, sinks))
    o_s_s = (o_s_t.reshape(DEC_BATCH, N_SWA_KV_HEADS, DEC_SEQ, SWA_GROUP, SWA_HEAD_DIM)
             .transpose(0, 2, 1, 3, 4).reshape(T_SAMPLE, 2048).astype(BF16))
    merged_s = merge_branches(o_r_s, o_s_s, u_s, w_ret_o_bf, w_swa_o_bf, 512)

    h, logits = post_mix(merged_p, merged_s, xp, xs, w_out_bf, ln1g, ln1b, w_router_pad, b_router_pad,
                         alpha, 512)

    gate, row_tok, pos, tables = route(logits[:, :N_EXPERTS])
    x_sorted = h[row_tok]
    y_rows = moe_experts(x_sorted, tables, w_gate_up[0], b_gate_up[0].reshape(N_EXPERTS, 1, 2 * D_FF),
                         w_down[0], b_down[0].reshape(N_EXPERTS, 1, D_MODEL))
    yg = y_rows[pos.T]
    y_p = combine(h, yg, gate, ln2g, ln2b, alpha, 0, T_PROMPT, 256)
    y_s = combine(h, yg, gate, ln2g, ln2b, alpha, T_PROMPT, T_SAMPLE, 256)

    return (y_p.reshape(BATCH, SEQ, D_MODEL), y_s.reshape(DEC_BATCH, DEC_SEQ, D_MODEL),
            state_p[None], k_cache_p, v_cache_p,
            state_s[None], k_cache_s.reshape(1, DEC_BATCH, WINDOW, N_SWA_KV_HEADS, SWA_HEAD_DIM),
            v_cache_s.reshape(1, DEC_BATCH, WINDOW, N_SWA_KV_HEADS, SWA_HEAD_DIM))
```

```python
import functools

import jax
import jax.numpy as jnp
from jax import lax
from jax.experimental import pallas as pl
from jax.experimental.pallas import tpu as pltpu

F32 = jnp.float32
BF16 = jnp.bfloat16

D_MODEL = 2048
BATCH = 4
SEQ = 2048
DEC_BATCH = 128
DEC_SEQ = 4
N_RET_HEADS = 8
RET_DK = 128
RET_DV = 256
RET_CHUNK = 128
N_SWA_HEADS = 32
N_SWA_KV_HEADS = 4
SWA_GROUP = 8
SWA_HEAD_DIM = 64
WINDOW = 128
N_EXPERTS = 32
TOP_K = 4
D_FF = 2048
SWIGLU_LIMIT = 7.0
SWIGLU_ALPHA = 1.702
LN_EPS = 1e-5
GN_EPS = 1e-6
NEG_INF = -1e30
DEPTH = 1

T_PROMPT = BATCH * SEQ
T_SAMPLE = DEC_BATCH * DEC_SEQ
T_ALL = T_PROMPT + T_SAMPLE
TK_ALL = T_ALL * TOP_K
D_IN = 12800

COL_QR, COL_KR, COL_VR, COL_GR, COL_QS, COL_GATE_R, COL_GATE_S, COL_KVS = (
    0, 1024, 2048, 4096, 6144, 8192, 10240, 12288)

MOE_SUB = 128
MOE_CHUNK_SUBS = 10
MOE_STATIC_SIZES = (7, 8, 9, 10)
MOE_R = MOE_SUB * MOE_CHUNK_SUBS
MOE_NSUB_MAX = TK_ALL // MOE_SUB + N_EXPERTS
MOE_XROWS = (MOE_NSUB_MAX + MOE_CHUNK_SUBS - 1) * MOE_SUB
MOE_NC = TK_ALL // MOE_R + N_EXPERTS + 2
MOE_TF = 256
MOE_NJ = D_FF // MOE_TF
MOE_VMEM_LIMIT = 60 * 1024 * 1024

VMEM_LIMIT = 56 * 1024 * 1024


def _params(sem):
    return pltpu.CompilerParams(dimension_semantics=sem, vmem_limit_bytes=VMEM_LIMIT)


def _in_proj_kernel(x_ref, w_ref, o_ref):
    o_ref[...] = jnp.dot(x_ref[...].astype(BF16), w_ref[...], preferred_element_type=F32)


def in_proj(x2d, w_bf, tm, tn):
    m, k = x2d.shape
    n = w_bf.shape[1]
    return pl.pallas_call(
        _in_proj_kernel,
        grid=(m // tm, n // tn),
        in_specs=[pl.BlockSpec((tm, k), lambda i, j: (i, 0)),
                  pl.BlockSpec((k, tn), lambda i, j: (0, j))],
        out_specs=pl.BlockSpec((tm, tn), lambda i, j: (i, j)),
        out_shape=jax.ShapeDtypeStruct((m, n), F32),
        compiler_params=_params(("parallel", "parallel")),
        name="in_proj",
    )(x2d, w_bf)


def _group_norm_gate(o, g):
    mu = jnp.mean(o, axis=-1, keepdims=True)
    oc = o - mu
    var = jnp.mean(oc * oc, axis=-1, keepdims=True)
    return oc * lax.rsqrt(var + GN_EPS) * (g * jax.nn.sigmoid(g))


def _dot_nt(a, b):
    return lax.dot_general(a, b, (((1,), (1,)), ((), ())), preferred_element_type=F32)


def _layer_norm(x, g, b):
    mu = jnp.mean(x, axis=-1, keepdims=True)
    xc = x - mu
    var = jnp.mean(xc * xc, axis=-1, keepdims=True)
    return xc * lax.rsqrt(var + LN_EPS) * g + b


def _ret_prompt_kernel(cd_ref, dec_ref, qd_ref, kd_ref, q_ref, k_ref, v_ref, g_ref, o_ref, st_ref):
    c = pl.program_id(1)

    @pl.when(c == 0)
    def _():
        st_ref[...] = jnp.zeros_like(st_ref)

    for h in range(N_RET_HEADS):
        q = q_ref[:, h * RET_DK:(h + 1) * RET_DK]
        k = k_ref[:, h * RET_DK:(h + 1) * RET_DK] * (RET_DK ** -0.5)
        v = v_ref[:, h * RET_DV:(h + 1) * RET_DV].astype(BF16)
        state = st_ref[0, h]
        scores = _dot_nt(q.astype(BF16), k.astype(BF16)) * dec_ref[h]
        o = (jnp.dot(scores.astype(BF16), v, preferred_element_type=F32)
             + jnp.dot((q * qd_ref[h]).astype(BF16), state.astype(BF16), preferred_element_type=F32))
        kt = (k * kd_ref[h]).T.astype(BF16)
        st_ref[0, h] = cd_ref[h] * state + jnp.dot(kt, v, preferred_element_type=F32)
        g = g_ref[:, h * RET_DV:(h + 1) * RET_DV]
        o_ref[:, h * RET_DV:(h + 1) * RET_DV] = _group_norm_gate(o, g).astype(BF16)


def retention_prompt(u, tables):
    cd, dec, qd, kd = tables
    nc = SEQ // RET_CHUNK
    row = lambda b, c: b * nc + c
    whole3 = lambda b, c: (0, 0, 0)
    return pl.pallas_call(
        _ret_prompt_kernel,
        grid=(BATCH, nc),
        in_specs=[
            pl.BlockSpec(memory_space=pltpu.SMEM),
            pl.BlockSpec((N_RET_HEADS, RET_CHUNK, RET_CHUNK), whole3),
            pl.BlockSpec((N_RET_HEADS, RET_CHUNK, 1), whole3),
            pl.BlockSpec((N_RET_HEADS, RET_CHUNK, 1), whole3),
            pl.BlockSpec((RET_CHUNK, 1024), lambda b, c: (row(b, c), COL_QR // 1024)),
            pl.BlockSpec((RET_CHUNK, 1024), lambda b, c: (row(b, c), COL_KR // 1024)),
            pl.BlockSpec((RET_CHUNK, 2048), lambda b, c: (row(b, c), COL_VR // 2048)),
            pl.BlockSpec((RET_CHUNK, 2048), lambda b, c: (row(b, c), COL_GR // 2048)),
        ],
        out_specs=[
            pl.BlockSpec((RET_CHUNK, 2048), lambda b, c: (row(b, c), 0)),
            pl.BlockSpec((1, N_RET_HEADS, RET_DK, RET_DV), lambda b, c: (b, 0, 0, 0)),
        ],
        out_shape=[
            jax.ShapeDtypeStruct((T_PROMPT, 2048), BF16),
            jax.ShapeDtypeStruct((BATCH, N_RET_HEADS, RET_DK, RET_DV), F32),
        ],
        compiler_params=_params(("parallel", "arbitrary")),
        name="retention_prompt",
    )(cd, dec, qd, kd, u, u, u, u)


RS_BB = 32


def _ret_sample_kernel(cd_ref, dec_ref, qd_ref, kd_ref, q_ref, k_ref, v_ref, g_ref, st_ref,
                       o_ref, sto_ref):
    h = pl.program_id(1)
    q = q_ref[...]
    k = k_ref[...] * (RET_DK ** -0.5)
    v = v_ref[...].astype(BF16)
    scores = _dot_nt(q.astype(BF16), k.astype(BF16)) * dec_ref[0]
    o_intra = jnp.dot(scores.astype(BF16), v, preferred_element_type=F32)
    qq = (q * qd_ref[0]).astype(BF16)
    kt = (k * kd_ref[0]).T
    col_batch = lax.broadcasted_iota(jnp.int32, kt.shape, 1) // DEC_SEQ
    first_of_pair = lax.broadcasted_iota(jnp.int32, (8, RET_DV), 0) < DEC_SEQ
    cd = cd_ref[h]
    pieces = []
    for p in range(RS_BB // 2):
        q8 = qq[8 * p:8 * p + 8]
        s0 = st_ref[2 * p].astype(BF16)
        s1 = st_ref[2 * p + 1].astype(BF16)
        r0 = jnp.dot(q8, s0, preferred_element_type=F32)
        r1 = jnp.dot(q8, s1, preferred_element_type=F32)
        pieces.append(jnp.where(first_of_pair, r0, r1))
    for b in range(RS_BB):
        ktm = jnp.where(col_batch == b, kt, 0.0).astype(BF16)
        sto_ref[b] = cd * st_ref[b] + jnp.dot(ktm, v, preferred_element_type=F32)
    o = o_intra + jnp.concatenate(pieces, axis=0)
    o_ref[...] = _group_norm_gate(o, g_ref[...]).astype(BF16)


def retention_sample(u_s, state, tables):
    cd, dec, qd, kd = tables
    rows = RS_BB * DEC_SEQ
    per_head = lambda i, h: (h, 0, 0)
    st_spec = pl.BlockSpec((RS_BB, None, RET_DK, RET_DV), lambda i, h: (i, h, 0, 0))
    return pl.pallas_call(
        _ret_sample_kernel,
        grid=(DEC_BATCH // RS_BB, N_RET_HEADS),
        in_specs=[
            pl.BlockSpec(memory_space=pltpu.SMEM),
            pl.BlockSpec((1, rows, rows), per_head),
            pl.BlockSpec((1, rows, 1), per_head),
            pl.BlockSpec((1, rows, 1), per_head),
            pl.BlockSpec((rows, RET_DK), lambda i, h: (i, COL_QR // RET_DK + h)),
            pl.BlockSpec((rows, RET_DK), lambda i, h: (i, COL_KR // RET_DK + h)),
            pl.BlockSpec((rows, RET_DV), lambda i, h: (i, COL_VR // RET_DV + h)),
            pl.BlockSpec((rows, RET_DV), lambda i, h: (i, COL_GR // RET_DV + h)),
            st_spec,
        ],
        out_specs=[pl.BlockSpec((rows, RET_DV), lambda i, h: (i, h)), st_spec],
        out_shape=[
            jax.ShapeDtypeStruct((T_SAMPLE, 2048), BF16),
            jax.ShapeDtypeStruct((DEC_BATCH, N_RET_HEADS, RET_DK, RET_DV), F32),
        ],
        compiler_params=_params(("parallel", "parallel")),
        name="retention_sample",
    )(cd, dec, qd, kd, u_s, u_s, u_s, u_s, state)


def _swa_prompt_kernel(sink_ref, bias_ref, q_ref, kvc_ref, kvp_ref, o_ref):
    kv_w = N_SWA_KV_HEADS * SWA_HEAD_DIM
    hd = SWA_HEAD_DIM
    qt = q_ref[...].T
    outs = []
    for kvh in range(N_SWA_KV_HEADS):
        lo = kvh * hd
        k_cat = jnp.concatenate([kvp_ref[:, lo:lo + hd], kvc_ref[:, lo:lo + hd]], axis=0).astype(BF16)
        v_cat = jnp.concatenate([kvp_ref[:, kv_w + lo:kv_w + lo + hd],
                                 kvc_ref[:, kv_w + lo:kv_w + lo + hd]], axis=0)
        vt = v_cat.T.astype(BF16)
        heads = range(kvh * SWA_GROUP, (kvh + 1) * SWA_GROUP)
        q8 = jnp.concatenate([qt[h * hd:(h + 1) * hd, :] for h in heads], axis=1).astype(BF16)
        s = jnp.dot(k_cat, q8, preferred_element_type=F32) * (hd ** -0.5) + bias_ref[0, kvh]
        sink = sink_ref[kvh]
        m = jnp.maximum(jnp.max(s, axis=0, keepdims=True), sink)
        e = jnp.exp(s - m)
        den = jnp.sum(e, axis=0, keepdims=True) + jnp.exp(sink - m)
        p = (e * (1.0 / den)).astype(BF16)
        ot = jnp.dot(vt, p, preferred_element_type=F32)
        outs += [ot[:, g * WINDOW:(g + 1) * WINDOW] for g in range(SWA_GROUP)]
    o_ref[...] = jnp.concatenate(outs, axis=0).T.astype(BF16)


def _swa_prompt_tables(slopes, sinks):
    qi = jnp.arange(WINDOW)[:, None] + WINDOW
    ki = jnp.arange(2 * WINDOW)[None, :]
    dist = qi - ki
    in_window = (dist >= 0) & (dist < WINDOW)
    valid = jnp.stack([in_window & (ki >= WINDOW), in_window])
    pen = -(slopes[:, None, None] * dist.astype(F32)[None])
    bias = jnp.where(valid[:, None], pen[None], NEG_INF)
    bias = bias.reshape(2, N_SWA_KV_HEADS, SWA_GROUP, WINDOW, 2 * WINDOW).transpose(0, 1, 4, 2, 3)
    bias = bias.reshape(2, N_SWA_KV_HEADS, 2 * WINDOW, SWA_GROUP * WINDOW)
    sink_l = jnp.repeat(sinks.reshape(N_SWA_KV_HEADS, SWA_GROUP), WINDOW, axis=1)
    return sink_l.reshape(N_SWA_KV_HEADS, 1, SWA_GROUP * WINDOW), bias


def swa_prompt(u, slopes, sinks):
    nb = SEQ // WINDOW
    kv_blk = COL_KVS // 512
    sink_l, bias = _swa_prompt_tables(slopes, sinks)
    lanes = SWA_GROUP * WINDOW
    return pl.pallas_call(
        _swa_prompt_kernel,
        grid=(BATCH, nb),
        in_specs=[
            pl.BlockSpec((N_SWA_KV_HEADS, 1, lanes), lambda b, j: (0, 0, 0)),
            pl.BlockSpec((1, N_SWA_KV_HEADS, 2 * WINDOW, lanes), lambda b, j: (jnp.minimum(j, 1), 0, 0, 0)),
            pl.BlockSpec((WINDOW, 2048), lambda b, j: (b * nb + j, COL_QS // 2048)),
            pl.BlockSpec((WINDOW, 512), lambda b, j: (b * nb + j, kv_blk)),
            pl.BlockSpec((WINDOW, 512), lambda b, j: (b * nb + jnp.maximum(j - 1, 0), kv_blk)),
        ],
        out_specs=pl.BlockSpec((WINDOW, 2048), lambda b, j: (b * nb + j, 0)),
        out_shape=jax.ShapeDtypeStruct((T_PROMPT, 2048), BF16),
        compiler_params=_params(("parallel", "parallel")),
        name="swa_prompt",
    )(sink_l, bias, u, u, u)


SS_BB = 8
SS_ROWS = DEC_SEQ * SWA_GROUP


def _bf16_round(x):
    return x.astype(BF16).astype(F32)


def _swa_sample_kernel(sink_ref, bias1_ref, bias2_ref, q_ref, kvn_ref, ck_ref, cv_ref,
                       o_ref, cko_ref, cvo_ref):
    kv_w = N_SWA_KV_HEADS * SWA_HEAD_DIM
    hd = SWA_HEAD_DIM
    scale = hd ** -0.5
    for kvh in range(N_SWA_KV_HEADS):
        lo = kvh * hd
        sink = sink_ref[kvh]
        q = q_ref[:, kvh]
        ck = ck_ref[:, :, lo:lo + hd].astype(BF16)
        cv = cv_ref[:, :, lo:lo + hd].astype(BF16)
        kn = _bf16_round(kvn_ref[:, :, lo:lo + hd])
        vn = _bf16_round(kvn_ref[:, :, kv_w + lo:kv_w + lo + hd])
        s1 = lax.dot_general(q.astype(BF16), ck, (((2,), (2,)), ((0,), (0,))),
                             preferred_element_type=F32) * scale + bias1_ref[kvh]
        qr = _bf16_round(q)
        s2 = [jnp.sum(qr * kn[:, t2:t2 + 1, :], axis=-1, keepdims=True) * scale + bias2_ref[kvh, t2]
              for t2 in range(DEC_SEQ)]
        m = jnp.maximum(jnp.max(s1, axis=-1, keepdims=True), sink)
        for st in s2:
            m = jnp.maximum(m, st)
        e1 = jnp.exp(s1 - m)
        e2 = [jnp.exp(st - m) for st in s2]
        den = jnp.sum(e1, axis=-1, keepdims=True) + jnp.exp(sink - m)
        for et in e2:
            den = den + et
        inv = 1.0 / den
        o = lax.dot_general((e1 * inv).astype(BF16), cv, (((2,), (1,)), ((0,), (0,))),
                            preferred_element_type=F32)
        for t2 in range(DEC_SEQ):
            o = o + _bf16_round(e2[t2] * inv) * vn[:, t2:t2 + 1, :]
        o_ref[:, kvh] = o
    keep = WINDOW - DEC_SEQ
    cko_ref[:, 0:keep, :] = ck_ref[:, DEC_SEQ:WINDOW, :]
    cko_ref[:, keep:WINDOW, :] = kvn_ref[:, :, 0:kv_w]
    cvo_ref[:, 0:keep, :] = cv_ref[:, DEC_SEQ:WINDOW, :]
    cvo_ref[:, keep:WINDOW, :] = kvn_ref[:, :, kv_w:2 * kv_w]


def _swa_sample_tables(slopes, sinks):
    g_slopes = jnp.tile(slopes.reshape(N_SWA_KV_HEADS, 1, SWA_GROUP), (1, DEC_SEQ, 1)).reshape(
        N_SWA_KV_HEADS, SS_ROWS)
    sink_rows = jnp.tile(sinks.reshape(N_SWA_KV_HEADS, 1, SWA_GROUP), (1, DEC_SEQ, 1)).reshape(
        N_SWA_KV_HEADS, SS_ROWS, 1)
    t_row = jnp.arange(SS_ROWS) // SWA_GROUP
    jcol = jnp.arange(WINDOW)
    dist1 = (t_row[:, None] + WINDOW - jcol[None, :])
    bias1 = jnp.where((dist1 < WINDOW)[None], -(g_slopes[:, :, None] * dist1.astype(F32)[None]), NEG_INF)
    t2 = jnp.arange(DEC_SEQ)
    dist2 = t_row[None, :] - t2[:, None]
    bias2 = jnp.where((dist2 >= 0)[None], -(g_slopes[:, None, :] * dist2.astype(F32)[None]), NEG_INF)
    return sink_rows, bias1, bias2[..., None]


def swa_sample(q_t, kv_new, cache_k, cache_v, tables):
    sink_rows, bias1, bias2 = tables
    kv_w = N_SWA_KV_HEADS * SWA_HEAD_DIM
    whole3 = lambda i: (0, 0, 0)
    cache_spec = pl.BlockSpec((SS_BB, WINDOW, kv_w), lambda i: (i, 0, 0))
    return pl.pallas_call(
        _swa_sample_kernel,
        grid=(DEC_BATCH // SS_BB,),
        in_specs=[
            pl.BlockSpec((N_SWA_KV_HEADS, SS_ROWS, 1), whole3),
            pl.BlockSpec((N_SWA_KV_HEADS, SS_ROWS, WINDOW), whole3),
            pl.BlockSpec((N_SWA_KV_HEADS, DEC_SEQ, SS_ROWS, 1), lambda i: (0, 0, 0, 0)),
            pl.BlockSpec((SS_BB, N_SWA_KV_HEADS, SS_ROWS, SWA_HEAD_DIM), lambda i: (i, 0, 0, 0)),
            pl.BlockSpec((SS_BB, DEC_SEQ, 2 * kv_w), lambda i: (i, 0, 0)),
            cache_spec, cache_spec,
        ],
        out_specs=[
            pl.BlockSpec((SS_BB, N_SWA_KV_HEADS, SS_ROWS, SWA_HEAD_DIM), lambda i: (i, 0, 0, 0)),
            cache_spec, cache_spec,
        ],
        out_shape=[
            jax.ShapeDtypeStruct((DEC_BATCH, N_SWA_KV_HEADS, SS_ROWS, SWA_HEAD_DIM), F32),
            jax.ShapeDtypeStruct((DEC_BATCH, WINDOW, kv_w), F32),
            jax.ShapeDtypeStruct((DEC_BATCH, WINDOW, kv_w), F32),
        ],
        compiler_params=_params(("parallel",)),
        name="swa_sample",
    )(sink_rows, bias1, bias2, q_t, kv_new, cache_k, cache_v)


def _merge_kernel(or_ref, os_ref, gr_ref, gs_ref, wr_ref, ws_ref, o_ref):
    a = jnp.dot(or_ref[...], wr_ref[...], preferred_element_type=F32)
    b = jnp.dot(os_ref[...], ws_ref[...], preferred_element_type=F32)
    merged = jax.nn.sigmoid(gr_ref[...]) * a + jax.nn.sigmoid(gs_ref[...]) * b
    o_ref[...] = merged.astype(BF16)


def merge_branches(o_r, o_s, u, w_ret_o_bf, w_swa_o_bf, tm):
    m = o_r.shape[0]
    resident = pl.BlockSpec((2048, D_MODEL), lambda i: (0, 0), pipeline_mode=pl.Buffered(1))
    return pl.pallas_call(
        _merge_kernel,
        grid=(m // tm,),
        in_specs=[
            pl.BlockSpec((tm, 2048), lambda i: (i, 0)),
            pl.BlockSpec((tm, 2048), lambda i: (i, 0)),
            pl.BlockSpec((tm, D_MODEL), lambda i: (i, COL_GATE_R // D_MODEL)),
            pl.BlockSpec((tm, D_MODEL), lambda i: (i, COL_GATE_S // D_MODEL)),
            resident, resident,
        ],
        out_specs=pl.BlockSpec((tm, D_MODEL), lambda i: (i, 0)),
        out_shape=jax.ShapeDtypeStruct((m, D_MODEL), BF16),
        compiler_params=_params(("parallel",)),
        name="merge_branches",
    )(o_r, o_s, u, u, w_ret_o_bf, w_swa_o_bf)


ROUTER_PAD = 128


def _post_mix_kernel(alpha, n_prompt_blocks, mp_ref, ms_ref, xp_ref, xs_ref, w_ref, g_ref, b_ref, wr_ref, br_ref,
                     h_ref, lg_ref):
    from_sample = pl.program_id(0) >= n_prompt_blocks
    merged = jnp.where(from_sample, ms_ref[...], mp_ref[...])
    x = jnp.where(from_sample, xs_ref[...], xp_ref[...])
    mix = jnp.dot(merged, w_ref[...], preferred_element_type=F32)
    h = _layer_norm(alpha * x + mix, g_ref[...], b_ref[...])
    h_ref[...] = h
    h_hi = h.astype(BF16)
    h_lo = (h - h_hi.astype(F32)).astype(BF16)
    hh = jnp.dot(h_hi, wr_ref[...], preferred_element_type=F32)
    lh = jnp.dot(h_lo, wr_ref[:, :ROUTER_PAD], preferred_element_type=F32)
    lg_ref[...] = (hh[:, :ROUTER_PAD] + hh[:, ROUTER_PAD:] + lh + br_ref[...]).T


def _router_split(w_router, b_router):
    w = jnp.pad(w_router, ((0, 0), (0, ROUTER_PAD - N_EXPERTS)))
    w_hi = w.astype(BF16)
    w_lo = (w - w_hi.astype(F32)).astype(BF16)
    b = jnp.pad(b_router, (0, ROUTER_PAD - N_EXPERTS)).reshape(1, ROUTER_PAD)
    return jnp.concatenate([w_hi, w_lo], axis=1), b


def post_mix(merged_p, merged_s, xp, xs, w_out_bf, ln_g, ln_b, w_router_pad, b_router_pad, alpha, tm):
    n_p = T_PROMPT // tm
    n_s = T_SAMPLE // tm
    prompt_row = lambda i: (jnp.minimum(i, n_p - 1), 0)
    sample_row = lambda i: (jnp.clip(i - n_p, 0, n_s - 1), 0)
    whole = lambda i: (0, 0)
    row = lambda i: (i, 0)
    return pl.pallas_call(
        functools.partial(_post_mix_kernel, alpha, n_p),
        grid=(n_p + n_s,),
        in_specs=[
            pl.BlockSpec((tm, 2048), prompt_row),
            pl.BlockSpec((tm, 2048), sample_row),
            pl.BlockSpec((tm, 2048), prompt_row),
            pl.BlockSpec((tm, 2048), sample_row),
            pl.BlockSpec((2048, 2048), whole),
            pl.BlockSpec((1, 2048), whole),
            pl.BlockSpec((1, 2048), whole),
            pl.BlockSpec((2048, 2 * ROUTER_PAD), whole),
            pl.BlockSpec((1, ROUTER_PAD), whole),
        ],
        out_specs=[pl.BlockSpec((tm, 2048), row), pl.BlockSpec((ROUTER_PAD, tm), lambda i: (0, i))],
        out_shape=[jax.ShapeDtypeStruct((T_ALL, D_MODEL), F32),
                   jax.ShapeDtypeStruct((ROUTER_PAD, T_ALL), F32)],
        compiler_params=_params(("parallel",)),
        name="post_mix",
    )(merged_p, merged_s, xp, xs, w_out_bf, ln_g, ln_b, w_router_pad, b_router_pad)


def _moe_kernel(ce_ref, cs_ref, cn_ref, nu_ref, tail_ref, x_hbm, wg_ref, wu_ref, wd_ref, bg_ref, bu_ref, bd_ref,
                y_hbm, x_f32, x_bf, y_acc, zero_buf, x_sem, y_sem, z_sem):
    c = pl.program_id(0)
    j = pl.program_id(1)
    nblk = cn_ref[c]
    slot = c % 2
    y_cur = y_acc.at[slot]

    def sub_rows(r):
        return pl.ds(pl.multiple_of(r * MOE_SUB, MOE_SUB), MOE_SUB)

    def x_copy(chunk):
        start = pl.multiple_of(cs_ref[chunk] * MOE_SUB, MOE_SUB)
        return pltpu.make_async_copy(x_hbm.at[pl.ds(start, MOE_R), :], x_f32, x_sem)

    def y_copy(r, sub):
        return pltpu.make_async_copy(y_cur.at[sub_rows(r), :], y_hbm.at[sub_rows(sub), :], y_sem.at[slot])

    def zero_copy(sub):
        return pltpu.make_async_copy(zero_buf, y_hbm.at[sub_rows(sub), :], z_sem)

    @pl.when(j == 0)
    def _():
        def drain(i, carry):
            y_copy(0, 0).wait()
            return carry

        lax.fori_loop(0, cn_ref[jnp.maximum(c - 2, 0)] * (c >= 2).astype(jnp.int32), drain, 0)

    @pl.when((nblk > 0) & (j == 0))
    def _():
        @pl.when(c == 0)
        def _():
            x_copy(0).start()

        x_copy(c).wait()

        def stage(r, carry):
            rows = sub_rows(r)
            x_bf[rows, :] = x_f32[rows, :].astype(BF16)
            y_cur[rows, :] = jnp.broadcast_to(bd_ref[0], (MOE_SUB, D_MODEL))
            return carry

        lax.fori_loop(0, nblk, stage, 0)

        @pl.when(c + 1 < nu_ref[0])
        def _():
            x_copy(c + 1).start()

    @pl.when(nblk > 0)
    def _():
        def rows_block(rows):
            xr = x_bf[rows, :]
            g = jnp.dot(xr, wg_ref[0], preferred_element_type=F32) + bg_ref[0]
            up = jnp.dot(xr, wu_ref[0], preferred_element_type=F32) + bu_ref[0]
            g = jnp.minimum(g, SWIGLU_LIMIT)
            up = jnp.clip(up, -SWIGLU_LIMIT, SWIGLU_LIMIT)
            glu = g * jax.nn.sigmoid(SWIGLU_ALPHA * g)
            act = ((up + 1.0) * glu).astype(BF16)
            y_cur[rows, :] += jnp.dot(act, wd_ref[0], preferred_element_type=F32)

        for n_static in MOE_STATIC_SIZES:
            @pl.when(nblk == n_static)
            def _(n_static=n_static):
                rows_block(pl.ds(0, n_static * MOE_SUB))

        @pl.when(nblk < MOE_STATIC_SIZES[0])
        def _():
            def one(r, carry):
                rows_block(sub_rows(r))
                return carry

            lax.fori_loop(0, nblk, one, 0)

    @pl.when((nblk > 0) & (j == MOE_NJ - 1))
    def _():
        def write_back(r, carry):
            y_copy(r, cs_ref[c] + r).start()
            return carry

        lax.fori_loop(0, nblk, write_back, 0)

    @pl.when((nblk == 0) & (j == 0))
    def _():
        first = tail_ref[0] + MOE_CHUNK_SUBS * (c - nu_ref[0])
        count = jnp.clip(MOE_XROWS // MOE_SUB - first, 0, MOE_CHUNK_SUBS)

        @pl.when(count > 0)
        def _():
            zero_buf[...] = jnp.zeros_like(zero_buf)

            def start(i, carry):
                zero_copy(first + i).start()
                return carry

            def wait(i, carry):
                zero_copy(first + i).wait()
                return carry

            lax.fori_loop(0, count, start, 0)
            lax.fori_loop(0, count, wait, 0)


def moe_experts(x_sorted, chunk_tables, w_gate_up, b_gate_up, w_down, b_down):
    nj = MOE_NJ

    def used_j(c, j, nu):
        return jnp.where(c < nu[0], j, nj - 1)

    return pl.pallas_call(
        _moe_kernel,
        grid_spec=pltpu.PrefetchScalarGridSpec(
            num_scalar_prefetch=5,
            grid=(MOE_NC, nj),
            in_specs=[
                pl.BlockSpec(memory_space=pl.ANY),
                pl.BlockSpec((1, D_MODEL, MOE_TF),
                             lambda c, j, ce, cs, cn, nu, tl: (ce[c], 0, used_j(c, j, nu))),
                pl.BlockSpec((1, D_MODEL, MOE_TF),
                             lambda c, j, ce, cs, cn, nu, tl: (ce[c], 0, nj + used_j(c, j, nu))),
                pl.BlockSpec((1, MOE_TF, D_MODEL),
                             lambda c, j, ce, cs, cn, nu, tl: (ce[c], used_j(c, j, nu), 0)),
                pl.BlockSpec((1, 1, MOE_TF),
                             lambda c, j, ce, cs, cn, nu, tl: (ce[c], 0, used_j(c, j, nu))),
                pl.BlockSpec((1, 1, MOE_TF),
                             lambda c, j, ce, cs, cn, nu, tl: (ce[c], 0, nj + used_j(c, j, nu))),
                pl.BlockSpec((1, 1, D_MODEL), lambda c, j, ce, cs, cn, nu, tl: (ce[c], 0, 0)),
            ],
            out_specs=pl.BlockSpec(memory_space=pl.ANY),
            scratch_shapes=[
                pltpu.VMEM((MOE_R, D_MODEL), F32),
                pltpu.VMEM((MOE_R, D_MODEL), BF16),
                pltpu.VMEM((2, MOE_R, D_MODEL), F32),
                pltpu.VMEM((MOE_SUB, D_MODEL), F32),
                pltpu.SemaphoreType.DMA(()),
                pltpu.SemaphoreType.DMA((2,)),
                pltpu.SemaphoreType.DMA(()),
            ],
        ),
        out_shape=jax.ShapeDtypeStruct((MOE_XROWS, D_MODEL), F32),
        compiler_params=pltpu.CompilerParams(dimension_semantics=("arbitrary", "arbitrary"),
                                             vmem_limit_bytes=MOE_VMEM_LIMIT),
        name="moe_experts",
    )(*chunk_tables, x_sorted, w_gate_up, w_gate_up, w_down, b_gate_up, b_gate_up, b_down)


ROUTE_TM = 512


def _route_kernel(lg_ref, tri_ref, e_ref, rank_ref, gate_ref, cnt_ref, seen):
    @pl.when(pl.program_id(0) == 0)
    def _():
        seen[...] = jnp.zeros_like(seen)

    work = lg_ref[0:N_EXPERTS, :]
    eidx = lax.broadcasted_iota(jnp.int32, work.shape, 0)
    vals, idxs, hots = [], [], []
    for _ in range(TOP_K):
        m = jnp.max(work, axis=0, keepdims=True)
        idx = jnp.min(jnp.where(work == m, eidx, N_EXPERTS), axis=0, keepdims=True)
        hot = eidx == idx
        vals.append(m)
        idxs.append(idx)
        hots.append(hot)
        work = jnp.where(hot, -jnp.inf, work)
    ex = [jnp.exp(v - vals[0]) for v in vals]
    inv = 1.0 / (ex[0] + ex[1] + ex[2] + ex[3])
    picked = jnp.zeros(work.shape, F32)
    for hot in hots:
        picked = picked + hot.astype(F32)
    before = jnp.dot(picked.astype(BF16), tri_ref[...], preferred_element_type=F32) + seen[...]
    ranks = [jnp.sum(jnp.where(hot, before, 0.0), axis=0, keepdims=True) for hot in hots]
    seen[...] += jnp.sum(picked, axis=1, keepdims=True)
    e_ref[...] = jnp.concatenate(idxs, axis=0)
    rank_ref[...] = jnp.concatenate(ranks, axis=0).astype(jnp.int32)
    gate_ref[...] = jnp.concatenate([e * inv for e in ex], axis=0)
    cnt_ref[...] = seen[...]


def route_topk(logits_t):
    tm = ROUTE_TM
    tri = (jnp.arange(tm)[:, None] < jnp.arange(tm)[None, :]).astype(BF16)
    pick = pl.BlockSpec((TOP_K, tm), lambda i: (0, i))
    return pl.pallas_call(
        _route_kernel,
        grid=(T_ALL // tm,),
        in_specs=[pl.BlockSpec((ROUTER_PAD, tm), lambda i: (0, i)),
                  pl.BlockSpec((tm, tm), lambda i: (0, 0))],
        out_specs=[pick, pick, pick, pl.BlockSpec((N_EXPERTS, 1), lambda i: (0, 0))],
        out_shape=[jax.ShapeDtypeStruct((TOP_K, T_ALL), jnp.int32),
                   jax.ShapeDtypeStruct((TOP_K, T_ALL), jnp.int32),
                   jax.ShapeDtypeStruct((TOP_K, T_ALL), F32),
                   jax.ShapeDtypeStruct((N_EXPERTS, 1), F32)],
        scratch_shapes=[pltpu.VMEM((N_EXPERTS, 1), F32)],
        compiler_params=_params(("arbitrary",)),
        name="route_topk",
    )(logits_t, tri)


def route(logits_t):
    e_kt, rank, gate_kt, counts_f = route_topk(logits_t)
    counts = counts_f.reshape(N_EXPERTS).astype(jnp.int32)
    nsub = (counts + MOE_SUB - 1) // MOE_SUB
    sub_base = jnp.cumsum(nsub) - nsub
    nch = (counts + MOE_R - 1) // MOE_R
    ch_end = jnp.cumsum(nch)
    ch_base = ch_end - nch
    n_used = ch_end[-1]
    pos = MOE_SUB * sub_base[e_kt] + rank
    tok = jnp.broadcast_to(jnp.arange(T_ALL, dtype=jnp.int32)[None, :], pos.shape)
    row_tok = jnp.zeros((MOE_XROWS,), jnp.int32).at[pos.reshape(-1)].set(
        tok.reshape(-1), unique_indices=True, mode="promise_in_bounds")
    c = jnp.arange(MOE_NC, dtype=jnp.int32)
    cc = jnp.minimum(c, n_used - 1)
    ce = jnp.minimum(jnp.sum(cc[:, None] >= ch_end[None, :], axis=1), N_EXPERTS - 1).astype(jnp.int32)
    kk = cc - ch_base[ce]
    cs = (sub_base[ce] + MOE_CHUNK_SUBS * kk).astype(jnp.int32)
    cn = jnp.where(c < n_used, jnp.minimum(MOE_CHUNK_SUBS, nsub[ce] - MOE_CHUNK_SUBS * kk), 0).astype(jnp.int32)
    tables = (ce, cs, cn, n_used.reshape(1).astype(jnp.int32), jnp.sum(nsub).reshape(1).astype(jnp.int32))
    return gate_kt.T, row_tok, pos, tables


def _combine_kernel(alpha, h_ref, yg_ref, gate_ref, g_ref, b_ref, o_ref):
    gate = gate_ref[...]
    f = yg_ref[0] * gate[:, 0:1]
    for k in range(1, TOP_K):
        f = f + yg_ref[k] * gate[:, k:k + 1]
    o_ref[...] = _layer_norm(alpha * h_ref[...] + f, g_ref[...], b_ref[...])


def combine(h, yg, gate, ln_g, ln_b, alpha, row_off, m, tm):
    blk_off = row_off // tm
    whole = lambda i: (0, 0)
    return pl.pallas_call(
        functools.partial(_combine_kernel, alpha),
        grid=(m // tm,),
        in_specs=[
            pl.BlockSpec((tm, D_MODEL), lambda i: (blk_off + i, 0)),
            pl.BlockSpec((TOP_K, tm, D_MODEL), lambda i: (0, blk_off + i, 0)),
            pl.BlockSpec((tm, TOP_K), lambda i: (blk_off + i, 0)),
            pl.BlockSpec((1, D_MODEL), whole),
            pl.BlockSpec((1, D_MODEL), whole),
        ],
        out_specs=pl.BlockSpec((tm, D_MODEL), lambda i: (i, 0)),
        out_shape=jax.ShapeDtypeStruct((m, D_MODEL), F32),
        compiler_params=_params(("parallel",)),
        name="combine",
    )(h, yg, gate, ln_g, ln_b)


def _retention_tables(chunk, group):
    log_gamma = jnp.log(1.0 - jnp.exp2(-5.0 - jnp.arange(N_RET_HEADS, dtype=F32)))
    n = chunk * group
    pos = jnp.arange(n) % chunk
    seq = jnp.arange(n) // chunk
    diff = (pos[:, None] - pos[None, :]).astype(F32)
    same = seq[:, None] == seq[None, :]
    lg = log_gamma[:, None, None]
    dec = jnp.where(same[None] & (diff >= 0)[None], jnp.exp(lg * jnp.maximum(diff, 0.0)[None]), 0.0)
    posf = pos.astype(F32)
    qd = jnp.exp(log_gamma[:, None] * (posf + 1.0))[..., None]
    kd = jnp.exp(log_gamma[:, None] * (chunk - 1.0 - posf))[..., None]
    cd = jnp.exp(log_gamma * chunk)
    return cd, dec, qd, kd


def _alibi_slopes():
    h = jnp.arange(1, N_SWA_HEADS + 1, dtype=F32)
    return jnp.exp2(-8.0 * h / N_SWA_HEADS)


def kernel(x_prompt, x_sample, state_ret, cache_swa_k, cache_swa_v, w_in, w_ret_o, w_swa_o, w_out, swa_sinks,
           ln1_g, ln1_b, w_router, b_router, w_gate_up, b_gate_up, w_down, b_down, ln2_g, ln2_b):
    alpha = (2.0 * DEPTH) ** 0.25
    kv_w = N_SWA_KV_HEADS * SWA_HEAD_DIM
    w = w_in[0]
    w_in_bf = jnp.concatenate([w[:, :8192], w[:, 8704:], w[:, 8192:8704]], axis=1).astype(BF16)
    w_ret_o_bf = w_ret_o[0].astype(BF16)
    w_swa_o_bf = w_swa_o[0].astype(BF16)
    w_out_bf = w_out[0].astype(BF16)
    w_router_pad, b_router_pad = _router_split(w_router[0], b_router[0])
    ln1g, ln1b = ln1_g[0].reshape(1, D_MODEL), ln1_b[0].reshape(1, D_MODEL)
    ln2g, ln2b = ln2_g[0].reshape(1, D_MODEL), ln2_b[0].reshape(1, D_MODEL)
    slopes = _alibi_slopes()
    sinks = swa_sinks[0].astype(F32)

    xp = x_prompt.reshape(T_PROMPT, D_MODEL)
    xs = x_sample.reshape(T_SAMPLE, D_MODEL)

    u_p = in_proj(xp, w_in_bf, 1024, 1280)
    o_r_p, state_p = retention_prompt(u_p, _retention_tables(RET_CHUNK, 1))
    o_s_p = swa_prompt(u_p, slopes, sinks)
    kv_p = u_p.reshape(BATCH, SEQ, D_IN)[:, SEQ - WINDOW:, COL_KVS:]
    k_cache_p = kv_p[..., :kv_w].reshape(1, BATCH, WINDOW, N_SWA_KV_HEADS, SWA_HEAD_DIM)
    v_cache_p = kv_p[..., kv_w:].reshape(1, BATCH, WINDOW, N_SWA_KV_HEADS, SWA_HEAD_DIM)
    merged_p = merge_branches(o_r_p, o_s_p, u_p, w_ret_o_bf, w_swa_o_bf, 512)

    u_s = in_proj(xs, w_in_bf, 512, 1280)
    o_r_s, state_s = retention_sample(u_s, state_ret[0], _retention_tables(DEC_SEQ, RS_BB))
    q_t = (u_s[:, COL_QS:COL_QS + 2048]
           .reshape(DEC_BATCH, DEC_SEQ, N_SWA_KV_HEADS, SWA_GROUP, SWA_HEAD_DIM)
           .transpose(0, 2, 1, 3, 4).reshape(DEC_BATCH, N_SWA_KV_HEADS, SS_ROWS, SWA_HEAD_DIM))
    kv_new = u_s[:, COL_KVS:].reshape(DEC_BATCH, DEC_SEQ, 2 * kv_w)
    o_s_t, k_cache_s, v_cache_s = swa_sample(
        q_t, kv_new, cache_swa_k[0].reshape(DEC_BATCH, WINDOW, kv_w),
        cache_swa_v[0].reshape(DEC_BATCH, WINDOW, kv_w), _swa_sample_tables(slopes, sinks))
    o_s_s = (o_s_t.reshape(DEC_BATCH, N_SWA_KV_HEADS, DEC_SEQ, SWA_GROUP, SWA_HEAD_DIM)
             .transpose(0, 2, 1, 3, 4).reshape(T_SAMPLE, 2048).astype(BF16))
    merged_s = merge_branches(o_r_s, o_s_s, u_s, w_ret_o_bf, w_swa_o_bf, 512)

    h, logits = post_mix(merged_p, merged_s, xp, xs, w_out_bf, ln1g, ln1b, w_router_pad, b_router_pad,
                         alpha, 512)

    gate, row_tok, pos, tables = route(logits)
    x_sorted = h.at[row_tok].get(mode="promise_in_bounds")
    y_rows = moe_experts(x_sorted, tables, w_gate_up[0], b_gate_up[0].reshape(N_EXPERTS, 1, 2 * D_FF),
                         w_down[0], b_down[0].reshape(N_EXPERTS, 1, D_MODEL))
    yg = y_rows.at[pos].get(mode="promise_in_bounds")
    y_p = combine(h, yg, gate, ln2g, ln2b, alpha, 0, T_PROMPT, 256)
    y_s = combine(h, yg, gate, ln2g, ln2b, alpha, T_PROMPT, T_SAMPLE, 256)

    return (y_p.reshape(BATCH, SEQ, D_MODEL), y_s.reshape(DEC_BATCH, DEC_SEQ, D_MODEL),
            state_p[None], k_cache_p, v_cache_p,
            state_s[None], k_cache_s.reshape(1, DEC_BATCH, WINDOW, N_SWA_KV_HEADS, SWA_HEAD_DIM),
            v_cache_s.reshape(1, DEC_BATCH, WINDOW, N_SWA_KV_HEADS, SWA_HEAD_DIM))
```

```python
import functools

import jax
import jax.numpy as jnp
from jax import lax
from jax.experimental import pallas as pl
from jax.experimental.pallas import tpu as pltpu

F32 = jnp.float32
BF16 = jnp.bfloat16

D_MODEL = 2048
BATCH = 4
SEQ = 2048
DEC_BATCH = 128
DEC_SEQ = 4
N_RET_HEADS = 8
RET_DK = 128
RET_DV = 256
RET_CHUNK = 128
N_SWA_HEADS = 32
N_SWA_KV_HEADS = 4
SWA_GROUP = 8
SWA_HEAD_DIM = 64
WINDOW = 128
N_EXPERTS = 32
TOP_K = 4
D_FF = 2048
SWIGLU_LIMIT = 7.0
SWIGLU_ALPHA = 1.702
LN_EPS = 1e-5
GN_EPS = 1e-6
NEG_INF = -1e30
DEPTH = 1

T_PROMPT = BATCH * SEQ
T_SAMPLE = DEC_BATCH * DEC_SEQ
T_ALL = T_PROMPT + T_SAMPLE
TK_ALL = T_ALL * TOP_K
D_IN = 12800

COL_QR, COL_KR, COL_VR, COL_GR, COL_QS, COL_KVS, COL_GATE_R, COL_GATE_S = (
    0, 1024, 2048, 4096, 6144, 8192, 8704, 10752)

MOE_SUB = 128
MOE_CHUNK_SUBS = 10
MOE_STATIC_SIZES = (7, 8, 9, 10)
MOE_R = MOE_SUB * MOE_CHUNK_SUBS
MOE_NSUB_MAX = TK_ALL // MOE_SUB + N_EXPERTS
MOE_XROWS = (MOE_NSUB_MAX + MOE_CHUNK_SUBS - 1) * MOE_SUB
MOE_NC = TK_ALL // MOE_R + N_EXPERTS + 2
MOE_TF = 256
MOE_NJ = D_FF // MOE_TF
MOE_VMEM_LIMIT = 60 * 1024 * 1024

VMEM_LIMIT = 56 * 1024 * 1024


def _params(sem):
    return pltpu.CompilerParams(dimension_semantics=sem, vmem_limit_bytes=VMEM_LIMIT)


def _in_proj_kernel(x_ref, w_ref, o_ref):
    o_ref[...] = jnp.dot(x_ref[...].astype(BF16), w_ref[...], preferred_element_type=F32)


def in_proj(x2d, w_bf, tm, tn):
    m, k = x2d.shape
    n = w_bf.shape[1]
    return pl.pallas_call(
        _in_proj_kernel,
        grid=(m // tm, n // tn),
        in_specs=[pl.BlockSpec((tm, k), lambda i, j: (i, 0)),
                  pl.BlockSpec((k, tn), lambda i, j: (0, j))],
        out_specs=pl.BlockSpec((tm, tn), lambda i, j: (i, j)),
        out_shape=jax.ShapeDtypeStruct((m, n), F32),
        compiler_params=_params(("parallel", "parallel")),
        name="in_proj",
    )(x2d, w_bf)


def _group_norm_gate(o, g):
    mu = jnp.mean(o, axis=-1, keepdims=True)
    oc = o - mu
    var = jnp.mean(oc * oc, axis=-1, keepdims=True)
    return oc * lax.rsqrt(var + GN_EPS) * (g * jax.nn.sigmoid(g))


def _dot_nt(a, b):
    return lax.dot_general(a, b, (((1,), (1,)), ((), ())), preferred_element_type=F32)


def _layer_norm(x, g, b):
    mu = jnp.mean(x, axis=-1, keepdims=True)
    xc = x - mu
    var = jnp.mean(xc * xc, axis=-1, keepdims=True)
    return xc * lax.rsqrt(var + LN_EPS) * g + b


def _ret_prompt_kernel(cd_ref, dec_ref, qd_ref, kd_ref, q_ref, k_ref, v_ref, g_ref, o_ref, st_ref):
    c = pl.program_id(1)

    @pl.when(c == 0)
    def _():
        st_ref[...] = jnp.zeros_like(st_ref)

    for h in range(N_RET_HEADS):
        q = q_ref[:, h * RET_DK:(h + 1) * RET_DK]
        k = k_ref[:, h * RET_DK:(h + 1) * RET_DK] * (RET_DK ** -0.5)
        v = v_ref[:, h * RET_DV:(h + 1) * RET_DV].astype(BF16)
        state = st_ref[0, h]
        scores = _dot_nt(q.astype(BF16), k.astype(BF16)) * dec_ref[h]
        o = (jnp.dot(scores.astype(BF16), v, preferred_element_type=F32)
             + jnp.dot((q * qd_ref[h]).astype(BF16), state.astype(BF16), preferred_element_type=F32))
        kt = (k * kd_ref[h]).T.astype(BF16)
        st_ref[0, h] = cd_ref[h] * state + jnp.dot(kt, v, preferred_element_type=F32)
        g = g_ref[:, h * RET_DV:(h + 1) * RET_DV]
        o_ref[:, h * RET_DV:(h + 1) * RET_DV] = _group_norm_gate(o, g).astype(BF16)


def retention_prompt(u, tables):
    cd, dec, qd, kd = tables
    nc = SEQ // RET_CHUNK
    row = lambda b, c: b * nc + c
    whole3 = lambda b, c: (0, 0, 0)
    return pl.pallas_call(
        _ret_prompt_kernel,
        grid=(BATCH, nc),
        in_specs=[
            pl.BlockSpec(memory_space=pltpu.SMEM),
            pl.BlockSpec((N_RET_HEADS, RET_CHUNK, RET_CHUNK), whole3),
            pl.BlockSpec((N_RET_HEADS, RET_CHUNK, 1), whole3),
            pl.BlockSpec((N_RET_HEADS, RET_CHUNK, 1), whole3),
            pl.BlockSpec((RET_CHUNK, 1024), lambda b, c: (row(b, c), COL_QR // 1024)),
            pl.BlockSpec((RET_CHUNK, 1024), lambda b, c: (row(b, c), COL_KR // 1024)),
            pl.BlockSpec((RET_CHUNK, 2048), lambda b, c: (row(b, c), COL_VR // 2048)),
            pl.BlockSpec((RET_CHUNK, 2048), lambda b, c: (row(b, c), COL_GR // 2048)),
        ],
        out_specs=[
            pl.BlockSpec((RET_CHUNK, 2048), lambda b, c: (row(b, c), 0)),
            pl.BlockSpec((1, N_RET_HEADS, RET_DK, RET_DV), lambda b, c: (b, 0, 0, 0)),
        ],
        out_shape=[
            jax.ShapeDtypeStruct((T_PROMPT, 2048), BF16),
            jax.ShapeDtypeStruct((BATCH, N_RET_HEADS, RET_DK, RET_DV), F32),
        ],
        compiler_params=_params(("parallel", "arbitrary")),
        name="retention_prompt",
    )(cd, dec, qd, kd, u, u, u, u)


RS_BB = 32


def _ret_sample_kernel(cd_ref, dec_ref, qd_ref, kd_ref, q_ref, k_ref, v_ref, g_ref, st_ref,
                       o_ref, sto_ref):
    h = pl.program_id(1)
    q = q_ref[...]
    k = k_ref[...] * (RET_DK ** -0.5)
    v = v_ref[...].astype(BF16)
    scores = _dot_nt(q.astype(BF16), k.astype(BF16)) * dec_ref[0]
    o_intra = jnp.dot(scores.astype(BF16), v, preferred_element_type=F32)
    qq = (q * qd_ref[0]).astype(BF16)
    kt = (k * kd_ref[0]).T
    col_batch = lax.broadcasted_iota(jnp.int32, kt.shape, 1) // DEC_SEQ
    first_of_pair = lax.broadcasted_iota(jnp.int32, (8, RET_DV), 0) < DEC_SEQ
    cd = cd_ref[h]
    pieces = []
    for p in range(RS_BB // 2):
        q8 = qq[8 * p:8 * p + 8]
        s0 = st_ref[2 * p].astype(BF16)
        s1 = st_ref[2 * p + 1].astype(BF16)
        r0 = jnp.dot(q8, s0, preferred_element_type=F32)
        r1 = jnp.dot(q8, s1, preferred_element_type=F32)
        pieces.append(jnp.where(first_of_pair, r0, r1))
    for b in range(RS_BB):
        ktm = jnp.where(col_batch == b, kt, 0.0).astype(BF16)
        sto_ref[b] = cd * st_ref[b] + jnp.dot(ktm, v, preferred_element_type=F32)
    o = o_intra + jnp.concatenate(pieces, axis=0)
    o_ref[...] = _group_norm_gate(o, g_ref[...]).astype(BF16)


def retention_sample(u_s, state, tables):
    cd, dec, qd, kd = tables
    rows = RS_BB * DEC_SEQ
    per_head = lambda i, h: (h, 0, 0)
    st_spec = pl.BlockSpec((RS_BB, None, RET_DK, RET_DV), lambda i, h: (i, h, 0, 0))
    return pl.pallas_call(
        _ret_sample_kernel,
        grid=(DEC_BATCH // RS_BB, N_RET_HEADS),
        in_specs=[
            pl.BlockSpec(memory_space=pltpu.SMEM),
            pl.BlockSpec((1, rows, rows), per_head),
            pl.BlockSpec((1, rows, 1), per_head),
            pl.BlockSpec((1, rows, 1), per_head),
            pl.BlockSpec((rows, RET_DK), lambda i, h: (i, COL_QR // RET_DK + h)),
            pl.BlockSpec((rows, RET_DK), lambda i, h: (i, COL_KR // RET_DK + h)),
            pl.BlockSpec((rows, RET_DV), lambda i, h: (i, COL_VR // RET_DV + h)),
            pl.BlockSpec((rows, RET_DV), lambda i, h: (i, COL_GR // RET_DV + h)),
            st_spec,
        ],
        out_specs=[pl.BlockSpec((rows, RET_DV), lambda i, h: (i, h)), st_spec],
        out_shape=[
            jax.ShapeDtypeStruct((T_SAMPLE, 2048), BF16),
            jax.ShapeDtypeStruct((DEC_BATCH, N_RET_HEADS, RET_DK, RET_DV), F32),
        ],
        compiler_params=_params(("parallel", "parallel")),
        name="retention_sample",
    )(cd, dec, qd, kd, u_s, u_s, u_s, u_s, state)


def _swa_prompt_kernel(sink_ref, bias_ref, q_ref, kvc_ref, kvp_ref, o_ref):
    kv_w = N_SWA_KV_HEADS * SWA_HEAD_DIM
    hd = SWA_HEAD_DIM
    qt = q_ref[...].T
    outs = []
    for kvh in range(N_SWA_KV_HEADS):
        lo = kvh * hd
        k_cat = jnp.concatenate([kvp_ref[:, lo:lo + hd], kvc_ref[:, lo:lo + hd]], axis=0).astype(BF16)
        v_cat = jnp.concatenate([kvp_ref[:, kv_w + lo:kv_w + lo + hd],
                                 kvc_ref[:, kv_w + lo:kv_w + lo + hd]], axis=0)
        vt = v_cat.T.astype(BF16)
        heads = range(kvh * SWA_GROUP, (kvh + 1) * SWA_GROUP)
        q8 = jnp.concatenate([qt[h * hd:(h + 1) * hd, :] for h in heads], axis=1).astype(BF16)
        s = jnp.dot(k_cat, q8, preferred_element_type=F32) * (hd ** -0.5) + bias_ref[0, kvh]
        sink = sink_ref[kvh]
        m = jnp.maximum(jnp.max(s, axis=0, keepdims=True), sink)
        e = jnp.exp(s - m)
        den = jnp.sum(e, axis=0, keepdims=True) + jnp.exp(sink - m)
        p = (e * (1.0 / den)).astype(BF16)
        ot = jnp.dot(vt, p, preferred_element_type=F32)
        outs += [ot[:, g * WINDOW:(g + 1) * WINDOW] for g in range(SWA_GROUP)]
    o_ref[...] = jnp.concatenate(outs, axis=0).T.astype(BF16)


def _swa_prompt_tables(slopes, sinks):
    qi = jnp.arange(WINDOW)[:, None] + WINDOW
    ki = jnp.arange(2 * WINDOW)[None, :]
    dist = qi - ki
    in_window = (dist >= 0) & (dist < WINDOW)
    valid = jnp.stack([in_window & (ki >= WINDOW), in_window])
    pen = -(slopes[:, None, None] * dist.astype(F32)[None])
    bias = jnp.where(valid[:, None], pen[None], NEG_INF)
    bias = bias.reshape(2, N_SWA_KV_HEADS, SWA_GROUP, WINDOW, 2 * WINDOW).transpose(0, 1, 4, 2, 3)
    bias = bias.reshape(2, N_SWA_KV_HEADS, 2 * WINDOW, SWA_GROUP * WINDOW)
    sink_l = jnp.repeat(sinks.reshape(N_SWA_KV_HEADS, SWA_GROUP), WINDOW, axis=1)
    return sink_l.reshape(N_SWA_KV_HEADS, 1, SWA_GROUP * WINDOW), bias


def swa_prompt(u, slopes, sinks):
    nb = SEQ // WINDOW
    kv_blk = COL_KVS // 512
    sink_l, bias = _swa_prompt_tables(slopes, sinks)
    lanes = SWA_GROUP * WINDOW
    return pl.pallas_call(
        _swa_prompt_kernel,
        grid=(BATCH, nb),
        in_specs=[
            pl.BlockSpec((N_SWA_KV_HEADS, 1, lanes), lambda b, j: (0, 0, 0)),
            pl.BlockSpec((1, N_SWA_KV_HEADS, 2 * WINDOW, lanes), lambda b, j: (jnp.minimum(j, 1), 0, 0, 0)),
            pl.BlockSpec((WINDOW, 2048), lambda b, j: (b * nb + j, COL_QS // 2048)),
            pl.BlockSpec((WINDOW, 512), lambda b, j: (b * nb + j, kv_blk)),
            pl.BlockSpec((WINDOW, 512), lambda b, j: (b * nb + jnp.maximum(j - 1, 0), kv_blk)),
        ],
        out_specs=pl.BlockSpec((WINDOW, 2048), lambda b, j: (b * nb + j, 0)),
        out_shape=jax.ShapeDtypeStruct((T_PROMPT, 2048), BF16),
        compiler_params=_params(("parallel", "parallel")),
        name="swa_prompt",
    )(sink_l, bias, u, u, u)


SS_BB = 8
SS_ROWS = DEC_SEQ * SWA_GROUP


def _bf16_round(x):
    return x.astype(BF16).astype(F32)


def _swa_sample_kernel(sink_ref, bias1_ref, bias2_ref, q_ref, kvn_ref, ck_ref, cv_ref,
                       o_ref, cko_ref, cvo_ref):
    kv_w = N_SWA_KV_HEADS * SWA_HEAD_DIM
    hd = SWA_HEAD_DIM
    scale = hd ** -0.5
    for kvh in range(N_SWA_KV_HEADS):
        lo = kvh * hd
        sink = sink_ref[kvh]
        q = q_ref[:, kvh]
        ck = ck_ref[:, :, lo:lo + hd].astype(BF16)
        cv = cv_ref[:, :, lo:lo + hd].astype(BF16)
        kn = _bf16_round(kvn_ref[:, :, lo:lo + hd])
        vn = _bf16_round(kvn_ref[:, :, kv_w + lo:kv_w + lo + hd])
        s1 = lax.dot_general(q.astype(BF16), ck, (((2,), (2,)), ((0,), (0,))),
                             preferred_element_type=F32) * scale + bias1_ref[kvh]
        qr = _bf16_round(q)
        s2 = [jnp.sum(qr * kn[:, t2:t2 + 1, :], axis=-1, keepdims=True) * scale + bias2_ref[kvh, t2]
              for t2 in range(DEC_SEQ)]
        m = jnp.maximum(jnp.max(s1, axis=-1, keepdims=True), sink)
        for st in s2:
            m = jnp.maximum(m, st)
        e1 = jnp.exp(s1 - m)
        e2 = [jnp.exp(st - m) for st in s2]
        den = jnp.sum(e1, axis=-1, keepdims=True) + jnp.exp(sink - m)
        for et in e2:
            den = den + et
        inv = 1.0 / den
        o = lax.dot_general((e1 * inv).astype(BF16), cv, (((2,), (1,)), ((0,), (0,))),
                            preferred_element_type=F32)
        for t2 in range(DEC_SEQ):
            o = o + _bf16_round(e2[t2] * inv) * vn[:, t2:t2 + 1, :]
        o_ref[:, kvh] = o
    keep = WINDOW - DEC_SEQ
    cko_ref[:, 0:keep, :] = ck_ref[:, DEC_SEQ:WINDOW, :]
    cko_ref[:, keep:WINDOW, :] = kvn_ref[:, :, 0:kv_w]
    cvo_ref[:, 0:keep, :] = cv_ref[:, DEC_SEQ:WINDOW, :]
    cvo_ref[:, keep:WINDOW, :] = kvn_ref[:, :, kv_w:2 * kv_w]


def _swa_sample_tables(slopes, sinks):
    g_slopes = jnp.tile(slopes.reshape(N_SWA_KV_HEADS, 1, SWA_GROUP), (1, DEC_SEQ, 1)).reshape(
        N_SWA_KV_HEADS, SS_ROWS)
    sink_rows = jnp.tile(sinks.reshape(N_SWA_KV_HEADS, 1, SWA_GROUP), (1, DEC_SEQ, 1)).reshape(
        N_SWA_KV_HEADS, SS_ROWS, 1)
    t_row = jnp.arange(SS_ROWS) // SWA_GROUP
    jcol = jnp.arange(WINDOW)
    dist1 = (t_row[:, None] + WINDOW - jcol[None, :])
    bias1 = jnp.where((dist1 < WINDOW)[None], -(g_slopes[:, :, None] * dist1.astype(F32)[None]), NEG_INF)
    t2 = jnp.arange(DEC_SEQ)
    dist2 = t_row[None, :] - t2[:, None]
    bias2 = jnp.where((dist2 >= 0)[None], -(g_slopes[:, None, :] * dist2.astype(F32)[None]), NEG_INF)
    return sink_rows, bias1, bias2[..., None]


def swa_sample(q_t, kv_new, cache_k, cache_v, tables):
    sink_rows, bias1, bias2 = tables
    kv_w = N_SWA_KV_HEADS * SWA_HEAD_DIM
    whole3 = lambda i: (0, 0, 0)
    cache_spec = pl.BlockSpec((SS_BB, WINDOW, kv_w), lambda i: (i, 0, 0))
    return pl.pallas_call(
        _swa_sample_kernel,
        grid=(DEC_BATCH // SS_BB,),
        in_specs=[
            pl.BlockSpec((N_SWA_KV_HEADS, SS_ROWS, 1), whole3),
            pl.BlockSpec((N_SWA_KV_HEADS, SS_ROWS, WINDOW), whole3),
            pl.BlockSpec((N_SWA_KV_HEADS, DEC_SEQ, SS_ROWS, 1), lambda i: (0, 0, 0, 0)),
            pl.BlockSpec((SS_BB, N_SWA_KV_HEADS, SS_ROWS, SWA_HEAD_DIM), lambda i: (i, 0, 0, 0)),
            pl.BlockSpec((SS_BB, DEC_SEQ, 2 * kv_w), lambda i: (i, 0, 0)),
            cache_spec, cache_spec,
        ],
        out_specs=[
            pl.BlockSpec((SS_BB, N_SWA_KV_HEADS, SS_ROWS, SWA_HEAD_DIM), lambda i: (i, 0, 0, 0)),
            cache_spec, cache_spec,
        ],
        out_shape=[
            jax.ShapeDtypeStruct((DEC_BATCH, N_SWA_KV_HEADS, SS_ROWS, SWA_HEAD_DIM), F32),
            jax.ShapeDtypeStruct((DEC_BATCH, WINDOW, kv_w), F32),
            jax.ShapeDtypeStruct((DEC_BATCH, WINDOW, kv_w), F32),
        ],
        compiler_params=_params(("parallel",)),
        name="swa_sample",
    )(sink_rows, bias1, bias2, q_t, kv_new, cache_k, cache_v)


def _merge_kernel(or_ref, os_ref, gr_ref, gs_ref, wr_ref, ws_ref, o_ref):
    a = jnp.dot(or_ref[...], wr_ref[...], preferred_element_type=F32)
    b = jnp.dot(os_ref[...], ws_ref[...], preferred_element_type=F32)
    merged = jax.nn.sigmoid(gr_ref[...]) * a + jax.nn.sigmoid(gs_ref[...]) * b
    o_ref[...] = merged.astype(BF16)


def merge_branches(o_r, o_s, u, w_ret_o_bf, w_swa_o_bf, tm):
    m = o_r.shape[0]
    resident = pl.BlockSpec((2048, D_MODEL), lambda i: (0, 0), pipeline_mode=pl.Buffered(1))
    return pl.pallas_call(
        _merge_kernel,
        grid=(m // tm,),
        in_specs=[
            pl.BlockSpec((tm, 2048), lambda i: (i, 0)),
            pl.BlockSpec((tm, 2048), lambda i: (i, 0)),
            pl.BlockSpec((pl.Element(tm), pl.Element(D_MODEL)), lambda i: (i * tm, COL_GATE_R)),
            pl.BlockSpec((pl.Element(tm), pl.Element(D_MODEL)), lambda i: (i * tm, COL_GATE_S)),
            resident, resident,
        ],
        out_specs=pl.BlockSpec((tm, D_MODEL), lambda i: (i, 0)),
        out_shape=jax.ShapeDtypeStruct((m, D_MODEL), BF16),
        compiler_params=_params(("parallel",)),
        name="merge_branches",
    )(o_r, o_s, u, u, w_ret_o_bf, w_swa_o_bf)


ROUTER_PAD = 128


def _post_mix_kernel(alpha, n_prompt_blocks, mp_ref, ms_ref, xp_ref, xs_ref, w_ref, g_ref, b_ref, wr_ref, br_ref,
                     h_ref, lg_ref):
    from_sample = pl.program_id(0) >= n_prompt_blocks
    merged = jnp.where(from_sample, ms_ref[...], mp_ref[...])
    x = jnp.where(from_sample, xs_ref[...], xp_ref[...])
    mix = jnp.dot(merged, w_ref[...], preferred_element_type=F32)
    h = _layer_norm(alpha * x + mix, g_ref[...], b_ref[...])
    h_ref[...] = h
    h_hi = h.astype(BF16)
    h_lo = (h - h_hi.astype(F32)).astype(BF16)
    hh = jnp.dot(h_hi, wr_ref[...], preferred_element_type=F32)
    lh = jnp.dot(h_lo, wr_ref[:, :ROUTER_PAD], preferred_element_type=F32)
    lg_ref[...] = (hh[:, :ROUTER_PAD] + hh[:, ROUTER_PAD:] + lh + br_ref[...]).T


def _router_split(w_router, b_router):
    w = jnp.pad(w_router, ((0, 0), (0, ROUTER_PAD - N_EXPERTS)))
    w_hi = w.astype(BF16)
    w_lo = (w - w_hi.astype(F32)).astype(BF16)
    b = jnp.pad(b_router, (0, ROUTER_PAD - N_EXPERTS)).reshape(1, ROUTER_PAD)
    return jnp.concatenate([w_hi, w_lo], axis=1), b


def post_mix(merged_p, merged_s, xp, xs, w_out_bf, ln_g, ln_b, w_router_pad, b_router_pad, alpha, tm):
    n_p = T_PROMPT // tm
    n_s = T_SAMPLE // tm
    prompt_row = lambda i: (jnp.minimum(i, n_p - 1), 0)
    sample_row = lambda i: (jnp.clip(i - n_p, 0, n_s - 1), 0)
    whole = lambda i: (0, 0)
    row = lambda i: (i, 0)
    return pl.pallas_call(
        functools.partial(_post_mix_kernel, alpha, n_p),
        grid=(n_p + n_s,),
        in_specs=[
            pl.BlockSpec((tm, 2048), prompt_row),
            pl.BlockSpec((tm, 2048), sample_row),
            pl.BlockSpec((tm, 2048), prompt_row),
            pl.BlockSpec((tm, 2048), sample_row),
            pl.BlockSpec((2048, 2048), whole),
            pl.BlockSpec((1, 2048), whole),
            pl.BlockSpec((1, 2048), whole),
            pl.BlockSpec((2048, 2 * ROUTER_PAD), whole),
            pl.BlockSpec((1, ROUTER_PAD), whole),
        ],
        out_specs=[pl.BlockSpec((tm, 2048), row), pl.BlockSpec((ROUTER_PAD, tm), lambda i: (0, i))],
        out_shape=[jax.ShapeDtypeStruct((T_ALL, D_MODEL), F32),
                   jax.ShapeDtypeStruct((ROUTER_PAD, T_ALL), F32)],
        compiler_params=_params(("parallel",)),
        name="post_mix",
    )(merged_p, merged_s, xp, xs, w_out_bf, ln_g, ln_b, w_router_pad, b_router_pad)


def _moe_kernel(ce_ref, cs_ref, cn_ref, nu_ref, tail_ref, x_hbm, wg_ref, wu_ref, wd_ref, bg_ref, bu_ref, bd_ref,
                y_hbm, x_f32, x_bf, y_acc, zero_buf, x_sem, y_sem, z_sem):
    c = pl.program_id(0)
    j = pl.program_id(1)
    nblk = cn_ref[c]
    slot = c % 2
    y_cur = y_acc.at[slot]

    def sub_rows(r):
        return pl.ds(pl.multiple_of(r * MOE_SUB, MOE_SUB), MOE_SUB)

    def x_copy(chunk):
        start = pl.multiple_of(cs_ref[chunk] * MOE_SUB, MOE_SUB)
        return pltpu.make_async_copy(x_hbm.at[pl.ds(start, MOE_R), :], x_f32, x_sem)

    def y_copy(r, sub):
        return pltpu.make_async_copy(y_cur.at[sub_rows(r), :], y_hbm.at[sub_rows(sub), :], y_sem.at[slot])

    def zero_copy(sub):
        return pltpu.make_async_copy(zero_buf, y_hbm.at[sub_rows(sub), :], z_sem)

    @pl.when(j == 0)
    def _():
        def drain(i, carry):
            y_copy(0, 0).wait()
            return carry

        lax.fori_loop(0, cn_ref[jnp.maximum(c - 2, 0)] * (c >= 2).astype(jnp.int32), drain, 0)

    @pl.when((nblk > 0) & (j == 0))
    def _():
        @pl.when(c == 0)
        def _():
            x_copy(0).start()

        x_copy(c).wait()

        def stage(r, carry):
            rows = sub_rows(r)
            x_bf[rows, :] = x_f32[rows, :].astype(BF16)
            y_cur[rows, :] = jnp.broadcast_to(bd_ref[0], (MOE_SUB, D_MODEL))
            return carry

        lax.fori_loop(0, nblk, stage, 0)

        @pl.when(c + 1 < nu_ref[0])
        def _():
            x_copy(c + 1).start()

    @pl.when(nblk > 0)
    def _():
        def rows_block(rows):
            xr = x_bf[rows, :]
            g = jnp.dot(xr, wg_ref[0], preferred_element_type=F32) + bg_ref[0]
            up = jnp.dot(xr, wu_ref[0], preferred_element_type=F32) + bu_ref[0]
            g = jnp.minimum(g, SWIGLU_LIMIT)
            up = jnp.clip(up, -SWIGLU_LIMIT, SWIGLU_LIMIT)
            glu = g * jax.nn.sigmoid(SWIGLU_ALPHA * g)
            act = ((up + 1.0) * glu).astype(BF16)
            y_cur[rows, :] += jnp.dot(act, wd_ref[0], preferred_element_type=F32)

        for n_static in MOE_STATIC_SIZES:
            @pl.when(nblk == n_static)
            def _(n_static=n_static):
                rows_block(pl.ds(0, n_static * MOE_SUB))

        @pl.when(nblk < MOE_STATIC_SIZES[0])
        def _():
            def one(r, carry):
                rows_block(sub_rows(r))
                return carry

            lax.fori_loop(0, nblk, one, 0)

    @pl.when((nblk > 0) & (j == MOE_NJ - 1))
    def _():
        def write_back(r, carry):
            y_copy(r, cs_ref[c] + r).start()
            return carry

        lax.fori_loop(0, nblk, write_back, 0)

    @pl.when((nblk == 0) & (j == 0))
    def _():
        first = tail_ref[0] + MOE_CHUNK_SUBS * (c - nu_ref[0])
        count = jnp.clip(MOE_XROWS // MOE_SUB - first, 0, MOE_CHUNK_SUBS)

        @pl.when(count > 0)
        def _():
            zero_buf[...] = jnp.zeros_like(zero_buf)

            def start(i, carry):
                zero_copy(first + i).start()
                return carry

            def wait(i, carry):
                zero_copy(first + i).wait()
                return carry

            lax.fori_loop(0, count, start, 0)
            lax.fori_loop(0, count, wait, 0)


def moe_experts(x_sorted, chunk_tables, w_gate_up, b_gate_up, w_down, b_down):
    nj = MOE_NJ

    def used_j(c, j, nu):
        return jnp.where(c < nu[0], j, nj - 1)

    return pl.pallas_call(
        _moe_kernel,
        grid_spec=pltpu.PrefetchScalarGridSpec(
            num_scalar_prefetch=5,
            grid=(MOE_NC, nj),
            in_specs=[
                pl.BlockSpec(memory_space=pl.ANY),
                pl.BlockSpec((1, D_MODEL, MOE_TF),
                             lambda c, j, ce, cs, cn, nu, tl: (ce[c], 0, used_j(c, j, nu))),
                pl.BlockSpec((1, D_MODEL, MOE_TF),
                             lambda c, j, ce, cs, cn, nu, tl: (ce[c], 0, nj + used_j(c, j, nu))),
                pl.BlockSpec((1, MOE_TF, D_MODEL),
                             lambda c, j, ce, cs, cn, nu, tl: (ce[c], used_j(c, j, nu), 0)),
                pl.BlockSpec((1, 1, MOE_TF),
                             lambda c, j, ce, cs, cn, nu, tl: (ce[c], 0, used_j(c, j, nu))),
                pl.BlockSpec((1, 1, MOE_TF),
                             lambda c, j, ce, cs, cn, nu, tl: (ce[c], 0, nj + used_j(c, j, nu))),
                pl.BlockSpec((1, 1, D_MODEL), lambda c, j, ce, cs, cn, nu, tl: (ce[c], 0, 0)),
            ],
            out_specs=pl.BlockSpec(memory_space=pl.ANY),
            scratch_shapes=[
                pltpu.VMEM((MOE_R, D_MODEL), F32),
                pltpu.VMEM((MOE_R, D_MODEL), BF16),
                pltpu.VMEM((2, MOE_R, D_MODEL), F32),
                pltpu.VMEM((MOE_SUB, D_MODEL), F32),
                pltpu.SemaphoreType.DMA(()),
                pltpu.SemaphoreType.DMA((2,)),
                pltpu.SemaphoreType.DMA(()),
            ],
        ),
        out_shape=jax.ShapeDtypeStruct((MOE_XROWS, D_MODEL), F32),
        compiler_params=pltpu.CompilerParams(dimension_semantics=("arbitrary", "arbitrary"),
                                             vmem_limit_bytes=MOE_VMEM_LIMIT),
        name="moe_experts",
    )(*chunk_tables, x_sorted, w_gate_up, w_gate_up, w_down, b_gate_up, b_gate_up, b_down)


ROUTE_TM = 512


def _route_kernel(lg_ref, tri_ref, e_ref, rank_ref, gate_ref, cnt_ref, seen):
    @pl.when(pl.program_id(0) == 0)
    def _():
        seen[...] = jnp.zeros_like(seen)

    work = lg_ref[0:N_EXPERTS, :]
    eidx = lax.broadcasted_iota(jnp.int32, work.shape, 0)
    vals, idxs, hots = [], [], []
    for _ in range(TOP_K):
        m = jnp.max(work, axis=0, keepdims=True)
        idx = jnp.min(jnp.where(work == m, eidx, N_EXPERTS), axis=0, keepdims=True)
        hot = eidx == idx
        vals.append(m)
        idxs.append(idx)
        hots.append(hot)
        work = jnp.where(hot, -jnp.inf, work)
    ex = [jnp.exp(v - vals[0]) for v in vals]
    inv = 1.0 / (ex[0] + ex[1] + ex[2] + ex[3])
    picked = jnp.zeros(work.shape, F32)
    for hot in hots:
        picked = picked + hot.astype(F32)
    before = jnp.dot(picked.astype(BF16), tri_ref[...], preferred_element_type=F32) + seen[...]
    ranks = [jnp.sum(jnp.where(hot, before, 0.0), axis=0, keepdims=True) for hot in hots]
    seen[...] += jnp.sum(picked, axis=1, keepdims=True)
    e_ref[...] = jnp.concatenate(idxs, axis=0)
    rank_ref[...] = jnp.concatenate(ranks, axis=0).astype(jnp.int32)
    gate_ref[...] = jnp.concatenate([e * inv for e in ex], axis=0)
    cnt_ref[...] = seen[...]


def route_topk(logits_t):
    tm = ROUTE_TM
    tri = (jnp.arange(tm)[:, None] < jnp.arange(tm)[None, :]).astype(BF16)
    pick = pl.BlockSpec((TOP_K, tm), lambda i: (0, i))
    return pl.pallas_call(
        _route_kernel,
        grid=(T_ALL // tm,),
        in_specs=[pl.BlockSpec((ROUTER_PAD, tm), lambda i: (0, i)),
                  pl.BlockSpec((tm, tm), lambda i: (0, 0))],
        out_specs=[pick, pick, pick, pl.BlockSpec((N_EXPERTS, 1), lambda i: (0, 0))],
        out_shape=[jax.ShapeDtypeStruct((TOP_K, T_ALL), jnp.int32),
                   jax.ShapeDtypeStruct((TOP_K, T_ALL), jnp.int32),
                   jax.ShapeDtypeStruct((TOP_K, T_ALL), F32),
                   jax.ShapeDtypeStruct((N_EXPERTS, 1), F32)],
        scratch_shapes=[pltpu.VMEM((N_EXPERTS, 1), F32)],
        compiler_params=_params(("arbitrary",)),
        name="route_topk",
    )(logits_t, tri)


def route(logits_t):
    e_kt, rank, gate_kt, counts_f = route_topk(logits_t)
    counts = counts_f.reshape(N_EXPERTS).astype(jnp.int32)
    nsub = (counts + MOE_SUB - 1) // MOE_SUB
    sub_base = jnp.cumsum(nsub) - nsub
    nch = (counts + MOE_R - 1) // MOE_R
    ch_end = jnp.cumsum(nch)
    ch_base = ch_end - nch
    n_used = ch_end[-1]
    is_e = e_kt[..., None] == jnp.arange(N_EXPERTS, dtype=jnp.int32)
    pos = MOE_SUB * jnp.sum(jnp.where(is_e, sub_base, 0), axis=-1) + rank
    tok = jnp.broadcast_to(jnp.arange(T_ALL, dtype=jnp.int32)[None, :], pos.shape)
    row_tok = jnp.zeros((MOE_XROWS,), jnp.int32).at[pos.reshape(-1)].set(
        tok.reshape(-1), unique_indices=True, mode="promise_in_bounds")
    c = jnp.arange(MOE_NC, dtype=jnp.int32)
    cc = jnp.minimum(c, n_used - 1)
    ce = jnp.minimum(jnp.sum(cc[:, None] >= ch_end[None, :], axis=1), N_EXPERTS - 1).astype(jnp.int32)
    kk = cc - ch_base[ce]
    cs = (sub_base[ce] + MOE_CHUNK_SUBS * kk).astype(jnp.int32)
    cn = jnp.where(c < n_used, jnp.minimum(MOE_CHUNK_SUBS, nsub[ce] - MOE_CHUNK_SUBS * kk), 0).astype(jnp.int32)
    tables = (ce, cs, cn, n_used.reshape(1).astype(jnp.int32), jnp.sum(nsub).reshape(1).astype(jnp.int32))
    return gate_kt.T, row_tok, pos, tables


def _combine_kernel(alpha, h_ref, yg_ref, gate_ref, g_ref, b_ref, o_ref):
    gate = gate_ref[...]
    f = yg_ref[0] * gate[:, 0:1]
    for k in range(1, TOP_K):
        f = f + yg_ref[k] * gate[:, k:k + 1]
    o_ref[...] = _layer_norm(alpha * h_ref[...] + f, g_ref[...], b_ref[...])


def combine(h, yg, gate, ln_g, ln_b, alpha, row_off, m, tm):
    blk_off = row_off // tm
    whole = lambda i: (0, 0)
    return pl.pallas_call(
        functools.partial(_combine_kernel, alpha),
        grid=(m // tm,),
        in_specs=[
            pl.BlockSpec((tm, D_MODEL), lambda i: (blk_off + i, 0)),
            pl.BlockSpec((TOP_K, tm, D_MODEL), lambda i: (0, blk_off + i, 0)),
            pl.BlockSpec((tm, TOP_K), lambda i: (blk_off + i, 0)),
            pl.BlockSpec((1, D_MODEL), whole),
            pl.BlockSpec((1, D_MODEL), whole),
        ],
        out_specs=pl.BlockSpec((tm, D_MODEL), lambda i: (i, 0)),
        out_shape=jax.ShapeDtypeStruct((m, D_MODEL), F32),
        compiler_params=_params(("parallel",)),
        name="combine",
    )(h, yg, gate, ln_g, ln_b)


def _retention_tables(chunk, group):
    log_gamma = jnp.log(1.0 - jnp.exp2(-5.0 - jnp.arange(N_RET_HEADS, dtype=F32)))
    n = chunk * group
    pos = jnp.arange(n) % chunk
    seq = jnp.arange(n) // chunk
    diff = (pos[:, None] - pos[None, :]).astype(F32)
    same = seq[:, None] == seq[None, :]
    lg = log_gamma[:, None, None]
    dec = jnp.where(same[None] & (diff >= 0)[None], jnp.exp(lg * jnp.maximum(diff, 0.0)[None]), 0.0)
    posf = pos.astype(F32)
    qd = jnp.exp(log_gamma[:, None] * (posf + 1.0))[..., None]
    kd = jnp.exp(log_gamma[:, None] * (chunk - 1.0 - posf))[..., None]
    cd = jnp.exp(log_gamma * chunk)
    return cd, dec, qd, kd


def _alibi_slopes():
    h = jnp.arange(1, N_SWA_HEADS + 1, dtype=F32)
    return jnp.exp2(-8.0 * h / N_SWA_HEADS)


def kernel(x_prompt, x_sample, state_ret, cache_swa_k, cache_swa_v, w_in, w_ret_o, w_swa_o, w_out, swa_sinks,
           ln1_g, ln1_b, w_router, b_router, w_gate_up, b_gate_up, w_down, b_down, ln2_g, ln2_b):
    alpha = (2.0 * DEPTH) ** 0.25
    kv_w = N_SWA_KV_HEADS * SWA_HEAD_DIM
    w = w_in[0]
    w_in_bf = w.astype(BF16)
    w_ret_o_bf = w_ret_o[0].astype(BF16)
    w_swa_o_bf = w_swa_o[0].astype(BF16)
    w_out_bf = w_out[0].astype(BF16)
    w_router_pad, b_router_pad = _router_split(w_router[0], b_router[0])
    ln1g, ln1b = ln1_g[0].reshape(1, D_MODEL), ln1_b[0].reshape(1, D_MODEL)
    ln2g, ln2b = ln2_g[0].reshape(1, D_MODEL), ln2_b[0].reshape(1, D_MODEL)
    slopes = _alibi_slopes()
    sinks = swa_sinks[0].astype(F32)

    xp = x_prompt.reshape(T_PROMPT, D_MODEL)
    xs = x_sample.reshape(T_SAMPLE, D_MODEL)

    u_p = in_proj(xp, w_in_bf, 1024, 1280)
    o_r_p, state_p = retention_prompt(u_p, _retention_tables(RET_CHUNK, 1))
    o_s_p = swa_prompt(u_p, slopes, sinks)
    kv_p = u_p.reshape(BATCH, SEQ, D_IN)[:, SEQ - WINDOW:, COL_KVS:COL_KVS + 2 * kv_w]
    k_cache_p = kv_p[..., :kv_w].reshape(1, BATCH, WINDOW, N_SWA_KV_HEADS, SWA_HEAD_DIM)
    v_cache_p = kv_p[..., kv_w:].reshape(1, BATCH, WINDOW, N_SWA_KV_HEADS, SWA_HEAD_DIM)
    merged_p = merge_branches(o_r_p, o_s_p, u_p, w_ret_o_bf, w_swa_o_bf, 512)

    u_s = in_proj(xs, w_in_bf, 512, 1280)
    o_r_s, state_s = retention_sample(u_s, state_ret[0], _retention_tables(DEC_SEQ, RS_BB))
    q_t = (u_s[:, COL_QS:COL_QS + 2048]
           .reshape(DEC_BATCH, DEC_SEQ, N_SWA_KV_HEADS, SWA_GROUP, SWA_HEAD_DIM)
           .transpose(0, 2, 1, 3, 4).reshape(DEC_BATCH, N_SWA_KV_HEADS, SS_ROWS, SWA_HEAD_DIM))
    kv_new = u_s[:, COL_KVS:COL_KVS + 2 * kv_w].reshape(DEC_BATCH, DEC_SEQ, 2 * kv_w)
    o_s_t, k_cache_s, v_cache_s = swa_sample(
        q_t, kv_new, cache_swa_k[0].reshape(DEC_BATCH, WINDOW, kv_w),
        cache_swa_v[0].reshape(DEC_BATCH, WINDOW, kv_w), _swa_sample_tables(slopes, sinks))
    o_s_s = (o_s_t.reshape(DEC_BATCH, N_SWA_KV_HEADS, DEC_SEQ, SWA_GROUP, SWA_HEAD_DIM)
             .transpose(0, 2, 1, 3, 4).reshape(T_SAMPLE, 2048).astype(BF16))
    merged_s = merge_branches(o_r_s, o_s_s, u_s, w_ret_o_bf, w_swa_o_bf, 512)

    h, logits = post_mix(merged_p, merged_s, xp, xs, w_out_bf, ln1g, ln1b, w_router_pad, b_router_pad,
                         alpha, 512)

    gate, row_tok, pos, tables = route(logits)
    x_sorted = h.at[row_tok].get(mode="promise_in_bounds")
    y_rows = moe_experts(x_sorted, tables, w_gate_up[0], b_gate_up[0].reshape(N_EXPERTS, 1, 2 * D_FF),
                         w_down[0], b_down[0].reshape(N_EXPERTS, 1, D_MODEL))
    yg = y_rows.at[pos].get(mode="promise_in_bounds")
    y_p = combine(h, yg, gate, ln2g, ln2b, alpha, 0, T_PROMPT, 256)
    y_s = combine(h, yg, gate, ln2g, ln2b, alpha, T_PROMPT, T_SAMPLE, 256)

    return (y_p.reshape(BATCH, SEQ, D_MODEL), y_s.reshape(DEC_BATCH, DEC_SEQ, D_MODEL),
            state_p[None], k_cache_p, v_cache_p,
            state_s[None], k_cache_s.reshape(1, DEC_BATCH, WINDOW, N_SWA_KV_HEADS, SWA_HEAD_DIM),
            v_cache_s.reshape(1, DEC_BATCH, WINDOW, N_SWA_KV_HEADS, SWA_HEAD_DIM))
```

```python
import functools

import jax
import jax.numpy as jnp
from jax import lax
from jax.experimental import pallas as pl
from jax.experimental.pallas import tpu as pltpu

F32 = jnp.float32
BF16 = jnp.bfloat16

D_MODEL = 2048
BATCH = 4
SEQ = 2048
DEC_BATCH = 128
DEC_SEQ = 4
N_RET_HEADS = 8
RET_DK = 128
RET_DV = 256
RET_CHUNK = 128
N_SWA_HEADS = 32
N_SWA_KV_HEADS = 4
SWA_GROUP = 8
SWA_HEAD_DIM = 64
WINDOW = 128
N_EXPERTS = 32
TOP_K = 4
D_FF = 2048
SWIGLU_LIMIT = 7.0
SWIGLU_ALPHA = 1.702
LN_EPS = 1e-5
GN_EPS = 1e-6
NEG_INF = -1e30
DEPTH = 1

T_PROMPT = BATCH * SEQ
T_SAMPLE = DEC_BATCH * DEC_SEQ
T_ALL = T_PROMPT + T_SAMPLE
TK_ALL = T_ALL * TOP_K
D_IN = 12800

COL_QR, COL_KR, COL_VR, COL_GR, COL_QS, COL_KVS, COL_GATE_R, COL_GATE_S = (
    0, 1024, 2048, 4096, 6144, 8192, 8704, 10752)

MOE_SUB = 128
MOE_CHUNK_SUBS = 11
MOE_STATIC_SIZES = (7, 8, 9, 10, 11)
MOE_R = MOE_SUB * MOE_CHUNK_SUBS
MOE_NSUB_MAX = TK_ALL // MOE_SUB + N_EXPERTS
MOE_XROWS = (MOE_NSUB_MAX + MOE_CHUNK_SUBS - 1) * MOE_SUB
MOE_NC = TK_ALL // MOE_R + N_EXPERTS + 2
MOE_TF = 256
MOE_NJ = D_FF // MOE_TF
MOE_VMEM_LIMIT = 60 * 1024 * 1024

VMEM_LIMIT = 56 * 1024 * 1024


def _params(sem):
    return pltpu.CompilerParams(dimension_semantics=sem, vmem_limit_bytes=VMEM_LIMIT)


def _in_proj_kernel(x_ref, w_ref, o_ref):
    o_ref[...] = jnp.dot(x_ref[...].astype(BF16), w_ref[...], preferred_element_type=F32)


def in_proj(x2d, w_bf, tm, tn):
    m, k = x2d.shape
    n = w_bf.shape[1]
    return pl.pallas_call(
        _in_proj_kernel,
        grid=(m // tm, n // tn),
        in_specs=[pl.BlockSpec((tm, k), lambda i, j: (i, 0)),
                  pl.BlockSpec((k, tn), lambda i, j: (0, j))],
        out_specs=pl.BlockSpec((tm, tn), lambda i, j: (i, j)),
        out_shape=jax.ShapeDtypeStruct((m, n), F32),
        compiler_params=_params(("parallel", "parallel")),
        name="in_proj",
    )(x2d, w_bf)


def _group_norm_gate(o, g):
    mu = jnp.mean(o, axis=-1, keepdims=True)
    oc = o - mu
    var = jnp.mean(oc * oc, axis=-1, keepdims=True)
    return oc * lax.rsqrt(var + GN_EPS) * (g * jax.nn.sigmoid(g))


def _dot_nt(a, b):
    return lax.dot_general(a, b, (((1,), (1,)), ((), ())), preferred_element_type=F32)


def _layer_norm(x, g, b):
    mu = jnp.mean(x, axis=-1, keepdims=True)
    xc = x - mu
    var = jnp.mean(xc * xc, axis=-1, keepdims=True)
    return xc * lax.rsqrt(var + LN_EPS) * g + b


def _ret_prompt_kernel(cd_ref, dec_ref, qd_ref, kd_ref, q_ref, k_ref, v_ref, g_ref, o_ref, st_ref):
    c = pl.program_id(1)

    @pl.when(c == 0)
    def _():
        st_ref[...] = jnp.zeros_like(st_ref)

    for h in range(N_RET_HEADS):
        q = q_ref[:, h * RET_DK:(h + 1) * RET_DK]
        k = k_ref[:, h * RET_DK:(h + 1) * RET_DK] * (RET_DK ** -0.5)
        v = v_ref[:, h * RET_DV:(h + 1) * RET_DV].astype(BF16)
        state = st_ref[0, h]
        scores = _dot_nt(q.astype(BF16), k.astype(BF16)) * dec_ref[h]
        o = (jnp.dot(scores.astype(BF16), v, preferred_element_type=F32)
             + jnp.dot((q * qd_ref[h]).astype(BF16), state.astype(BF16), preferred_element_type=F32))
        kt = (k * kd_ref[h]).T.astype(BF16)
        st_ref[0, h] = cd_ref[h] * state + jnp.dot(kt, v, preferred_element_type=F32)
        g = g_ref[:, h * RET_DV:(h + 1) * RET_DV]
        o_ref[:, h * RET_DV:(h + 1) * RET_DV] = _group_norm_gate(o, g).astype(BF16)


def retention_prompt(u, tables):
    cd, dec, qd, kd = tables
    nc = SEQ // RET_CHUNK
    row = lambda b, c: b * nc + c
    whole3 = lambda b, c: (0, 0, 0)
    return pl.pallas_call(
        _ret_prompt_kernel,
        grid=(BATCH, nc),
        in_specs=[
            pl.BlockSpec(memory_space=pltpu.SMEM),
            pl.BlockSpec((N_RET_HEADS, RET_CHUNK, RET_CHUNK), whole3),
            pl.BlockSpec((N_RET_HEADS, RET_CHUNK, 1), whole3),
            pl.BlockSpec((N_RET_HEADS, RET_CHUNK, 1), whole3),
            pl.BlockSpec((RET_CHUNK, 1024), lambda b, c: (row(b, c), COL_QR // 1024)),
            pl.BlockSpec((RET_CHUNK, 1024), lambda b, c: (row(b, c), COL_KR // 1024)),
            pl.BlockSpec((RET_CHUNK, 2048), lambda b, c: (row(b, c), COL_VR // 2048)),
            pl.BlockSpec((RET_CHUNK, 2048), lambda b, c: (row(b, c), COL_GR // 2048)),
        ],
        out_specs=[
            pl.BlockSpec((RET_CHUNK, 2048), lambda b, c: (row(b, c), 0)),
            pl.BlockSpec((1, N_RET_HEADS, RET_DK, RET_DV), lambda b, c: (b, 0, 0, 0)),
        ],
        out_shape=[
            jax.ShapeDtypeStruct((T_PROMPT, 2048), BF16),
            jax.ShapeDtypeStruct((BATCH, N_RET_HEADS, RET_DK, RET_DV), F32),
        ],
        compiler_params=_params(("parallel", "arbitrary")),
        name="retention_prompt",
    )(cd, dec, qd, kd, u, u, u, u)


RS_BB = 32


def _ret_sample_kernel(cd_ref, dec_ref, qd_ref, kd_ref, q_ref, k_ref, v_ref, g_ref, st_ref,
                       o_ref, sto_ref):
    h = pl.program_id(1)
    q = q_ref[...]
    k = k_ref[...] * (RET_DK ** -0.5)
    v = v_ref[...].astype(BF16)
    scores = _dot_nt(q.astype(BF16), k.astype(BF16)) * dec_ref[0]
    o_intra = jnp.dot(scores.astype(BF16), v, preferred_element_type=F32)
    qq = (q * qd_ref[0]).astype(BF16)
    kt = (k * kd_ref[0]).T
    col_batch = lax.broadcasted_iota(jnp.int32, kt.shape, 1) // DEC_SEQ
    first_of_pair = lax.broadcasted_iota(jnp.int32, (8, RET_DV), 0) < DEC_SEQ
    cd = cd_ref[h]
    pieces = []
    for p in range(RS_BB // 2):
        q8 = qq[8 * p:8 * p + 8]
        s0 = st_ref[2 * p].astype(BF16)
        s1 = st_ref[2 * p + 1].astype(BF16)
        r0 = jnp.dot(q8, s0, preferred_element_type=F32)
        r1 = jnp.dot(q8, s1, preferred_element_type=F32)
        pieces.append(jnp.where(first_of_pair, r0, r1))
    for b in range(RS_BB):
        ktm = jnp.where(col_batch == b, kt, 0.0).astype(BF16)
        sto_ref[b] = cd * st_ref[b] + jnp.dot(ktm, v, preferred_element_type=F32)
    o = o_intra + jnp.concatenate(pieces, axis=0)
    o_ref[...] = _group_norm_gate(o, g_ref[...]).astype(BF16)


def retention_sample(u_s, state, tables):
    cd, dec, qd, kd = tables
    rows = RS_BB * DEC_SEQ
    per_head = lambda i, h: (h, 0, 0)
    st_spec = pl.BlockSpec((RS_BB, None, RET_DK, RET_DV), lambda i, h: (i, h, 0, 0))
    return pl.pallas_call(
        _ret_sample_kernel,
        grid=(DEC_BATCH // RS_BB, N_RET_HEADS),
        in_specs=[
            pl.BlockSpec(memory_space=pltpu.SMEM),
            pl.BlockSpec((1, rows, rows), per_head),
            pl.BlockSpec((1, rows, 1), per_head),
            pl.BlockSpec((1, rows, 1), per_head),
            pl.BlockSpec((rows, RET_DK), lambda i, h: (i, COL_QR // RET_DK + h)),
            pl.BlockSpec((rows, RET_DK), lambda i, h: (i, COL_KR // RET_DK + h)),
            pl.BlockSpec((rows, RET_DV), lambda i, h: (i, COL_VR // RET_DV + h)),
            pl.BlockSpec((rows, RET_DV), lambda i, h: (i, COL_GR // RET_DV + h)),
            st_spec,
        ],
        out_specs=[pl.BlockSpec((rows, RET_DV), lambda i, h: (i, h)), st_spec],
        out_shape=[
            jax.ShapeDtypeStruct((T_SAMPLE, 2048), BF16),
            jax.ShapeDtypeStruct((DEC_BATCH, N_RET_HEADS, RET_DK, RET_DV), F32),
        ],
        compiler_params=_params(("parallel", "parallel")),
        name="retention_sample",
    )(cd, dec, qd, kd, u_s, u_s, u_s, u_s, state)


def _swa_prompt_kernel(sink_ref, bias_ref, q_ref, kvc_ref, kvp_ref, o_ref):
    kv_w = N_SWA_KV_HEADS * SWA_HEAD_DIM
    hd = SWA_HEAD_DIM
    qt = q_ref[...].T
    outs = []
    for kvh in range(N_SWA_KV_HEADS):
        lo = kvh * hd
        k_cat = jnp.concatenate([kvp_ref[:, lo:lo + hd], kvc_ref[:, lo:lo + hd]], axis=0).astype(BF16)
        v_cat = jnp.concatenate([kvp_ref[:, kv_w + lo:kv_w + lo + hd],
                                 kvc_ref[:, kv_w + lo:kv_w + lo + hd]], axis=0)
        vt = v_cat.T.astype(BF16)
        heads = range(kvh * SWA_GROUP, (kvh + 1) * SWA_GROUP)
        q8 = jnp.concatenate([qt[h * hd:(h + 1) * hd, :] for h in heads], axis=1)
        q8 = (q8 * (hd ** -0.5)).astype(BF16)
        s = jnp.dot(k_cat, q8, preferred_element_type=F32) + bias_ref[0, kvh]
        sink = sink_ref[kvh]
        m = jnp.maximum(jnp.max(s, axis=0, keepdims=True), sink)
        e = jnp.exp(s - m)
        den = jnp.sum(e, axis=0, keepdims=True) + jnp.exp(sink - m)
        p = (e * (1.0 / den)).astype(BF16)
        ot = jnp.dot(vt, p, preferred_element_type=F32)
        outs += [ot[:, g * WINDOW:(g + 1) * WINDOW] for g in range(SWA_GROUP)]
    o_ref[...] = jnp.concatenate(outs, axis=0).T.astype(BF16)


def _swa_prompt_tables(slopes, sinks):
    qi = jnp.arange(WINDOW)[:, None] + WINDOW
    ki = jnp.arange(2 * WINDOW)[None, :]
    dist = qi - ki
    in_window = (dist >= 0) & (dist < WINDOW)
    valid = jnp.stack([in_window & (ki >= WINDOW), in_window])
    pen = -(slopes[:, None, None] * dist.astype(F32)[None])
    bias = jnp.where(valid[:, None], pen[None], NEG_INF)
    bias = bias.reshape(2, N_SWA_KV_HEADS, SWA_GROUP, WINDOW, 2 * WINDOW).transpose(0, 1, 4, 2, 3)
    bias = bias.reshape(2, N_SWA_KV_HEADS, 2 * WINDOW, SWA_GROUP * WINDOW)
    sink_l = jnp.repeat(sinks.reshape(N_SWA_KV_HEADS, SWA_GROUP), WINDOW, axis=1)
    return sink_l.reshape(N_SWA_KV_HEADS, 1, SWA_GROUP * WINDOW), bias


def swa_prompt(u, slopes, sinks):
    nb = SEQ // WINDOW
    kv_blk = COL_KVS // 512
    sink_l, bias = _swa_prompt_tables(slopes, sinks)
    lanes = SWA_GROUP * WINDOW
    return pl.pallas_call(
        _swa_prompt_kernel,
        grid=(BATCH, nb),
        in_specs=[
            pl.BlockSpec((N_SWA_KV_HEADS, 1, lanes), lambda b, j: (0, 0, 0)),
            pl.BlockSpec((1, N_SWA_KV_HEADS, 2 * WINDOW, lanes), lambda b, j: (jnp.minimum(j, 1), 0, 0, 0)),
            pl.BlockSpec((WINDOW, 2048), lambda b, j: (b * nb + j, COL_QS // 2048)),
            pl.BlockSpec((WINDOW, 512), lambda b, j: (b * nb + j, kv_blk)),
            pl.BlockSpec((WINDOW, 512), lambda b, j: (b * nb + jnp.maximum(j - 1, 0), kv_blk)),
        ],
        out_specs=pl.BlockSpec((WINDOW, 2048), lambda b, j: (b * nb + j, 0)),
        out_shape=jax.ShapeDtypeStruct((T_PROMPT, 2048), BF16),
        compiler_params=_params(("parallel", "parallel")),
        name="swa_prompt",
    )(sink_l, bias, u, u, u)


SS_BB = 8
SS_ROWS = DEC_SEQ * SWA_GROUP


def _bf16_round(x):
    return x.astype(BF16).astype(F32)


def _swa_sample_kernel(sink_ref, bias1_ref, bias2_ref, q_ref, kvn_ref, ck_ref, cv_ref,
                       o_ref, cko_ref, cvo_ref):
    kv_w = N_SWA_KV_HEADS * SWA_HEAD_DIM
    hd = SWA_HEAD_DIM
    scale = hd ** -0.5
    for kvh in range(N_SWA_KV_HEADS):
        lo = kvh * hd
        sink = sink_ref[kvh]
        q = q_ref[:, kvh]
        ck = ck_ref[:, :, lo:lo + hd].astype(BF16)
        cv = cv_ref[:, :, lo:lo + hd].astype(BF16)
        kn = _bf16_round(kvn_ref[:, :, lo:lo + hd])
        vn = _bf16_round(kvn_ref[:, :, kv_w + lo:kv_w + lo + hd])
        s1 = lax.dot_general(q.astype(BF16), ck, (((2,), (2,)), ((0,), (0,))),
                             preferred_element_type=F32) * scale + bias1_ref[kvh]
        qr = _bf16_round(q)
        s2 = [jnp.sum(qr * kn[:, t2:t2 + 1, :], axis=-1, keepdims=True) * scale + bias2_ref[kvh, t2]
              for t2 in range(DEC_SEQ)]
        m = jnp.maximum(jnp.max(s1, axis=-1, keepdims=True), sink)
        for st in s2:
            m = jnp.maximum(m, st)
        e1 = jnp.exp(s1 - m)
        e2 = [jnp.exp(st - m) for st in s2]
        den = jnp.sum(e1, axis=-1, keepdims=True) + jnp.exp(sink - m)
        for et in e2:
            den = den + et
        inv = 1.0 / den
        o = lax.dot_general((e1 * inv).astype(BF16), cv, (((2,), (1,)), ((0,), (0,))),
                            preferred_element_type=F32)
        for t2 in range(DEC_SEQ):
            o = o + _bf16_round(e2[t2] * inv) * vn[:, t2:t2 + 1, :]
        o_ref[:, kvh] = o
    keep = WINDOW - DEC_SEQ
    cko_ref[:, 0:keep, :] = ck_ref[:, DEC_SEQ:WINDOW, :]
    cko_ref[:, keep:WINDOW, :] = kvn_ref[:, :, 0:kv_w]
    cvo_ref[:, 0:keep, :] = cv_ref[:, DEC_SEQ:WINDOW, :]
    cvo_ref[:, keep:WINDOW, :] = kvn_ref[:, :, kv_w:2 * kv_w]


def _swa_sample_tables(slopes, sinks):
    g_slopes = jnp.tile(slopes.reshape(N_SWA_KV_HEADS, 1, SWA_GROUP), (1, DEC_SEQ, 1)).reshape(
        N_SWA_KV_HEADS, SS_ROWS)
    sink_rows = jnp.tile(sinks.reshape(N_SWA_KV_HEADS, 1, SWA_GROUP), (1, DEC_SEQ, 1)).reshape(
        N_SWA_KV_HEADS, SS_ROWS, 1)
    t_row = jnp.arange(SS_ROWS) // SWA_GROUP
    jcol = jnp.arange(WINDOW)
    dist1 = (t_row[:, None] + WINDOW - jcol[None, :])
    bias1 = jnp.where((dist1 < WINDOW)[None], -(g_slopes[:, :, None] * dist1.astype(F32)[None]), NEG_INF)
    t2 = jnp.arange(DEC_SEQ)
    dist2 = t_row[None, :] - t2[:, None]
    bias2 = jnp.where((dist2 >= 0)[None], -(g_slopes[:, None, :] * dist2.astype(F32)[None]), NEG_INF)
    return sink_rows, bias1, bias2[..., None]


def swa_sample(q_t, kv_new, cache_k, cache_v, tables):
    sink_rows, bias1, bias2 = tables
    kv_w = N_SWA_KV_HEADS * SWA_HEAD_DIM
    whole3 = lambda i: (0, 0, 0)
    cache_spec = pl.BlockSpec((SS_BB, WINDOW, kv_w), lambda i: (i, 0, 0))
    return pl.pallas_call(
        _swa_sample_kernel,
        grid=(DEC_BATCH // SS_BB,),
        in_specs=[
            pl.BlockSpec((N_SWA_KV_HEADS, SS_ROWS, 1), whole3),
            pl.BlockSpec((N_SWA_KV_HEADS, SS_ROWS, WINDOW), whole3),
            pl.BlockSpec((N_SWA_KV_HEADS, DEC_SEQ, SS_ROWS, 1), lambda i: (0, 0, 0, 0)),
            pl.BlockSpec((SS_BB, N_SWA_KV_HEADS, SS_ROWS, SWA_HEAD_DIM), lambda i: (i, 0, 0, 0)),
            pl.BlockSpec((SS_BB, DEC_SEQ, 2 * kv_w), lambda i: (i, 0, 0)),
            cache_spec, cache_spec,
        ],
        out_specs=[
            pl.BlockSpec((SS_BB, N_SWA_KV_HEADS, SS_ROWS, SWA_HEAD_DIM), lambda i: (i, 0, 0, 0)),
            cache_spec, cache_spec,
        ],
        out_shape=[
            jax.ShapeDtypeStruct((DEC_BATCH, N_SWA_KV_HEADS, SS_ROWS, SWA_HEAD_DIM), F32),
            jax.ShapeDtypeStruct((DEC_BATCH, WINDOW, kv_w), F32),
            jax.ShapeDtypeStruct((DEC_BATCH, WINDOW, kv_w), F32),
        ],
        compiler_params=_params(("parallel",)),
        name="swa_sample",
    )(sink_rows, bias1, bias2, q_t, kv_new, cache_k, cache_v)


def _merge_kernel(or_ref, os_ref, gr_ref, gs_ref, wr_ref, ws_ref, o_ref):
    a = jnp.dot(or_ref[...], wr_ref[...], preferred_element_type=F32)
    b = jnp.dot(os_ref[...], ws_ref[...], preferred_element_type=F32)
    merged = jax.nn.sigmoid(gr_ref[...]) * a + jax.nn.sigmoid(gs_ref[...]) * b
    o_ref[...] = merged.astype(BF16)


def merge_branches(o_r, o_s, u, w_ret_o_bf, w_swa_o_bf, tm):
    m = o_r.shape[0]
    resident = pl.BlockSpec((2048, D_MODEL), lambda i: (0, 0), pipeline_mode=pl.Buffered(1))
    return pl.pallas_call(
        _merge_kernel,
        grid=(m // tm,),
        in_specs=[
            pl.BlockSpec((tm, 2048), lambda i: (i, 0)),
            pl.BlockSpec((tm, 2048), lambda i: (i, 0)),
            pl.BlockSpec((pl.Element(tm), pl.Element(D_MODEL)), lambda i: (i * tm, COL_GATE_R)),
            pl.BlockSpec((pl.Element(tm), pl.Element(D_MODEL)), lambda i: (i * tm, COL_GATE_S)),
            resident, resident,
        ],
        out_specs=pl.BlockSpec((tm, D_MODEL), lambda i: (i, 0)),
        out_shape=jax.ShapeDtypeStruct((m, D_MODEL), BF16),
        compiler_params=_params(("parallel",)),
        name="merge_branches",
    )(o_r, o_s, u, u, w_ret_o_bf, w_swa_o_bf)


ROUTER_PAD = 128


def _post_mix_kernel(alpha, n_prompt_blocks, mp_ref, ms_ref, xp_ref, xs_ref, w_ref, g_ref, b_ref, wr_ref, br_ref,
                     h_ref, lg_ref):
    from_sample = pl.program_id(0) >= n_prompt_blocks
    merged = jnp.where(from_sample, ms_ref[...], mp_ref[...])
    x = jnp.where(from_sample, xs_ref[...], xp_ref[...])
    mix = jnp.dot(merged, w_ref[...], preferred_element_type=F32)
    h = _layer_norm(alpha * x + mix, g_ref[...], b_ref[...])
    h_ref[...] = h
    h_hi = h.astype(BF16)
    h_lo = (h - h_hi.astype(F32)).astype(BF16)
    hh = jnp.dot(h_hi, wr_ref[...], preferred_element_type=F32)
    lh = jnp.dot(h_lo, wr_ref[:, :ROUTER_PAD], preferred_element_type=F32)
    lg_ref[...] = (hh[:, :ROUTER_PAD] + hh[:, ROUTER_PAD:] + lh + br_ref[...]).T


def _router_split(w_router, b_router):
    w = jnp.pad(w_router, ((0, 0), (0, ROUTER_PAD - N_EXPERTS)))
    w_hi = w.astype(BF16)
    w_lo = (w - w_hi.astype(F32)).astype(BF16)
    b = jnp.pad(b_router, (0, ROUTER_PAD - N_EXPERTS)).reshape(1, ROUTER_PAD)
    return jnp.concatenate([w_hi, w_lo], axis=1), b


def post_mix(merged_p, merged_s, xp, xs, w_out_bf, ln_g, ln_b, w_router_pad, b_router_pad, alpha, tm):
    n_p = T_PROMPT // tm
    n_s = T_SAMPLE // tm
    prompt_row = lambda i: (jnp.minimum(i, n_p - 1), 0)
    sample_row = lambda i: (jnp.clip(i - n_p, 0, n_s - 1), 0)
    whole = lambda i: (0, 0)
    row = lambda i: (i, 0)
    return pl.pallas_call(
        functools.partial(_post_mix_kernel, alpha, n_p),
        grid=(n_p + n_s,),
        in_specs=[
            pl.BlockSpec((tm, 2048), prompt_row),
            pl.BlockSpec((tm, 2048), sample_row),
            pl.BlockSpec((tm, 2048), prompt_row),
            pl.BlockSpec((tm, 2048), sample_row),
            pl.BlockSpec((2048, 2048), whole),
            pl.BlockSpec((1, 2048), whole),
            pl.BlockSpec((1, 2048), whole),
            pl.BlockSpec((2048, 2 * ROUTER_PAD), whole),
            pl.BlockSpec((1, ROUTER_PAD), whole),
        ],
        out_specs=[pl.BlockSpec((tm, 2048), row), pl.BlockSpec((ROUTER_PAD, tm), lambda i: (0, i))],
        out_shape=[jax.ShapeDtypeStruct((T_ALL, D_MODEL), F32),
                   jax.ShapeDtypeStruct((ROUTER_PAD, T_ALL), F32)],
        compiler_params=_params(("parallel",)),
        name="post_mix",
    )(merged_p, merged_s, xp, xs, w_out_bf, ln_g, ln_b, w_router_pad, b_router_pad)


def _moe_kernel(ce_ref, cs_ref, cn_ref, nu_ref, tail_ref, x_hbm, wg_ref, wu_ref, wd_ref, bg_ref, bu_ref, bd_ref,
                y_hbm, x_f32, x_bf, y_acc, zero_buf, x_sem, y_sem, z_sem):
    c = pl.program_id(0)
    j = pl.program_id(1)
    nblk = cn_ref[c]
    slot = c % 2
    y_cur = y_acc.at[slot]

    def sub_rows(r):
        return pl.ds(pl.multiple_of(r * MOE_SUB, MOE_SUB), MOE_SUB)

    def x_copy(chunk):
        start = pl.multiple_of(cs_ref[chunk] * MOE_SUB, MOE_SUB)
        return pltpu.make_async_copy(x_hbm.at[pl.ds(start, MOE_R), :], x_f32, x_sem)

    def y_copy(r, sub):
        return pltpu.make_async_copy(y_cur.at[sub_rows(r), :], y_hbm.at[sub_rows(sub), :], y_sem.at[slot])

    def zero_copy(sub):
        return pltpu.make_async_copy(zero_buf, y_hbm.at[sub_rows(sub), :], z_sem)

    @pl.when(j == 0)
    def _():
        def drain(i, carry):
            y_copy(0, 0).wait()
            return carry

        lax.fori_loop(0, cn_ref[jnp.maximum(c - 2, 0)] * (c >= 2).astype(jnp.int32), drain, 0)

    @pl.when((nblk > 0) & (j == 0))
    def _():
        @pl.when(c == 0)
        def _():
            x_copy(0).start()

        x_copy(c).wait()

        def stage(r, carry):
            rows = sub_rows(r)
            x_bf[rows, :] = x_f32[rows, :].astype(BF16)
            y_cur[rows, :] = jnp.broadcast_to(bd_ref[0], (MOE_SUB, D_MODEL))
            return carry

        lax.fori_loop(0, nblk, stage, 0)

        @pl.when(c + 1 < nu_ref[0])
        def _():
            x_copy(c + 1).start()

    @pl.when(nblk > 0)
    def _():
        def rows_block(rows):
            xr = x_bf[rows, :]
            g = jnp.dot(xr, wg_ref[0], preferred_element_type=F32) + bg_ref[0]
            up = jnp.dot(xr, wu_ref[0], preferred_element_type=F32) + bu_ref[0]
            g = jnp.minimum(g, SWIGLU_LIMIT)
            up = jnp.clip(up, -SWIGLU_LIMIT, SWIGLU_LIMIT)
            glu = g * jax.nn.sigmoid(SWIGLU_ALPHA * g)
            act = ((up + 1.0) * glu).astype(BF16)
            y_cur[rows, :] += jnp.dot(act, wd_ref[0], preferred_element_type=F32)

        for n_static in MOE_STATIC_SIZES:
            @pl.when(nblk == n_static)
            def _(n_static=n_static):
                rows_block(pl.ds(0, n_static * MOE_SUB))

        @pl.when(nblk < MOE_STATIC_SIZES[0])
        def _():
            def one(r, carry):
                rows_block(sub_rows(r))
                return carry

            lax.fori_loop(0, nblk, one, 0)

    @pl.when((nblk > 0) & (j == MOE_NJ - 1))
    def _():
        def write_back(r, carry):
            y_copy(r, cs_ref[c] + r).start()
            return carry

        lax.fori_loop(0, nblk, write_back, 0)

    @pl.when((nblk == 0) & (j == 0))
    def _():
        first = tail_ref[0] + MOE_CHUNK_SUBS * (c - nu_ref[0])
        count = jnp.clip(MOE_XROWS // MOE_SUB - first, 0, MOE_CHUNK_SUBS)

        @pl.when(count > 0)
        def _():
            zero_buf[...] = jnp.zeros_like(zero_buf)

            def start(i, carry):
                zero_copy(first + i).start()
                return carry

            def wait(i, carry):
                zero_copy(first + i).wait()
                return carry

            lax.fori_loop(0, count, start, 0)
            lax.fori_loop(0, count, wait, 0)


def moe_experts(x_sorted, chunk_tables, w_gate_up, b_gate_up, w_down, b_down):
    nj = MOE_NJ

    def used_j(c, j, nu):
        return jnp.where(c < nu[0], j, nj - 1)

    return pl.pallas_call(
        _moe_kernel,
        grid_spec=pltpu.PrefetchScalarGridSpec(
            num_scalar_prefetch=5,
            grid=(MOE_NC, nj),
            in_specs=[
                pl.BlockSpec(memory_space=pl.ANY),
                pl.BlockSpec((1, D_MODEL, MOE_TF),
                             lambda c, j, ce, cs, cn, nu, tl: (ce[c], 0, used_j(c, j, nu))),
                pl.BlockSpec((1, D_MODEL, MOE_TF),
                             lambda c, j, ce, cs, cn, nu, tl: (ce[c], 0, nj + used_j(c, j, nu))),
                pl.BlockSpec((1, MOE_TF, D_MODEL),
                             lambda c, j, ce, cs, cn, nu, tl: (ce[c], used_j(c, j, nu), 0)),
                pl.BlockSpec((1, 1, MOE_TF),
                             lambda c, j, ce, cs, cn, nu, tl: (ce[c], 0, used_j(c, j, nu))),
                pl.BlockSpec((1, 1, MOE_TF),
                             lambda c, j, ce, cs, cn, nu, tl: (ce[c], 0, nj + used_j(c, j, nu))),
                pl.BlockSpec((1, 1, D_MODEL), lambda c, j, ce, cs, cn, nu, tl: (ce[c], 0, 0)),
            ],
            out_specs=pl.BlockSpec(memory_space=pl.ANY),
            scratch_shapes=[
                pltpu.VMEM((MOE_R, D_MODEL), F32),
                pltpu.VMEM((MOE_R, D_MODEL), BF16),
                pltpu.VMEM((2, MOE_R, D_MODEL), F32),
                pltpu.VMEM((MOE_SUB, D_MODEL), F32),
                pltpu.SemaphoreType.DMA(()),
                pltpu.SemaphoreType.DMA((2,)),
                pltpu.SemaphoreType.DMA(()),
            ],
        ),
        out_shape=jax.ShapeDtypeStruct((MOE_XROWS, D_MODEL), F32),
        compiler_params=pltpu.CompilerParams(dimension_semantics=("arbitrary", "arbitrary"),
                                             vmem_limit_bytes=MOE_VMEM_LIMIT),
        name="moe_experts",
    )(*chunk_tables, x_sorted, w_gate_up, w_gate_up, w_down, b_gate_up, b_gate_up, b_down)


ROUTE_TM = 512


def _route_kernel(lg_ref, tri_ref, e_ref, rank_ref, gate_ref, cnt_ref, seen):
    @pl.when(pl.program_id(0) == 0)
    def _():
        seen[...] = jnp.zeros_like(seen)

    work = lg_ref[0:N_EXPERTS, :]
    eidx = lax.broadcasted_iota(jnp.int32, work.shape, 0)
    vals, idxs, hots = [], [], []
    for _ in range(TOP_K):
        m = jnp.max(work, axis=0, keepdims=True)
        idx = jnp.min(jnp.where(work == m, eidx, N_EXPERTS), axis=0, keepdims=True)
        hot = eidx == idx
        vals.append(m)
        idxs.append(idx)
        hots.append(hot)
        work = jnp.where(hot, -jnp.inf, work)
    ex = [jnp.exp(v - vals[0]) for v in vals]
    inv = 1.0 / (ex[0] + ex[1] + ex[2] + ex[3])
    picked = jnp.zeros(work.shape, F32)
    for hot in hots:
        picked = picked + hot.astype(F32)
    before = jnp.dot(picked.astype(BF16), tri_ref[...], preferred_element_type=F32) + seen[...]
    ranks = [jnp.sum(jnp.where(hot, before, 0.0), axis=0, keepdims=True) for hot in hots]
    seen[...] += jnp.sum(picked, axis=1, keepdims=True)
    e_ref[...] = jnp.concatenate(idxs, axis=0)
    rank_ref[...] = jnp.concatenate(ranks, axis=0).astype(jnp.int32)
    gate_ref[...] = jnp.concatenate([e * inv for e in ex], axis=0)
    cnt_ref[...] = seen[...]


def route_topk(logits_t):
    tm = ROUTE_TM
    tri = (jnp.arange(tm)[:, None] < jnp.arange(tm)[None, :]).astype(BF16)
    pick = pl.BlockSpec((TOP_K, tm), lambda i: (0, i))
    return pl.pallas_call(
        _route_kernel,
        grid=(T_ALL // tm,),
        in_specs=[pl.BlockSpec((ROUTER_PAD, tm), lambda i: (0, i)),
                  pl.BlockSpec((tm, tm), lambda i: (0, 0))],
        out_specs=[pick, pick, pick, pl.BlockSpec((N_EXPERTS, 1), lambda i: (0, 0))],
        out_shape=[jax.ShapeDtypeStruct((TOP_K, T_ALL), jnp.int32),
                   jax.ShapeDtypeStruct((TOP_K, T_ALL), jnp.int32),
                   jax.ShapeDtypeStruct((TOP_K, T_ALL), F32),
                   jax.ShapeDtypeStruct((N_EXPERTS, 1), F32)],
        scratch_shapes=[pltpu.VMEM((N_EXPERTS, 1), F32)],
        compiler_params=_params(("arbitrary",)),
        name="route_topk",
    )(logits_t, tri)


def route(logits_t):
    e_kt, rank, gate_kt, counts_f = route_topk(logits_t)
    counts = counts_f.reshape(N_EXPERTS).astype(jnp.int32)
    nsub = (counts + MOE_SUB - 1) // MOE_SUB
    sub_base = jnp.cumsum(nsub) - nsub
    nch = (counts + MOE_R - 1) // MOE_R
    ch_end = jnp.cumsum(nch)
    ch_base = ch_end - nch
    n_used = ch_end[-1]
    is_e = e_kt[..., None] == jnp.arange(N_EXPERTS, dtype=jnp.int32)
    pos = MOE_SUB * jnp.sum(jnp.where(is_e, sub_base, 0), axis=-1) + rank
    tok = jnp.broadcast_to(jnp.arange(T_ALL, dtype=jnp.int32)[None, :], pos.shape)
    row_tok = jnp.zeros((MOE_XROWS,), jnp.int32).at[pos.reshape(-1)].set(
        tok.reshape(-1), unique_indices=True, mode="promise_in_bounds")
    c = jnp.arange(MOE_NC, dtype=jnp.int32)
    cc = jnp.minimum(c, n_used - 1)
    ce = jnp.minimum(jnp.sum(cc[:, None] >= ch_end[None, :], axis=1), N_EXPERTS - 1).astype(jnp.int32)
    kk = cc - ch_base[ce]
    cs = (sub_base[ce] + MOE_CHUNK_SUBS * kk).astype(jnp.int32)
    cn = jnp.where(c < n_used, jnp.minimum(MOE_CHUNK_SUBS, nsub[ce] - MOE_CHUNK_SUBS * kk), 0).astype(jnp.int32)
    tables = (ce, cs, cn, n_used.reshape(1).astype(jnp.int32), jnp.sum(nsub).reshape(1).astype(jnp.int32))
    return gate_kt.T, row_tok, pos, tables


def _combine_kernel(alpha, h_ref, yg_ref, gate_ref, g_ref, b_ref, o_ref):
    gate = gate_ref[...]
    f = yg_ref[0] * gate[:, 0:1]
    for k in range(1, TOP_K):
        f = f + yg_ref[k] * gate[:, k:k + 1]
    o_ref[...] = _layer_norm(alpha * h_ref[...] + f, g_ref[...], b_ref[...])


def combine(h, yg, gate, ln_g, ln_b, alpha, row_off, m, tm):
    blk_off = row_off // tm
    whole = lambda i: (0, 0)
    return pl.pallas_call(
        functools.partial(_combine_kernel, alpha),
        grid=(m // tm,),
        in_specs=[
            pl.BlockSpec((tm, D_MODEL), lambda i: (blk_off + i, 0)),
            pl.BlockSpec((TOP_K, tm, D_MODEL), lambda i: (0, blk_off + i, 0)),
            pl.BlockSpec((tm, TOP_K), lambda i: (blk_off + i, 0)),
            pl.BlockSpec((1, D_MODEL), whole),
            pl.BlockSpec((1, D_MODEL), whole),
        ],
        out_specs=pl.BlockSpec((tm, D_MODEL), lambda i: (i, 0)),
        out_shape=jax.ShapeDtypeStruct((m, D_MODEL), F32),
        compiler_params=_params(("parallel",)),
        name="combine",
    )(h, yg, gate, ln_g, ln_b)


def _retention_tables(chunk, group):
    log_gamma = jnp.log(1.0 - jnp.exp2(-5.0 - jnp.arange(N_RET_HEADS, dtype=F32)))
    n = chunk * group
    pos = jnp.arange(n) % chunk
    seq = jnp.arange(n) // chunk
    diff = (pos[:, None] - pos[None, :]).astype(F32)
    same = seq[:, None] == seq[None, :]
    lg = log_gamma[:, None, None]
    dec = jnp.where(same[None] & (diff >= 0)[None], jnp.exp(lg * jnp.maximum(diff, 0.0)[None]), 0.0)
    posf = pos.astype(F32)
    qd = jnp.exp(log_gamma[:, None] * (posf + 1.0))[..., None]
    kd = jnp.exp(log_gamma[:, None] * (chunk - 1.0 - posf))[..., None]
    cd = jnp.exp(log_gamma * chunk)
    return cd, dec, qd, kd


def _alibi_slopes():
    h = jnp.arange(1, N_SWA_HEADS + 1, dtype=F32)
    return jnp.exp2(-8.0 * h / N_SWA_HEADS)


def kernel(x_prompt, x_sample, state_ret, cache_swa_k, cache_swa_v, w_in, w_ret_o, w_swa_o, w_out, swa_sinks,
           ln1_g, ln1_b, w_router, b_router, w_gate_up, b_gate_up, w_down, b_down, ln2_g, ln2_b):
    alpha = (2.0 * DEPTH) ** 0.25
    kv_w = N_SWA_KV_HEADS * SWA_HEAD_DIM
    w = w_in[0]
    w_in_bf = w.astype(BF16)
    w_ret_o_bf = w_ret_o[0].astype(BF16)
    w_swa_o_bf = w_swa_o[0].astype(BF16)
    w_out_bf = w_out[0].astype(BF16)
    w_router_pad, b_router_pad = _router_split(w_router[0], b_router[0])
    ln1g, ln1b = ln1_g[0].reshape(1, D_MODEL), ln1_b[0].reshape(1, D_MODEL)
    ln2g, ln2b = ln2_g[0].reshape(1, D_MODEL), ln2_b[0].reshape(1, D_MODEL)
    slopes = _alibi_slopes()
    sinks = swa_sinks[0].astype(F32)

    xp = x_prompt.reshape(T_PROMPT, D_MODEL)
    xs = x_sample.reshape(T_SAMPLE, D_MODEL)

    u_p = in_proj(xp, w_in_bf, 1024, 1280)
    o_r_p, state_p = retention_prompt(u_p, _retention_tables(RET_CHUNK, 1))
    o_s_p = swa_prompt(u_p, slopes, sinks)
    kv_p = u_p.reshape(BATCH, SEQ, D_IN)[:, SEQ - WINDOW:, COL_KVS:COL_KVS + 2 * kv_w]
    k_cache_p = kv_p[..., :kv_w].reshape(1, BATCH, WINDOW, N_SWA_KV_HEADS, SWA_HEAD_DIM)
    v_cache_p = kv_p[..., kv_w:].reshape(1, BATCH, WINDOW, N_SWA_KV_HEADS, SWA_HEAD_DIM)
    merged_p = merge_branches(o_r_p, o_s_p, u_p, w_ret_o_bf, w_swa_o_bf, 512)

    u_s = in_proj(xs, w_in_bf, 512, 1280)
    o_r_s, state_s = retention_sample(u_s, state_ret[0], _retention_tables(DEC_SEQ, RS_BB))
    q_t = (u_s[:, COL_QS:COL_QS + 2048]
           .reshape(DEC_BATCH, DEC_SEQ, N_SWA_KV_HEADS, SWA_GROUP, SWA_HEAD_DIM)
           .transpose(0, 2, 1, 3, 4).reshape(DEC_BATCH, N_SWA_KV_HEADS, SS_ROWS, SWA_HEAD_DIM))
    kv_new = u_s[:, COL_KVS:COL_KVS + 2 * kv_w].reshape(DEC_BATCH, DEC_SEQ, 2 * kv_w)
    o_s_t, k_cache_s, v_cache_s = swa_sample(
        q_t, kv_new, cache_swa_k[0].reshape(DEC_BATCH, WINDOW, kv_w),
        cache_swa_v[0].reshape(DEC_BATCH, WINDOW, kv_w), _swa_sample_tables(slopes, sinks))
    o_s_s = (o_s_t.reshape(DEC_BATCH, N_SWA_KV_HEADS, DEC_SEQ, SWA_GROUP, SWA_HEAD_DIM)
             .transpose(0, 2, 1, 3, 4).reshape(T_SAMPLE, 2048).astype(BF16))
    merged_s = merge_branches(o_r_s, o_s_s, u_s, w_ret_o_bf, w_swa_o_bf, 512)

    h, logits = post_mix(merged_p, merged_s, xp, xs, w_out_bf, ln1g, ln1b, w_router_pad, b_router_pad,
                         alpha, 512)

    gate, row_tok, pos, tables = route(logits)
    x_sorted = h.at[row_tok].get(mode="promise_in_bounds")
    y_rows = moe_experts(x_sorted, tables, w_gate_up[0], b_gate_up[0].reshape(N_EXPERTS, 1, 2 * D_FF),
                         w_down[0], b_down[0].reshape(N_EXPERTS, 1, D_MODEL))
    yg = y_rows.at[pos].get(mode="promise_in_bounds")
    y_p = combine(h, yg, gate, ln2g, ln2b, alpha, 0, T_PROMPT, 256)
    y_s = combine(h, yg, gate, ln2g, ln2b, alpha, T_PROMPT, T_SAMPLE, 256)

    return (y_p.reshape(BATCH, SEQ, D_MODEL), y_s.reshape(DEC_BATCH, DEC_SEQ, D_MODEL),
            state_p[None], k_cache_p, v_cache_p,
            state_s[None], k_cache_s.reshape(1, DEC_BATCH, WINDOW, N_SWA_KV_HEADS, SWA_HEAD_DIM),
            v_cache_s.reshape(1, DEC_BATCH, WINDOW, N_SWA_KV_HEADS, SWA_HEAD_DIM))
```

```python
import functools

import jax
import jax.numpy as jnp
from jax import lax
from jax.experimental import pallas as pl
from jax.experimental.pallas import tpu as pltpu

F32 = jnp.float32
BF16 = jnp.bfloat16

D_MODEL = 2048
BATCH = 4
SEQ = 2048
DEC_BATCH = 128
DEC_SEQ = 4
N_RET_HEADS = 8
RET_DK = 128
RET_DV = 256
RET_CHUNK = 128
N_SWA_HEADS = 32
N_SWA_KV_HEADS = 4
SWA_GROUP = 8
SWA_HEAD_DIM = 64
WINDOW = 128
N_EXPERTS = 32
TOP_K = 4
D_FF = 2048
SWIGLU_LIMIT = 7.0
SWIGLU_ALPHA = 1.702
LN_EPS = 1e-5
GN_EPS = 1e-6
NEG_INF = -1e30
DEPTH = 1

T_PROMPT = BATCH * SEQ
T_SAMPLE = DEC_BATCH * DEC_SEQ
T_ALL = T_PROMPT + T_SAMPLE
TK_ALL = T_ALL * TOP_K
D_IN = 12800

COL_QR, COL_KR, COL_VR, COL_GR, COL_QS, COL_KVS, COL_GATE_R, COL_GATE_S = (
    0, 1024, 2048, 4096, 6144, 8192, 8704, 10752)

MOE_SUB = 128
MOE_CHUNK_SUBS = 11
MOE_STATIC_SIZES = (7, 8, 9, 10, 11)
MOE_R = MOE_SUB * MOE_CHUNK_SUBS
MOE_NSUB_MAX = TK_ALL // MOE_SUB + N_EXPERTS
MOE_XROWS = (MOE_NSUB_MAX + MOE_CHUNK_SUBS - 1) * MOE_SUB
MOE_NC = TK_ALL // MOE_R + N_EXPERTS + 2
MOE_TF = 256
MOE_NJ = D_FF // MOE_TF
MOE_VMEM_LIMIT = 60 * 1024 * 1024

VMEM_LIMIT = 56 * 1024 * 1024


def _params(sem):
    return pltpu.CompilerParams(dimension_semantics=sem, vmem_limit_bytes=VMEM_LIMIT)


def _in_proj_kernel(x_ref, w_ref, o_ref):
    o_ref[...] = jnp.dot(x_ref[...].astype(BF16), w_ref[...], preferred_element_type=F32)


def in_proj(x2d, w_bf, tm, tn):
    m, k = x2d.shape
    n = w_bf.shape[1]
    return pl.pallas_call(
        _in_proj_kernel,
        grid=(m // tm, n // tn),
        in_specs=[pl.BlockSpec((tm, k), lambda i, j: (i, 0)),
                  pl.BlockSpec((k, tn), lambda i, j: (0, j))],
        out_specs=pl.BlockSpec((tm, tn), lambda i, j: (i, j)),
        out_shape=jax.ShapeDtypeStruct((m, n), F32),
        compiler_params=_params(("parallel", "parallel")),
        name="in_proj",
    )(x2d, w_bf)


def _group_norm_gate(o, g):
    mu = jnp.mean(o, axis=-1, keepdims=True)
    oc = o - mu
    var = jnp.mean(oc * oc, axis=-1, keepdims=True)
    return oc * lax.rsqrt(var + GN_EPS) * (g * jax.nn.sigmoid(g))


def _dot_nt(a, b):
    return lax.dot_general(a, b, (((1,), (1,)), ((), ())), preferred_element_type=F32)


def _layer_norm(x, g, b):
    mu = jnp.mean(x, axis=-1, keepdims=True)
    xc = x - mu
    var = jnp.mean(xc * xc, axis=-1, keepdims=True)
    return xc * lax.rsqrt(var + LN_EPS) * g + b


def _ret_prompt_kernel(cd_ref, dec_ref, qd_ref, kd_ref, q_ref, k_ref, v_ref, g_ref, o_ref, st_ref):
    c = pl.program_id(1)

    @pl.when(c == 0)
    def _():
        st_ref[...] = jnp.zeros_like(st_ref)

    heads = range(N_RET_HEADS)
    q = jnp.stack([q_ref[:, h * RET_DK:(h + 1) * RET_DK] for h in heads])
    k = jnp.stack([k_ref[:, h * RET_DK:(h + 1) * RET_DK] for h in heads]) * (RET_DK ** -0.5)
    v = jnp.stack([v_ref[:, h * RET_DV:(h + 1) * RET_DV] for h in heads]).astype(BF16)
    g = jnp.stack([g_ref[:, h * RET_DV:(h + 1) * RET_DV] for h in heads])
    state = st_ref[0]
    bmm = functools.partial(lax.dot_general, preferred_element_type=F32)
    scores = bmm(q.astype(BF16), k.astype(BF16), (((2,), (2,)), ((0,), (0,)))) * dec_ref[...]
    o = (bmm(scores.astype(BF16), v, (((2,), (1,)), ((0,), (0,))))
         + bmm((q * qd_ref[...]).astype(BF16), state.astype(BF16), (((2,), (1,)), ((0,), (0,)))))
    kt = jnp.swapaxes(k * kd_ref[...], 1, 2).astype(BF16)
    cd = jnp.stack([jnp.full((1, 1), cd_ref[h], F32) for h in heads])
    st_ref[0] = cd * state + bmm(kt, v, (((2,), (1,)), ((0,), (0,))))
    out = _group_norm_gate(o, g).astype(BF16)
    for h in heads:
        o_ref[:, h * RET_DV:(h + 1) * RET_DV] = out[h]


def retention_prompt(u, tables):
    cd, dec, qd, kd = tables
    nc = SEQ // RET_CHUNK
    row = lambda b, c: b * nc + c
    whole3 = lambda b, c: (0, 0, 0)
    return pl.pallas_call(
        _ret_prompt_kernel,
        grid=(BATCH, nc),
        in_specs=[
            pl.BlockSpec(memory_space=pltpu.SMEM),
            pl.BlockSpec((N_RET_HEADS, RET_CHUNK, RET_CHUNK), whole3),
            pl.BlockSpec((N_RET_HEADS, RET_CHUNK, 1), whole3),
            pl.BlockSpec((N_RET_HEADS, RET_CHUNK, 1), whole3),
            pl.BlockSpec((RET_CHUNK, 1024), lambda b, c: (row(b, c), COL_QR // 1024)),
            pl.BlockSpec((RET_CHUNK, 1024), lambda b, c: (row(b, c), COL_KR // 1024)),
            pl.BlockSpec((RET_CHUNK, 2048), lambda b, c: (row(b, c), COL_VR // 2048)),
            pl.BlockSpec((RET_CHUNK, 2048), lambda b, c: (row(b, c), COL_GR // 2048)),
        ],
        out_specs=[
            pl.BlockSpec((RET_CHUNK, 2048), lambda b, c: (row(b, c), 0)),
            pl.BlockSpec((1, N_RET_HEADS, RET_DK, RET_DV), lambda b, c: (b, 0, 0, 0)),
        ],
        out_shape=[
            jax.ShapeDtypeStruct((T_PROMPT, 2048), BF16),
            jax.ShapeDtypeStruct((BATCH, N_RET_HEADS, RET_DK, RET_DV), F32),
        ],
        compiler_params=_params(("parallel", "arbitrary")),
        name="retention_prompt",
    )(cd, dec, qd, kd, u, u, u, u)


RS_BB = 32


def _ret_sample_kernel(cd_ref, dec_ref, qd_ref, kd_ref, q_ref, k_ref, v_ref, g_ref, st_ref,
                       o_ref, sto_ref):
    h = pl.program_id(1)
    q = q_ref[...]
    k = k_ref[...] * (RET_DK ** -0.5)
    v = v_ref[...].astype(BF16)
    scores = _dot_nt(q.astype(BF16), k.astype(BF16)) * dec_ref[0]
    o_intra = jnp.dot(scores.astype(BF16), v, preferred_element_type=F32)
    qq = (q * qd_ref[0]).astype(BF16)
    kt = (k * kd_ref[0]).T
    col_batch = lax.broadcasted_iota(jnp.int32, kt.shape, 1) // DEC_SEQ
    first_of_pair = lax.broadcasted_iota(jnp.int32, (8, RET_DV), 0) < DEC_SEQ
    cd = cd_ref[h]
    pieces = []
    for p in range(RS_BB // 2):
        q8 = qq[8 * p:8 * p + 8]
        s0 = st_ref[2 * p].astype(BF16)
        s1 = st_ref[2 * p + 1].astype(BF16)
        r0 = jnp.dot(q8, s0, preferred_element_type=F32)
        r1 = jnp.dot(q8, s1, preferred_element_type=F32)
        pieces.append(jnp.where(first_of_pair, r0, r1))
    for b in range(RS_BB):
        ktm = jnp.where(col_batch == b, kt, 0.0).astype(BF16)
        sto_ref[b] = cd * st_ref[b] + jnp.dot(ktm, v, preferred_element_type=F32)
    o = o_intra + jnp.concatenate(pieces, axis=0)
    o_ref[...] = _group_norm_gate(o, g_ref[...]).astype(BF16)


def retention_sample(u_s, state, tables):
    cd, dec, qd, kd = tables
    rows = RS_BB * DEC_SEQ
    per_head = lambda i, h: (h, 0, 0)
    st_spec = pl.BlockSpec((RS_BB, None, RET_DK, RET_DV), lambda i, h: (i, h, 0, 0))
    return pl.pallas_call(
        _ret_sample_kernel,
        grid=(DEC_BATCH // RS_BB, N_RET_HEADS),
        in_specs=[
            pl.BlockSpec(memory_space=pltpu.SMEM),
            pl.BlockSpec((1, rows, rows), per_head),
            pl.BlockSpec((1, rows, 1), per_head),
            pl.BlockSpec((1, rows, 1), per_head),
            pl.BlockSpec((rows, RET_DK), lambda i, h: (i, COL_QR // RET_DK + h)),
            pl.BlockSpec((rows, RET_DK), lambda i, h: (i, COL_KR // RET_DK + h)),
            pl.BlockSpec((rows, RET_DV), lambda i, h: (i, COL_VR // RET_DV + h)),
            pl.BlockSpec((rows, RET_DV), lambda i, h: (i, COL_GR // RET_DV + h)),
            st_spec,
        ],
        out_specs=[pl.BlockSpec((rows, RET_DV), lambda i, h: (i, h)), st_spec],
        out_shape=[
            jax.ShapeDtypeStruct((T_SAMPLE, 2048), BF16),
            jax.ShapeDtypeStruct((DEC_BATCH, N_RET_HEADS, RET_DK, RET_DV), F32),
        ],
        compiler_params=_params(("parallel", "parallel")),
        name="retention_sample",
    )(cd, dec, qd, kd, u_s, u_s, u_s, u_s, state)


def _swa_prompt_kernel(sink_ref, bias_ref, q_ref, kvc_ref, kvp_ref, o_ref):
    kv_w = N_SWA_KV_HEADS * SWA_HEAD_DIM
    hd = SWA_HEAD_DIM
    qt = q_ref[...].T
    outs = []
    for kvh in range(N_SWA_KV_HEADS):
        lo = kvh * hd
        k_cat = jnp.concatenate([kvp_ref[:, lo:lo + hd], kvc_ref[:, lo:lo + hd]], axis=0).astype(BF16)
        v_cat = jnp.concatenate([kvp_ref[:, kv_w + lo:kv_w + lo + hd],
                                 kvc_ref[:, kv_w + lo:kv_w + lo + hd]], axis=0)
        vt = v_cat.T.astype(BF16)
        heads = range(kvh * SWA_GROUP, (kvh + 1) * SWA_GROUP)
        q8 = jnp.concatenate([qt[h * hd:(h + 1) * hd, :] for h in heads], axis=1)
        q8 = (q8 * (hd ** -0.5)).astype(BF16)
        s = jnp.dot(k_cat, q8, preferred_element_type=F32) + bias_ref[0, kvh]
        sink = sink_ref[kvh]
        m = jnp.maximum(jnp.max(s, axis=0, keepdims=True), sink)
        e = jnp.exp(s - m)
        den = jnp.sum(e, axis=0, keepdims=True) + jnp.exp(sink - m)
        p = (e * (1.0 / den)).astype(BF16)
        ot = jnp.dot(vt, p, preferred_element_type=F32)
        outs += [ot[:, g * WINDOW:(g + 1) * WINDOW] for g in range(SWA_GROUP)]
    o_ref[...] = jnp.concatenate(outs, axis=0).T.astype(BF16)


def _swa_prompt_tables(slopes, sinks):
    qi = jnp.arange(WINDOW)[:, None] + WINDOW
    ki = jnp.arange(2 * WINDOW)[None, :]
    dist = qi - ki
    in_window = (dist >= 0) & (dist < WINDOW)
    valid = jnp.stack([in_window & (ki >= WINDOW), in_window])
    pen = -(slopes[:, None, None] * dist.astype(F32)[None])
    bias = jnp.where(valid[:, None], pen[None], NEG_INF)
    bias = bias.reshape(2, N_SWA_KV_HEADS, SWA_GROUP, WINDOW, 2 * WINDOW).transpose(0, 1, 4, 2, 3)
    bias = bias.reshape(2, N_SWA_KV_HEADS, 2 * WINDOW, SWA_GROUP * WINDOW)
    sink_l = jnp.repeat(sinks.reshape(N_SWA_KV_HEADS, SWA_GROUP), WINDOW, axis=1)
    return sink_l.reshape(N_SWA_KV_HEADS, 1, SWA_GROUP * WINDOW), bias


def swa_prompt(u, slopes, sinks):
    nb = SEQ // WINDOW
    kv_blk = COL_KVS // 512
    sink_l, bias = _swa_prompt_tables(slopes, sinks)
    lanes = SWA_GROUP * WINDOW
    return pl.pallas_call(
        _swa_prompt_kernel,
        grid=(BATCH, nb),
        in_specs=[
            pl.BlockSpec((N_SWA_KV_HEADS, 1, lanes), lambda b, j: (0, 0, 0)),
            pl.BlockSpec((1, N_SWA_KV_HEADS, 2 * WINDOW, lanes), lambda b, j: (jnp.minimum(j, 1), 0, 0, 0)),
            pl.BlockSpec((WINDOW, 2048), lambda b, j: (b * nb + j, COL_QS // 2048)),
            pl.BlockSpec((WINDOW, 512), lambda b, j: (b * nb + j, kv_blk)),
            pl.BlockSpec((WINDOW, 512), lambda b, j: (b * nb + jnp.maximum(j - 1, 0), kv_blk)),
        ],
        out_specs=pl.BlockSpec((WINDOW, 2048), lambda b, j: (b * nb + j, 0)),
        out_shape=jax.ShapeDtypeStruct((T_PROMPT, 2048), BF16),
        compiler_params=_params(("parallel", "parallel")),
        name="swa_prompt",
    )(sink_l, bias, u, u, u)


SS_BB = 8
SS_ROWS = DEC_SEQ * SWA_GROUP


def _bf16_round(x):
    return x.astype(BF16).astype(F32)


def _swa_sample_kernel(sink_ref, bias1_ref, bias2_ref, q_ref, kvn_ref, ck_ref, cv_ref,
                       o_ref, cko_ref, cvo_ref):
    kv_w = N_SWA_KV_HEADS * SWA_HEAD_DIM
    hd = SWA_HEAD_DIM
    scale = hd ** -0.5
    for kvh in range(N_SWA_KV_HEADS):
        lo = kvh * hd
        sink = sink_ref[kvh]
        q = q_ref[:, kvh]
        ck = ck_ref[:, :, lo:lo + hd].astype(BF16)
        cv = cv_ref[:, :, lo:lo + hd].astype(BF16)
        kn = _bf16_round(kvn_ref[:, :, lo:lo + hd])
        vn = _bf16_round(kvn_ref[:, :, kv_w + lo:kv_w + lo + hd])
        s1 = lax.dot_general(q.astype(BF16), ck, (((2,), (2,)), ((0,), (0,))),
                             preferred_element_type=F32) * scale + bias1_ref[kvh]
        qr = _bf16_round(q)
        s2 = [jnp.sum(qr * kn[:, t2:t2 + 1, :], axis=-1, keepdims=True) * scale + bias2_ref[kvh, t2]
              for t2 in range(DEC_SEQ)]
        m = jnp.maximum(jnp.max(s1, axis=-1, keepdims=True), sink)
        for st in s2:
            m = jnp.maximum(m, st)
        e1 = jnp.exp(s1 - m)
        e2 = [jnp.exp(st - m) for st in s2]
        den = jnp.sum(e1, axis=-1, keepdims=True) + jnp.exp(sink - m)
        for et in e2:
            den = den + et
        inv = 1.0 / den
        o = lax.dot_general((e1 * inv).astype(BF16), cv, (((2,), (1,)), ((0,), (0,))),
                            preferred_element_type=F32)
        for t2 in range(DEC_SEQ):
            o = o + _bf16_round(e2[t2] * inv) * vn[:, t2:t2 + 1, :]
        o_ref[:, kvh] = o
    keep = WINDOW - DEC_SEQ
    cko_ref[:, 0:keep, :] = ck_ref[:, DEC_SEQ:WINDOW, :]
    cko_ref[:, keep:WINDOW, :] = kvn_ref[:, :, 0:kv_w]
    cvo_ref[:, 0:keep, :] = cv_ref[:, DEC_SEQ:WINDOW, :]
    cvo_ref[:, keep:WINDOW, :] = kvn_ref[:, :, kv_w:2 * kv_w]


def _swa_sample_tables(slopes, sinks):
    g_slopes = jnp.tile(slopes.reshape(N_SWA_KV_HEADS, 1, SWA_GROUP), (1, DEC_SEQ, 1)).reshape(
        N_SWA_KV_HEADS, SS_ROWS)
    sink_rows = jnp.tile(sinks.reshape(N_SWA_KV_HEADS, 1, SWA_GROUP), (1, DEC_SEQ, 1)).reshape(
        N_SWA_KV_HEADS, SS_ROWS, 1)
    t_row = jnp.arange(SS_ROWS) // SWA_GROUP
    jcol = jnp.arange(WINDOW)
    dist1 = (t_row[:, None] + WINDOW - jcol[None, :])
    bias1 = jnp.where((dist1 < WINDOW)[None], -(g_slopes[:, :, None] * dist1.astype(F32)[None]), NEG_INF)
    t2 = jnp.arange(DEC_SEQ)
    dist2 = t_row[None, :] - t2[:, None]
    bias2 = jnp.where((dist2 >= 0)[None], -(g_slopes[:, None, :] * dist2.astype(F32)[None]), NEG_INF)
    return sink_rows, bias1, bias2[..., None]


def swa_sample(q_t, kv_new, cache_k, cache_v, tables):
    sink_rows, bias1, bias2 = tables
    kv_w = N_SWA_KV_HEADS * SWA_HEAD_DIM
    whole3 = lambda i: (0, 0, 0)
    cache_spec = pl.BlockSpec((SS_BB, WINDOW, kv_w), lambda i: (i, 0, 0))
    return pl.pallas_call(
        _swa_sample_kernel,
        grid=(DEC_BATCH // SS_BB,),
        in_specs=[
            pl.BlockSpec((N_SWA_KV_HEADS, SS_ROWS, 1), whole3),
            pl.BlockSpec((N_SWA_KV_HEADS, SS_ROWS, WINDOW), whole3),
            pl.BlockSpec((N_SWA_KV_HEADS, DEC_SEQ, SS_ROWS, 1), lambda i: (0, 0, 0, 0)),
            pl.BlockSpec((SS_BB, N_SWA_KV_HEADS, SS_ROWS, SWA_HEAD_DIM), lambda i: (i, 0, 0, 0)),
            pl.BlockSpec((SS_BB, DEC_SEQ, 2 * kv_w), lambda i: (i, 0, 0)),
            cache_spec, cache_spec,
        ],
        out_specs=[
            pl.BlockSpec((SS_BB, N_SWA_KV_HEADS, SS_ROWS, SWA_HEAD_DIM), lambda i: (i, 0, 0, 0)),
            cache_spec, cache_spec,
        ],
        out_shape=[
            jax.ShapeDtypeStruct((DEC_BATCH, N_SWA_KV_HEADS, SS_ROWS, SWA_HEAD_DIM), F32),
            jax.ShapeDtypeStruct((DEC_BATCH, WINDOW, kv_w), F32),
            jax.ShapeDtypeStruct((DEC_BATCH, WINDOW, kv_w), F32),
        ],
        compiler_params=_params(("parallel",)),
        name="swa_sample",
    )(sink_rows, bias1, bias2, q_t, kv_new, cache_k, cache_v)


def _merge_kernel(or_ref, os_ref, gr_ref, gs_ref, wr_ref, ws_ref, o_ref):
    a = jnp.dot(or_ref[...], wr_ref[...], preferred_element_type=F32)
    b = jnp.dot(os_ref[...], ws_ref[...], preferred_element_type=F32)
    merged = jax.nn.sigmoid(gr_ref[...]) * a + jax.nn.sigmoid(gs_ref[...]) * b
    o_ref[...] = merged.astype(BF16)


def merge_branches(o_r, o_s, u, w_ret_o_bf, w_swa_o_bf, tm):
    m = o_r.shape[0]
    resident = pl.BlockSpec((2048, D_MODEL), lambda i: (0, 0), pipeline_mode=pl.Buffered(1))
    return pl.pallas_call(
        _merge_kernel,
        grid=(m // tm,),
        in_specs=[
            pl.BlockSpec((tm, 2048), lambda i: (i, 0)),
            pl.BlockSpec((tm, 2048), lambda i: (i, 0)),
            pl.BlockSpec((pl.Element(tm), pl.Element(D_MODEL)), lambda i: (i * tm, COL_GATE_R)),
            pl.BlockSpec((pl.Element(tm), pl.Element(D_MODEL)), lambda i: (i * tm, COL_GATE_S)),
            resident, resident,
        ],
        out_specs=pl.BlockSpec((tm, D_MODEL), lambda i: (i, 0)),
        out_shape=jax.ShapeDtypeStruct((m, D_MODEL), BF16),
        compiler_params=_params(("parallel",)),
        name="merge_branches",
    )(o_r, o_s, u, u, w_ret_o_bf, w_swa_o_bf)


ROUTER_PAD = 128


def _post_mix_kernel(alpha, n_prompt_blocks, mp_ref, ms_ref, xp_ref, xs_ref, w_ref, g_ref, b_ref, wr_ref, br_ref,
                     h_ref, lg_ref):
    from_sample = pl.program_id(0) >= n_prompt_blocks
    merged = jnp.where(from_sample, ms_ref[...], mp_ref[...])
    x = jnp.where(from_sample, xs_ref[...], xp_ref[...])
    mix = jnp.dot(merged, w_ref[...], preferred_element_type=F32)
    h = _layer_norm(alpha * x + mix, g_ref[...], b_ref[...])
    h_ref[...] = h
    h_hi = h.astype(BF16)
    h_lo = (h - h_hi.astype(F32)).astype(BF16)
    hh = jnp.dot(h_hi, wr_ref[...], preferred_element_type=F32)
    lh = jnp.dot(h_lo, wr_ref[:, :ROUTER_PAD], preferred_element_type=F32)
    lg_ref[...] = (hh[:, :ROUTER_PAD] + hh[:, ROUTER_PAD:] + lh + br_ref[...]).T


def _router_split(w_router, b_router):
    w = jnp.pad(w_router, ((0, 0), (0, ROUTER_PAD - N_EXPERTS)))
    w_hi = w.astype(BF16)
    w_lo = (w - w_hi.astype(F32)).astype(BF16)
    b = jnp.pad(b_router, (0, ROUTER_PAD - N_EXPERTS)).reshape(1, ROUTER_PAD)
    return jnp.concatenate([w_hi, w_lo], axis=1), b


def post_mix(merged_p, merged_s, xp, xs, w_out_bf, ln_g, ln_b, w_router_pad, b_router_pad, alpha, tm):
    n_p = T_PROMPT // tm
    n_s = T_SAMPLE // tm
    prompt_row = lambda i: (jnp.minimum(i, n_p - 1), 0)
    sample_row = lambda i: (jnp.clip(i - n_p, 0, n_s - 1), 0)
    whole = lambda i: (0, 0)
    row = lambda i: (i, 0)
    return pl.pallas_call(
        functools.partial(_post_mix_kernel, alpha, n_p),
        grid=(n_p + n_s,),
        in_specs=[
            pl.BlockSpec((tm, 2048), prompt_row),
            pl.BlockSpec((tm, 2048), sample_row),
            pl.BlockSpec((tm, 2048), prompt_row),
            pl.BlockSpec((tm, 2048), sample_row),
            pl.BlockSpec((2048, 2048), whole),
            pl.BlockSpec((1, 2048), whole),
            pl.BlockSpec((1, 2048), whole),
            pl.BlockSpec((2048, 2 * ROUTER_PAD), whole),
            pl.BlockSpec((1, ROUTER_PAD), whole),
        ],
        out_specs=[pl.BlockSpec((tm, 2048), row), pl.BlockSpec((ROUTER_PAD, tm), lambda i: (0, i))],
        out_shape=[jax.ShapeDtypeStruct((T_ALL, D_MODEL), F32),
                   jax.ShapeDtypeStruct((ROUTER_PAD, T_ALL), F32)],
        compiler_params=_params(("parallel",)),
        name="post_mix",
    )(merged_p, merged_s, xp, xs, w_out_bf, ln_g, ln_b, w_router_pad, b_router_pad)


def _moe_kernel(ce_ref, cs_ref, cn_ref, nu_ref, tail_ref, x_hbm, wg_ref, wu_ref, wd_ref, bg_ref, bu_ref, bd_ref,
                y_hbm, x_f32, x_bf, y_acc, zero_buf, x_sem, y_sem, z_sem):
    c = pl.program_id(0)
    j = pl.program_id(1)
    nblk = cn_ref[c]
    slot = c % 2
    y_cur = y_acc.at[slot]

    def sub_rows(r):
        return pl.ds(pl.multiple_of(r * MOE_SUB, MOE_SUB), MOE_SUB)

    def x_copy(chunk):
        start = pl.multiple_of(cs_ref[chunk] * MOE_SUB, MOE_SUB)
        return pltpu.make_async_copy(x_hbm.at[pl.ds(start, MOE_R), :], x_f32, x_sem)

    def y_copy(r, sub):
        return pltpu.make_async_copy(y_cur.at[sub_rows(r), :], y_hbm.at[sub_rows(sub), :], y_sem.at[slot])

    def zero_copy(sub):
        return pltpu.make_async_copy(zero_buf, y_hbm.at[sub_rows(sub), :], z_sem)

    @pl.when(j == 0)
    def _():
        def drain(i, carry):
            y_copy(0, 0).wait()
            return carry

        lax.fori_loop(0, cn_ref[jnp.maximum(c - 2, 0)] * (c >= 2).astype(jnp.int32), drain, 0)

    @pl.when((nblk > 0) & (j == 0))
    def _():
        @pl.when(c == 0)
        def _():
            x_copy(0).start()

        x_copy(c).wait()

        def stage(r, carry):
            rows = sub_rows(r)
            x_bf[rows, :] = x_f32[rows, :].astype(BF16)
            y_cur[rows, :] = jnp.broadcast_to(bd_ref[0], (MOE_SUB, D_MODEL))
            return carry

        lax.fori_loop(0, nblk, stage, 0)

        @pl.when(c + 1 < nu_ref[0])
        def _():
            x_copy(c + 1).start()

    @pl.when(nblk > 0)
    def _():
        def rows_block(rows):
            xr = x_bf[rows, :]
            g = jnp.dot(xr, wg_ref[0], preferred_element_type=F32) + bg_ref[0]
            up = jnp.dot(xr, wu_ref[0], preferred_element_type=F32) + bu_ref[0]
            g = jnp.minimum(g, SWIGLU_LIMIT)
            up = jnp.clip(up, -SWIGLU_LIMIT, SWIGLU_LIMIT)
            glu = g * jax.nn.sigmoid(SWIGLU_ALPHA * g)
            act = ((up + 1.0) * glu).astype(BF16)
            y_cur[rows, :] += jnp.dot(act, wd_ref[0], preferred_element_type=F32)

        for n_static in MOE_STATIC_SIZES:
            @pl.when(nblk == n_static)
            def _(n_static=n_static):
                rows_block(pl.ds(0, n_static * MOE_SUB))

        @pl.when(nblk < MOE_STATIC_SIZES[0])
        def _():
            def one(r, carry):
                rows_block(sub_rows(r))
                return carry

            lax.fori_loop(0, nblk, one, 0)

    @pl.when((nblk > 0) & (j == MOE_NJ - 1))
    def _():
        def write_back(r, carry):
            y_copy(r, cs_ref[c] + r).start()
            return carry

        lax.fori_loop(0, nblk, write_back, 0)

    @pl.when((nblk == 0) & (j == 0))
    def _():
        first = tail_ref[0] + MOE_CHUNK_SUBS * (c - nu_ref[0])
        count = jnp.clip(MOE_XROWS // MOE_SUB - first, 0, MOE_CHUNK_SUBS)

        @pl.when(count > 0)
        def _():
            zero_buf[...] = jnp.zeros_like(zero_buf)

            def start(i, carry):
                zero_copy(first + i).start()
                return carry

            def wait(i, carry):
                zero_copy(first + i).wait()
                return carry

            lax.fori_loop(0, count, start, 0)
            lax.fori_loop(0, count, wait, 0)


def moe_experts(x_sorted, chunk_tables, w_gate_up, b_gate_up, w_down, b_down):
    nj = MOE_NJ

    def used_j(c, j, nu):
        return jnp.where(c < nu[0], j, nj - 1)

    return pl.pallas_call(
        _moe_kernel,
        grid_spec=pltpu.PrefetchScalarGridSpec(
            num_scalar_prefetch=5,
            grid=(MOE_NC, nj),
            in_specs=[
                pl.BlockSpec(memory_space=pl.ANY),
                pl.BlockSpec((1, D_MODEL, MOE_TF),
                             lambda c, j, ce, cs, cn, nu, tl: (ce[c], 0, used_j(c, j, nu))),
                pl.BlockSpec((1, D_MODEL, MOE_TF),
                             lambda c, j, ce, cs, cn, nu, tl: (ce[c], 0, nj + used_j(c, j, nu))),
                pl.BlockSpec((1, MOE_TF, D_MODEL),
                             lambda c, j, ce, cs, cn, nu, tl: (ce[c], used_j(c, j, nu), 0)),
                pl.BlockSpec((1, 1, MOE_TF),
                             lambda c, j, ce, cs, cn, nu, tl: (ce[c], 0, used_j(c, j, nu))),
                pl.BlockSpec((1, 1, MOE_TF),
                             lambda c, j, ce, cs, cn, nu, tl: (ce[c], 0, nj + used_j(c, j, nu))),
                pl.BlockSpec((1, 1, D_MODEL), lambda c, j, ce, cs, cn, nu, tl: (ce[c], 0, 0)),
            ],
            out_specs=pl.BlockSpec(memory_space=pl.ANY),
            scratch_shapes=[
                pltpu.VMEM((MOE_R, D_MODEL), F32),
                pltpu.VMEM((MOE_R, D_MODEL), BF16),
                pltpu.VMEM((2, MOE_R, D_MODEL), F32),
                pltpu.VMEM((MOE_SUB, D_MODEL), F32),
                pltpu.SemaphoreType.DMA(()),
                pltpu.SemaphoreType.DMA((2,)),
                pltpu.SemaphoreType.DMA(()),
            ],
        ),
        out_shape=jax.ShapeDtypeStruct((MOE_XROWS, D_MODEL), F32),
        compiler_params=pltpu.CompilerParams(dimension_semantics=("arbitrary", "arbitrary"),
                                             vmem_limit_bytes=MOE_VMEM_LIMIT),
        name="moe_experts",
    )(*chunk_tables, x_sorted, w_gate_up, w_gate_up, w_down, b_gate_up, b_gate_up, b_down)


ROUTE_TM = 512


def _route_kernel(lg_ref, tri_ref, e_ref, rank_ref, gate_ref, cnt_ref, seen):
    @pl.when(pl.program_id(0) == 0)
    def _():
        seen[...] = jnp.zeros_like(seen)

    work = lg_ref[0:N_EXPERTS, :]
    eidx = lax.broadcasted_iota(jnp.int32, work.shape, 0)
    vals, idxs, hots = [], [], []
    for _ in range(TOP_K):
        m = jnp.max(work, axis=0, keepdims=True)
        idx = jnp.min(jnp.where(work == m, eidx, N_EXPERTS), axis=0, keepdims=True)
        hot = eidx == idx
        vals.append(m)
        idxs.append(idx)
        hots.append(hot)
        work = jnp.where(hot, -jnp.inf, work)
    ex = [jnp.exp(v - vals[0]) for v in vals]
    inv = 1.0 / (ex[0] + ex[1] + ex[2] + ex[3])
    picked = jnp.zeros(work.shape, F32)
    for hot in hots:
        picked = picked + hot.astype(F32)
    before = jnp.dot(picked.astype(BF16), tri_ref[...], preferred_element_type=F32) + seen[...]
    ranks = [jnp.sum(jnp.where(hot, before, 0.0), axis=0, keepdims=True) for hot in hots]
    seen[...] += jnp.sum(picked, axis=1, keepdims=True)
    e_ref[...] = jnp.concatenate(idxs, axis=0)
    rank_ref[...] = jnp.concatenate(ranks, axis=0).astype(jnp.int32)
    gate_ref[...] = jnp.concatenate([e * inv for e in ex], axis=0)
    cnt_ref[...] = seen[...]


def route_topk(logits_t):
    tm = ROUTE_TM
    tri = (jnp.arange(tm)[:, None] < jnp.arange(tm)[None, :]).astype(BF16)
    pick = pl.BlockSpec((TOP_K, tm), lambda i: (0, i))
    return pl.pallas_call(
        _route_kernel,
        grid=(T_ALL // tm,),
        in_specs=[pl.BlockSpec((ROUTER_PAD, tm), lambda i: (0, i)),
                  pl.BlockSpec((tm, tm), lambda i: (0, 0))],
        out_specs=[pick, pick, pick, pl.BlockSpec((N_EXPERTS, 1), lambda i: (0, 0))],
        out_shape=[jax.ShapeDtypeStruct((TOP_K, T_ALL), jnp.int32),
                   jax.ShapeDtypeStruct((TOP_K, T_ALL), jnp.int32),
                   jax.ShapeDtypeStruct((TOP_K, T_ALL), F32),
                   jax.ShapeDtypeStruct((N_EXPERTS, 1), F32)],
        scratch_shapes=[pltpu.VMEM((N_EXPERTS, 1), F32)],
        compiler_params=_params(("arbitrary",)),
        name="route_topk",
    )(logits_t, tri)


def route(logits_t):
    e_kt, rank, gate_kt, counts_f = route_topk(logits_t)
    counts = counts_f.reshape(N_EXPERTS).astype(jnp.int32)
    nsub = (counts + MOE_SUB - 1) // MOE_SUB
    sub_base = jnp.cumsum(nsub) - nsub
    nch = (counts + MOE_R - 1) // MOE_R
    ch_end = jnp.cumsum(nch)
    ch_base = ch_end - nch
    n_used = ch_end[-1]
    is_e = e_kt[..., None] == jnp.arange(N_EXPERTS, dtype=jnp.int32)
    pos = MOE_SUB * jnp.sum(jnp.where(is_e, sub_base, 0), axis=-1) + rank
    tok = jnp.broadcast_to(jnp.arange(T_ALL, dtype=jnp.int32)[None, :], pos.shape)
    pad_i = jnp.arange(MOE_SUB - 1, dtype=jnp.int32)
    pad_pos = (MOE_SUB * sub_base + counts)[:, None] + pad_i[None, :]
    pad_pos = jnp.where(pad_i[None, :] < (MOE_SUB * nsub - counts)[:, None], pad_pos, MOE_XROWS)
    tail_pos = MOE_SUB * jnp.sum(nsub) + jnp.arange(MOE_XROWS - TK_ALL, dtype=jnp.int32)
    tail_pos = jnp.minimum(tail_pos, MOE_XROWS)
    keys = jnp.concatenate([pos.reshape(-1), pad_pos.reshape(-1), tail_pos]).astype(jnp.int32)
    vals = jnp.concatenate([tok.reshape(-1), jnp.zeros((keys.shape[0] - TK_ALL,), jnp.int32)])
    row_tok = lax.sort_key_val(keys, vals)[1][:MOE_XROWS]
    c = jnp.arange(MOE_NC, dtype=jnp.int32)
    cc = jnp.minimum(c, n_used - 1)
    ce = jnp.minimum(jnp.sum(cc[:, None] >= ch_end[None, :], axis=1), N_EXPERTS - 1).astype(jnp.int32)
    kk = cc - ch_base[ce]
    cs = (sub_base[ce] + MOE_CHUNK_SUBS * kk).astype(jnp.int32)
    cn = jnp.where(c < n_used, jnp.minimum(MOE_CHUNK_SUBS, nsub[ce] - MOE_CHUNK_SUBS * kk), 0).astype(jnp.int32)
    tables = (ce, cs, cn, n_used.reshape(1).astype(jnp.int32), jnp.sum(nsub).reshape(1).astype(jnp.int32))
    return gate_kt.T, row_tok, pos, tables


def _combine_kernel(alpha, h_ref, yg_ref, gate_ref, g_ref, b_ref, o_ref):
    gate = gate_ref[...]
    f = yg_ref[0] * gate[:, 0:1]
    for k in range(1, TOP_K):
        f = f + yg_ref[k] * gate[:, k:k + 1]
    o_ref[...] = _layer_norm(alpha * h_ref[...] + f, g_ref[...], b_ref[...])


def combine(h, yg, gate, ln_g, ln_b, alpha, row_off, m, tm):
    blk_off = row_off // tm
    whole = lambda i: (0, 0)
    return pl.pallas_call(
        functools.partial(_combine_kernel, alpha),
        grid=(m // tm,),
        in_specs=[
            pl.BlockSpec((tm, D_MODEL), lambda i: (blk_off + i, 0)),
            pl.BlockSpec((TOP_K, tm, D_MODEL), lambda i: (0, blk_off + i, 0)),
            pl.BlockSpec((tm, TOP_K), lambda i: (blk_off + i, 0)),
            pl.BlockSpec((1, D_MODEL), whole),
            pl.BlockSpec((1, D_MODEL), whole),
        ],
        out_specs=pl.BlockSpec((tm, D_MODEL), lambda i: (i, 0)),
        out_shape=jax.ShapeDtypeStruct((m, D_MODEL), F32),
        compiler_params=_params(("parallel",)),
        name="combine",
    )(h, yg, gate, ln_g, ln_b)


def _retention_tables(chunk, group):
    log_gamma = jnp.log(1.0 - jnp.exp2(-5.0 - jnp.arange(N_RET_HEADS, dtype=F32)))
    n = chunk * group
    pos = jnp.arange(n) % chunk
    seq = jnp.arange(n) // chunk
    diff = (pos[:, None] - pos[None, :]).astype(F32)
    same = seq[:, None] == seq[None, :]
    lg = log_gamma[:, None, None]
    dec = jnp.where(same[None] & (diff >= 0)[None], jnp.exp(lg * jnp.maximum(diff, 0.0)[None]), 0.0)
    posf = pos.astype(F32)
    qd = jnp.exp(log_gamma[:, None] * (posf + 1.0))[..., None]
    kd = jnp.exp(log_gamma[:, None] * (chunk - 1.0 - posf))[..., None]
    cd = jnp.exp(log_gamma * chunk)
    return cd, dec, qd, kd


def _alibi_slopes():
    h = jnp.arange(1, N_SWA_HEADS + 1, dtype=F32)
    return jnp.exp2(-8.0 * h / N_SWA_HEADS)


def kernel(x_prompt, x_sample, state_ret, cache_swa_k, cache_swa_v, w_in, w_ret_o, w_swa_o, w_out, swa_sinks,
           ln1_g, ln1_b, w_router, b_router, w_gate_up, b_gate_up, w_down, b_down, ln2_g, ln2_b):
    alpha = (2.0 * DEPTH) ** 0.25
    kv_w = N_SWA_KV_HEADS * SWA_HEAD_DIM
    w = w_in[0]
    w_in_bf = w.astype(BF16)
    w_ret_o_bf = w_ret_o[0].astype(BF16)
    w_swa_o_bf = w_swa_o[0].astype(BF16)
    w_out_bf = w_out[0].astype(BF16)
    w_router_pad, b_router_pad = _router_split(w_router[0], b_router[0])
    ln1g, ln1b = ln1_g[0].reshape(1, D_MODEL), ln1_b[0].reshape(1, D_MODEL)
    ln2g, ln2b = ln2_g[0].reshape(1, D_MODEL), ln2_b[0].reshape(1, D_MODEL)
    slopes = _alibi_slopes()
    sinks = swa_sinks[0].astype(F32)

    xp = x_prompt.reshape(T_PROMPT, D_MODEL)
    xs = x_sample.reshape(T_SAMPLE, D_MODEL)

    u_p = in_proj(xp, w_in_bf, 1024, 1280)
    o_r_p, state_p = retention_prompt(u_p, _retention_tables(RET_CHUNK, 1))
    o_s_p = swa_prompt(u_p, slopes, sinks)
    kv_p = u_p.reshape(BATCH, SEQ, D_IN)[:, SEQ - WINDOW:, COL_KVS:COL_KVS + 2 * kv_w]
    k_cache_p = kv_p[..., :kv_w].reshape(1, BATCH, WINDOW, N_SWA_KV_HEADS, SWA_HEAD_DIM)
    v_cache_p = kv_p[..., kv_w:].reshape(1, BATCH, WINDOW, N_SWA_KV_HEADS, SWA_HEAD_DIM)
    merged_p = merge_branches(o_r_p, o_s_p, u_p, w_ret_o_bf, w_swa_o_bf, 512)

    u_s = in_proj(xs, w_in_bf, 512, 1280)
    o_r_s, state_s = retention_sample(u_s, state_ret[0], _retention_tables(DEC_SEQ, RS_BB))
    q_t = (u_s[:, COL_QS:COL_QS + 2048]
           .reshape(DEC_BATCH, DEC_SEQ, N_SWA_KV_HEADS, SWA_GROUP, SWA_HEAD_DIM)
           .transpose(0, 2, 1, 3, 4).reshape(DEC_BATCH, N_SWA_KV_HEADS, SS_ROWS, SWA_HEAD_DIM))
    kv_new = u_s[:, COL_KVS:COL_KVS + 2 * kv_w].reshape(DEC_BATCH, DEC_SEQ, 2 * kv_w)
    o_s_t, k_cache_s, v_cache_s = swa_sample(
        q_t, kv_new, cache_swa_k[0].reshape(DEC_BATCH, WINDOW, kv_w),
        cache_swa_v[0].reshape(DEC_BATCH, WINDOW, kv_w), _swa_sample_tables(slopes, sinks))
    o_s_s = (o_s_t.reshape(DEC_BATCH, N_SWA_KV_HEADS, DEC_SEQ, SWA_GROUP, SWA_HEAD_DIM)
             .transpose(0, 2, 1, 3, 4).reshape(T_SAMPLE, 2048).astype(BF16))
    merged_s = merge_branches(o_r_s, o_s_s, u_s, w_ret_o_bf, w_swa_o_bf, 512)

    h, logits = post_mix(merged_p, merged_s, xp, xs, w_out_bf, ln1g, ln1b, w_router_pad, b_router_pad,
                         alpha, 512)

    gate, row_tok, pos, tables = route(logits)
    x_sorted = h.at[row_tok].get(mode="promise_in_bounds")
    y_rows = moe_experts(x_sorted, tables, w_gate_up[0], b_gate_up[0].reshape(N_EXPERTS, 1, 2 * D_FF),
                         w_down[0], b_down[0].reshape(N_EXPERTS, 1, D_MODEL))
    yg = y_rows.at[pos].get(mode="promise_in_bounds")
    y_p = combine(h, yg, gate, ln2g, ln2b, alpha, 0, T_PROMPT, 256)
    y_s = combine(h, yg, gate, ln2g, ln2b, alpha, T_PROMPT, T_SAMPLE, 256)

    return (y_p.reshape(BATCH, SEQ, D_MODEL), y_s.reshape(DEC_BATCH, DEC_SEQ, D_MODEL),
            state_p[None], k_cache_p, v_cache_p,
            state_s[None], k_cache_s.reshape(1, DEC_BATCH, WINDOW, N_SWA_KV_HEADS, SWA_HEAD_DIM),
            v_cache_s.reshape(1, DEC_BATCH, WINDOW, N_SWA_KV_HEADS, SWA_HEAD_DIM))
```

```python
import functools

import jax
import jax.numpy as jnp
from jax import lax
from jax.experimental import pallas as pl
from jax.experimental.pallas import tpu as pltpu

F32 = jnp.float32
BF16 = jnp.bfloat16

D_MODEL = 2048
BATCH = 4
SEQ = 2048
DEC_BATCH = 128
DEC_SEQ = 4
N_RET_HEADS = 8
RET_DK = 128
RET_DV = 256
RET_CHUNK = 128
N_SWA_HEADS = 32
N_SWA_KV_HEADS = 4
SWA_GROUP = 8
SWA_HEAD_DIM = 64
WINDOW = 128
N_EXPERTS = 32
TOP_K = 4
D_FF = 2048
SWIGLU_LIMIT = 7.0
SWIGLU_ALPHA = 1.702
LN_EPS = 1e-5
GN_EPS = 1e-6
NEG_INF = -1e30
DEPTH = 1

T_PROMPT = BATCH * SEQ
T_SAMPLE = DEC_BATCH * DEC_SEQ
T_ALL = T_PROMPT + T_SAMPLE
TK_ALL = T_ALL * TOP_K
D_IN = 12800

COL_QR, COL_KR, COL_VR, COL_GR, COL_QS, COL_KVS, COL_GATE_R, COL_GATE_S = (
    0, 1024, 2048, 4096, 6144, 8192, 8704, 10752)

MOE_SUB = 128
MOE_CHUNK_SUBS = 11
MOE_STATIC_SIZES = (7, 8, 9, 10, 11)
MOE_R = MOE_SUB * MOE_CHUNK_SUBS
MOE_NSUB_MAX = TK_ALL // MOE_SUB + N_EXPERTS
MOE_XROWS = (MOE_NSUB_MAX + MOE_CHUNK_SUBS - 1) * MOE_SUB
MOE_NC = TK_ALL // MOE_R + N_EXPERTS + 2
MOE_TF = 256
MOE_NJ = D_FF // MOE_TF
MOE_VMEM_LIMIT = 60 * 1024 * 1024

VMEM_LIMIT = 56 * 1024 * 1024


def _params(sem):
    return pltpu.CompilerParams(dimension_semantics=sem, vmem_limit_bytes=VMEM_LIMIT)


def _in_proj_kernel(x_ref, w_ref, o_ref):
    o_ref[...] = jnp.dot(x_ref[...].astype(BF16), w_ref[...], preferred_element_type=F32)


def in_proj(x2d, w_bf, tm, tn):
    m, k = x2d.shape
    n = w_bf.shape[1]
    return pl.pallas_call(
        _in_proj_kernel,
        grid=(m // tm, n // tn),
        in_specs=[pl.BlockSpec((tm, k), lambda i, j: (i, 0)),
                  pl.BlockSpec((k, tn), lambda i, j: (0, j))],
        out_specs=pl.BlockSpec((tm, tn), lambda i, j: (i, j)),
        out_shape=jax.ShapeDtypeStruct((m, n), F32),
        compiler_params=_params(("parallel", "parallel")),
        name="in_proj",
    )(x2d, w_bf)


def _group_norm_gate(o, g):
    mu = jnp.mean(o, axis=-1, keepdims=True)
    oc = o - mu
    var = jnp.mean(oc * oc, axis=-1, keepdims=True)
    return oc * lax.rsqrt(var + GN_EPS) * (g * jax.nn.sigmoid(g))


def _dot_nt(a, b):
    return lax.dot_general(a, b, (((1,), (1,)), ((), ())), preferred_element_type=F32)


def _layer_norm(x, g, b):
    mu = jnp.mean(x, axis=-1, keepdims=True)
    xc = x - mu
    var = jnp.mean(xc * xc, axis=-1, keepdims=True)
    return xc * lax.rsqrt(var + LN_EPS) * g + b


def _ret_prompt_kernel(cd_ref, dec_ref, qd_ref, kd_ref, q_ref, k_ref, v_ref, g_ref, o_ref, st_ref):
    c = pl.program_id(1)

    @pl.when(c == 0)
    def _():
        st_ref[...] = jnp.zeros_like(st_ref)

    heads = range(N_RET_HEADS)
    q = jnp.stack([q_ref[:, h * RET_DK:(h + 1) * RET_DK] for h in heads])
    k = jnp.stack([k_ref[:, h * RET_DK:(h + 1) * RET_DK] for h in heads]) * (RET_DK ** -0.5)
    v = jnp.stack([v_ref[:, h * RET_DV:(h + 1) * RET_DV] for h in heads]).astype(BF16)
    g = jnp.stack([g_ref[:, h * RET_DV:(h + 1) * RET_DV] for h in heads])
    state = st_ref[0]
    bmm = functools.partial(lax.dot_general, preferred_element_type=F32)
    scores = bmm(q.astype(BF16), k.astype(BF16), (((2,), (2,)), ((0,), (0,)))) * dec_ref[...]
    o = (bmm(scores.astype(BF16), v, (((2,), (1,)), ((0,), (0,))))
         + bmm((q * qd_ref[...]).astype(BF16), state.astype(BF16), (((2,), (1,)), ((0,), (0,)))))
    kt = jnp.swapaxes(k * kd_ref[...], 1, 2).astype(BF16)
    cd = jnp.stack([jnp.full((1, 1), cd_ref[h], F32) for h in heads])
    st_ref[0] = cd * state + bmm(kt, v, (((2,), (1,)), ((0,), (0,))))
    out = _group_norm_gate(o, g).astype(BF16)
    for h in heads:
        o_ref[:, h * RET_DV:(h + 1) * RET_DV] = out[h]


RS_BB = 32


def _ret_sample_kernel(cd_ref, dec_ref, qd_ref, kd_ref, q_ref, k_ref, v_ref, g_ref, st_ref,
                       o_ref, sto_ref):
    h = pl.program_id(1)
    q = q_ref[...]
    k = k_ref[...] * (RET_DK ** -0.5)
    v = v_ref[...].astype(BF16)
    scores = _dot_nt(q.astype(BF16), k.astype(BF16)) * dec_ref[0]
    o_intra = jnp.dot(scores.astype(BF16), v, preferred_element_type=F32)
    qq = (q * qd_ref[0]).astype(BF16)
    kt = (k * kd_ref[0]).T
    col_batch = lax.broadcasted_iota(jnp.int32, kt.shape, 1) // DEC_SEQ
    first_of_pair = lax.broadcasted_iota(jnp.int32, (8, RET_DV), 0) < DEC_SEQ
    cd = cd_ref[h]
    pieces = []
    for p in range(RS_BB // 2):
        q8 = qq[8 * p:8 * p + 8]
        s0 = st_ref[2 * p].astype(BF16)
        s1 = st_ref[2 * p + 1].astype(BF16)
        r0 = jnp.dot(q8, s0, preferred_element_type=F32)
        r1 = jnp.dot(q8, s1, preferred_element_type=F32)
        pieces.append(jnp.where(first_of_pair, r0, r1))
    for b in range(RS_BB):
        ktm = jnp.where(col_batch == b, kt, 0.0).astype(BF16)
        sto_ref[b] = cd * st_ref[b] + jnp.dot(ktm, v, preferred_element_type=F32)
    o = o_intra + jnp.concatenate(pieces, axis=0)
    o_ref[...] = _group_norm_gate(o, g_ref[...]).astype(BF16)


def _swa_prompt_kernel(sink_ref, bias_ref, q_ref, kvc_ref, kvp_ref, o_ref):
    kv_w = N_SWA_KV_HEADS * SWA_HEAD_DIM
    hd = SWA_HEAD_DIM
    qt = q_ref[...].T
    outs = []
    for kvh in range(N_SWA_KV_HEADS):
        lo = kvh * hd
        k_cat = jnp.concatenate([kvp_ref[:, lo:lo + hd], kvc_ref[:, lo:lo + hd]], axis=0).astype(BF16)
        v_cat = jnp.concatenate([kvp_ref[:, kv_w + lo:kv_w + lo + hd],
                                 kvc_ref[:, kv_w + lo:kv_w + lo + hd]], axis=0)
        vt = v_cat.T.astype(BF16)
        heads = range(kvh * SWA_GROUP, (kvh + 1) * SWA_GROUP)
        q8 = jnp.concatenate([qt[h * hd:(h + 1) * hd, :] for h in heads], axis=1)
        q8 = (q8 * (hd ** -0.5)).astype(BF16)
        s = jnp.dot(k_cat, q8, preferred_element_type=F32) + bias_ref[0, kvh]
        sink = sink_ref[kvh]
        m = jnp.maximum(jnp.max(s, axis=0, keepdims=True), sink)
        e = jnp.exp(s - m)
        den = jnp.sum(e, axis=0, keepdims=True) + jnp.exp(sink - m)
        p = (e * (1.0 / den)).astype(BF16)
        ot = jnp.dot(vt, p, preferred_element_type=F32)
        outs += [ot[:, g * WINDOW:(g + 1) * WINDOW] for g in range(SWA_GROUP)]
    o_ref[...] = jnp.concatenate(outs, axis=0).T.astype(BF16)


def _swa_prompt_tables(slopes, sinks):
    qi = jnp.arange(WINDOW)[:, None] + WINDOW
    ki = jnp.arange(2 * WINDOW)[None, :]
    dist = qi - ki
    in_window = (dist >= 0) & (dist < WINDOW)
    valid = jnp.stack([in_window & (ki >= WINDOW), in_window])
    pen = -(slopes[:, None, None] * dist.astype(F32)[None])
    bias = jnp.where(valid[:, None], pen[None], NEG_INF)
    bias = bias.reshape(2, N_SWA_KV_HEADS, SWA_GROUP, WINDOW, 2 * WINDOW).transpose(0, 1, 4, 2, 3)
    bias = bias.reshape(2, N_SWA_KV_HEADS, 2 * WINDOW, SWA_GROUP * WINDOW)
    sink_l = jnp.repeat(sinks.reshape(N_SWA_KV_HEADS, SWA_GROUP), WINDOW, axis=1)
    return sink_l.reshape(N_SWA_KV_HEADS, 1, SWA_GROUP * WINDOW), bias


def _mixer_prompt_kernel(cd_ref, dec_ref, qd_ref, kd_ref, sink_ref, bias_ref, qr_ref, kr_ref, vr_ref, gr_ref,
                         qs_ref, kvc_ref, kvp_ref, or_ref, st_ref, os_ref):
    _ret_prompt_kernel(cd_ref, dec_ref, qd_ref, kd_ref, qr_ref, kr_ref, vr_ref, gr_ref, or_ref, st_ref)
    _swa_prompt_kernel(sink_ref, bias_ref, qs_ref, kvc_ref, kvp_ref, os_ref)


def mixer_prompt(u, ret_tables, slopes, sinks):
    cd, dec, qd, kd = ret_tables
    sink_l, bias = _swa_prompt_tables(slopes, sinks)
    nb = SEQ // WINDOW
    lanes = SWA_GROUP * WINDOW
    kv_blk = COL_KVS // 512
    row = lambda b, c: b * nb + c
    whole3 = lambda b, c: (0, 0, 0)
    return pl.pallas_call(
        _mixer_prompt_kernel,
        grid=(BATCH, nb),
        in_specs=[
            pl.BlockSpec(memory_space=pltpu.SMEM),
            pl.BlockSpec((N_RET_HEADS, RET_CHUNK, RET_CHUNK), whole3),
            pl.BlockSpec((N_RET_HEADS, RET_CHUNK, 1), whole3),
            pl.BlockSpec((N_RET_HEADS, RET_CHUNK, 1), whole3),
            pl.BlockSpec((N_SWA_KV_HEADS, 1, lanes), whole3),
            pl.BlockSpec((1, N_SWA_KV_HEADS, 2 * WINDOW, lanes), lambda b, c: (jnp.minimum(c, 1), 0, 0, 0)),
            pl.BlockSpec((RET_CHUNK, 1024), lambda b, c: (row(b, c), COL_QR // 1024)),
            pl.BlockSpec((RET_CHUNK, 1024), lambda b, c: (row(b, c), COL_KR // 1024)),
            pl.BlockSpec((RET_CHUNK, 2048), lambda b, c: (row(b, c), COL_VR // 2048)),
            pl.BlockSpec((RET_CHUNK, 2048), lambda b, c: (row(b, c), COL_GR // 2048)),
            pl.BlockSpec((WINDOW, 2048), lambda b, c: (row(b, c), COL_QS // 2048)),
            pl.BlockSpec((WINDOW, 512), lambda b, c: (row(b, c), kv_blk)),
            pl.BlockSpec((WINDOW, 512), lambda b, c: (b * nb + jnp.maximum(c - 1, 0), kv_blk)),
        ],
        out_specs=[
            pl.BlockSpec((RET_CHUNK, 2048), lambda b, c: (row(b, c), 0)),
            pl.BlockSpec((1, N_RET_HEADS, RET_DK, RET_DV), lambda b, c: (b, 0, 0, 0)),
            pl.BlockSpec((WINDOW, 2048), lambda b, c: (row(b, c), 0)),
        ],
        out_shape=[
            jax.ShapeDtypeStruct((T_PROMPT, 2048), BF16),
            jax.ShapeDtypeStruct((BATCH, N_RET_HEADS, RET_DK, RET_DV), F32),
            jax.ShapeDtypeStruct((T_PROMPT, 2048), BF16),
        ],
        compiler_params=_params(("parallel", "arbitrary")),
        name="mixer_prompt",
    )(cd, dec, qd, kd, sink_l, bias, u, u, u, u, u, u, u)


SS_BB = RS_BB // N_RET_HEADS
SS_ROWS = DEC_SEQ * SWA_GROUP


def _bf16_round(x):
    return x.astype(BF16).astype(F32)


def _swa_sample_kernel(sink_ref, bias1_ref, bias2_ref, q_ref, kvn_ref, ck_ref, cv_ref,
                       o_ref, cko_ref, cvo_ref):
    kv_w = N_SWA_KV_HEADS * SWA_HEAD_DIM
    hd = SWA_HEAD_DIM
    scale = hd ** -0.5
    for kvh in range(N_SWA_KV_HEADS):
        lo = kvh * hd
        sink = sink_ref[kvh]
        q = q_ref[:, kvh]
        ck = ck_ref[:, :, lo:lo + hd].astype(BF16)
        cv = cv_ref[:, :, lo:lo + hd].astype(BF16)
        kn = _bf16_round(kvn_ref[:, :, lo:lo + hd])
        vn = _bf16_round(kvn_ref[:, :, kv_w + lo:kv_w + lo + hd])
        s1 = lax.dot_general(q.astype(BF16), ck, (((2,), (2,)), ((0,), (0,))),
                             preferred_element_type=F32) * scale + bias1_ref[kvh]
        qr = _bf16_round(q)
        s2 = [jnp.sum(qr * kn[:, t2:t2 + 1, :], axis=-1, keepdims=True) * scale + bias2_ref[kvh, t2]
              for t2 in range(DEC_SEQ)]
        m = jnp.maximum(jnp.max(s1, axis=-1, keepdims=True), sink)
        for st in s2:
            m = jnp.maximum(m, st)
        e1 = jnp.exp(s1 - m)
        e2 = [jnp.exp(st - m) for st in s2]
        den = jnp.sum(e1, axis=-1, keepdims=True) + jnp.exp(sink - m)
        for et in e2:
            den = den + et
        inv = 1.0 / den
        o = lax.dot_general((e1 * inv).astype(BF16), cv, (((2,), (1,)), ((0,), (0,))),
                            preferred_element_type=F32)
        for t2 in range(DEC_SEQ):
            o = o + _bf16_round(e2[t2] * inv) * vn[:, t2:t2 + 1, :]
        o_ref[:, kvh] = o
    keep = WINDOW - DEC_SEQ
    cko_ref[:, 0:keep, :] = ck_ref[:, DEC_SEQ:WINDOW, :]
    cko_ref[:, keep:WINDOW, :] = kvn_ref[:, :, 0:kv_w]
    cvo_ref[:, 0:keep, :] = cv_ref[:, DEC_SEQ:WINDOW, :]
    cvo_ref[:, keep:WINDOW, :] = kvn_ref[:, :, kv_w:2 * kv_w]


def _swa_sample_tables(slopes, sinks):
    g_slopes = jnp.tile(slopes.reshape(N_SWA_KV_HEADS, 1, SWA_GROUP), (1, DEC_SEQ, 1)).reshape(
        N_SWA_KV_HEADS, SS_ROWS)
    sink_rows = jnp.tile(sinks.reshape(N_SWA_KV_HEADS, 1, SWA_GROUP), (1, DEC_SEQ, 1)).reshape(
        N_SWA_KV_HEADS, SS_ROWS, 1)
    t_row = jnp.arange(SS_ROWS) // SWA_GROUP
    jcol = jnp.arange(WINDOW)
    dist1 = (t_row[:, None] + WINDOW - jcol[None, :])
    bias1 = jnp.where((dist1 < WINDOW)[None], -(g_slopes[:, :, None] * dist1.astype(F32)[None]), NEG_INF)
    t2 = jnp.arange(DEC_SEQ)
    dist2 = t_row[None, :] - t2[:, None]
    bias2 = jnp.where((dist2 >= 0)[None], -(g_slopes[:, None, :] * dist2.astype(F32)[None]), NEG_INF)
    return sink_rows, bias1, bias2[..., None]


def _mixer_sample_kernel(cd_ref, dec_ref, qd_ref, kd_ref, qr_ref, kr_ref, vr_ref, gr_ref, st_ref,
                         sink_ref, bias1_ref, bias2_ref, qs_ref, kvn_ref, ck_ref, cv_ref,
                         or_ref, sto_ref, os_ref, cko_ref, cvo_ref):
    _ret_sample_kernel(cd_ref, dec_ref, qd_ref, kd_ref, qr_ref, kr_ref, vr_ref, gr_ref, st_ref, or_ref, sto_ref)
    _swa_sample_kernel(sink_ref, bias1_ref, bias2_ref, qs_ref, kvn_ref, ck_ref, cv_ref, os_ref, cko_ref, cvo_ref)


def mixer_sample(u_s, state, ret_tables, q_t, kv_new, cache_k, cache_v, swa_tables):
    cd, dec, qd, kd = ret_tables
    sink_rows, bias1, bias2 = swa_tables
    kv_w = N_SWA_KV_HEADS * SWA_HEAD_DIM
    rows = RS_BB * DEC_SEQ
    per_head = lambda i, h: (h, 0, 0)
    whole3 = lambda i, h: (0, 0, 0)
    blk = lambda i, h: i * N_RET_HEADS + h
    st_spec = pl.BlockSpec((RS_BB, None, RET_DK, RET_DV), lambda i, h: (i, h, 0, 0))
    cache_spec = pl.BlockSpec((SS_BB, WINDOW, kv_w), lambda i, h: (blk(i, h), 0, 0))
    q_spec = pl.BlockSpec((SS_BB, N_SWA_KV_HEADS, SS_ROWS, SWA_HEAD_DIM), lambda i, h: (blk(i, h), 0, 0, 0))
    return pl.pallas_call(
        _mixer_sample_kernel,
        grid=(DEC_BATCH // RS_BB, N_RET_HEADS),
        in_specs=[
            pl.BlockSpec(memory_space=pltpu.SMEM),
            pl.BlockSpec((1, rows, rows), per_head),
            pl.BlockSpec((1, rows, 1), per_head),
            pl.BlockSpec((1, rows, 1), per_head),
            pl.BlockSpec((rows, RET_DK), lambda i, h: (i, COL_QR // RET_DK + h)),
            pl.BlockSpec((rows, RET_DK), lambda i, h: (i, COL_KR // RET_DK + h)),
            pl.BlockSpec((rows, RET_DV), lambda i, h: (i, COL_VR // RET_DV + h)),
            pl.BlockSpec((rows, RET_DV), lambda i, h: (i, COL_GR // RET_DV + h)),
            st_spec,
            pl.BlockSpec((N_SWA_KV_HEADS, SS_ROWS, 1), whole3),
            pl.BlockSpec((N_SWA_KV_HEADS, SS_ROWS, WINDOW), whole3),
            pl.BlockSpec((N_SWA_KV_HEADS, DEC_SEQ, SS_ROWS, 1), lambda i, h: (0, 0, 0, 0)),
            q_spec,
            pl.BlockSpec((SS_BB, DEC_SEQ, 2 * kv_w), lambda i, h: (blk(i, h), 0, 0)),
            cache_spec, cache_spec,
        ],
        out_specs=[pl.BlockSpec((rows, RET_DV), lambda i, h: (i, h)), st_spec, q_spec, cache_spec, cache_spec],
        out_shape=[
            jax.ShapeDtypeStruct((T_SAMPLE, 2048), BF16),
            jax.ShapeDtypeStruct((DEC_BATCH, N_RET_HEADS, RET_DK, RET_DV), F32),
            jax.ShapeDtypeStruct((DEC_BATCH, N_SWA_KV_HEADS, SS_ROWS, SWA_HEAD_DIM), F32),
            jax.ShapeDtypeStruct((DEC_BATCH, WINDOW, kv_w), F32),
            jax.ShapeDtypeStruct((DEC_BATCH, WINDOW, kv_w), F32),
        ],
        compiler_params=_params(("parallel", "parallel")),
        name="mixer_sample",
    )(cd, dec, qd, kd, u_s, u_s, u_s, u_s, state, sink_rows, bias1, bias2, q_t, kv_new, cache_k, cache_v)


def _merge_kernel(or_ref, os_ref, gr_ref, gs_ref, wr_ref, ws_ref, o_ref):
    a = jnp.dot(or_ref[...], wr_ref[...], preferred_element_type=F32)
    b = jnp.dot(os_ref[...], ws_ref[...], preferred_element_type=F32)
    merged = jax.nn.sigmoid(gr_ref[...]) * a + jax.nn.sigmoid(gs_ref[...]) * b
    o_ref[...] = merged.astype(BF16)


def merge_branches(o_r, o_s, u, w_ret_o_bf, w_swa_o_bf, tm):
    m = o_r.shape[0]
    resident = pl.BlockSpec((2048, D_MODEL), lambda i: (0, 0), pipeline_mode=pl.Buffered(1))
    return pl.pallas_call(
        _merge_kernel,
        grid=(m // tm,),
        in_specs=[
            pl.BlockSpec((tm, 2048), lambda i: (i, 0)),
            pl.BlockSpec((tm, 2048), lambda i: (i, 0)),
            pl.BlockSpec((pl.Element(tm), pl.Element(D_MODEL)), lambda i: (i * tm, COL_GATE_R)),
            pl.BlockSpec((pl.Element(tm), pl.Element(D_MODEL)), lambda i: (i * tm, COL_GATE_S)),
            resident, resident,
        ],
        out_specs=pl.BlockSpec((tm, D_MODEL), lambda i: (i, 0)),
        out_shape=jax.ShapeDtypeStruct((m, D_MODEL), BF16),
        compiler_params=_params(("parallel",)),
        name="merge_branches",
    )(o_r, o_s, u, u, w_ret_o_bf, w_swa_o_bf)


ROUTER_PAD = 128


def _post_mix_kernel(alpha, n_prompt_blocks, mp_ref, ms_ref, xp_ref, xs_ref, w_ref, g_ref, b_ref, wr_ref, br_ref,
                     h_ref, lg_ref):
    from_sample = pl.program_id(0) >= n_prompt_blocks
    merged = jnp.where(from_sample, ms_ref[...], mp_ref[...])
    x = jnp.where(from_sample, xs_ref[...], xp_ref[...])
    mix = jnp.dot(merged, w_ref[...], preferred_element_type=F32)
    h = _layer_norm(alpha * x + mix, g_ref[...], b_ref[...])
    h_ref[...] = h
    h_hi = h.astype(BF16)
    h_lo = (h - h_hi.astype(F32)).astype(BF16)
    hh = jnp.dot(h_hi, wr_ref[...], preferred_element_type=F32)
    lh = jnp.dot(h_lo, wr_ref[:, :ROUTER_PAD], preferred_element_type=F32)
    lg_ref[...] = (hh[:, :ROUTER_PAD] + hh[:, ROUTER_PAD:] + lh + br_ref[...]).T


def _router_split(w_router, b_router):
    w = jnp.pad(w_router, ((0, 0), (0, ROUTER_PAD - N_EXPERTS)))
    w_hi = w.astype(BF16)
    w_lo = (w - w_hi.astype(F32)).astype(BF16)
    b = jnp.pad(b_router, (0, ROUTER_PAD - N_EXPERTS)).reshape(1, ROUTER_PAD)
    return jnp.concatenate([w_hi, w_lo], axis=1), b


def post_mix(merged_p, merged_s, xp, xs, w_out_bf, ln_g, ln_b, w_router_pad, b_router_pad, alpha, tm):
    n_p = T_PROMPT // tm
    n_s = T_SAMPLE // tm
    prompt_row = lambda i: (jnp.minimum(i, n_p - 1), 0)
    sample_row = lambda i: (jnp.clip(i - n_p, 0, n_s - 1), 0)
    whole = lambda i: (0, 0)
    row = lambda i: (i, 0)
    return pl.pallas_call(
        functools.partial(_post_mix_kernel, alpha, n_p),
        grid=(n_p + n_s,),
        in_specs=[
            pl.BlockSpec((tm, 2048), prompt_row),
            pl.BlockSpec((tm, 2048), sample_row),
            pl.BlockSpec((tm, 2048), prompt_row),
            pl.BlockSpec((tm, 2048), sample_row),
            pl.BlockSpec((2048, 2048), whole),
            pl.BlockSpec((1, 2048), whole),
            pl.BlockSpec((1, 2048), whole),
            pl.BlockSpec((2048, 2 * ROUTER_PAD), whole),
            pl.BlockSpec((1, ROUTER_PAD), whole),
        ],
        out_specs=[pl.BlockSpec((tm, 2048), row), pl.BlockSpec((ROUTER_PAD, tm), lambda i: (0, i))],
        out_shape=[jax.ShapeDtypeStruct((T_ALL, D_MODEL), F32),
                   jax.ShapeDtypeStruct((ROUTER_PAD, T_ALL), F32)],
        compiler_params=_params(("parallel",)),
        name="post_mix",
    )(merged_p, merged_s, xp, xs, w_out_bf, ln_g, ln_b, w_router_pad, b_router_pad)


def _moe_kernel(ce_ref, cs_ref, cn_ref, nu_ref, tail_ref, x_hbm, wg_ref, wu_ref, wd_ref, bg_ref, bu_ref, bd_ref,
                y_hbm, x_f32, x_bf, y_acc, zero_buf, x_sem, y_sem, z_sem):
    c = pl.program_id(0)
    j = pl.program_id(1)
    nblk = cn_ref[c]
    slot = c % 2
    y_cur = y_acc.at[slot]

    def sub_rows(r):
        return pl.ds(pl.multiple_of(r * MOE_SUB, MOE_SUB), MOE_SUB)

    def x_copy(chunk):
        start = pl.multiple_of(cs_ref[chunk] * MOE_SUB, MOE_SUB)
        return pltpu.make_async_copy(x_hbm.at[pl.ds(start, MOE_R), :], x_f32, x_sem)

    def y_copy(r, sub):
        return pltpu.make_async_copy(y_cur.at[sub_rows(r), :], y_hbm.at[sub_rows(sub), :], y_sem.at[slot])

    def zero_copy(sub):
        return pltpu.make_async_copy(zero_buf, y_hbm.at[sub_rows(sub), :], z_sem)

    @pl.when(j == 0)
    def _():
        def drain(i, carry):
            y_copy(0, 0).wait()
            return carry

        lax.fori_loop(0, cn_ref[jnp.maximum(c - 2, 0)] * (c >= 2).astype(jnp.int32), drain, 0)

    @pl.when((nblk > 0) & (j == 0))
    def _():
        @pl.when(c == 0)
        def _():
            x_copy(0).start()

        x_copy(c).wait()

        def stage(r, carry):
            rows = sub_rows(r)
            x_bf[rows, :] = x_f32[rows, :].astype(BF16)
            y_cur[rows, :] = jnp.broadcast_to(bd_ref[0], (MOE_SUB, D_MODEL))
            return carry

        lax.fori_loop(0, nblk, stage, 0)

        @pl.when(c + 1 < nu_ref[0])
        def _():
            x_copy(c + 1).start()

    @pl.when(nblk > 0)
    def _():
        def rows_block(rows):
            xr = x_bf[rows, :]
            g = jnp.dot(xr, wg_ref[0], preferred_element_type=F32) + bg_ref[0]
            up = jnp.dot(xr, wu_ref[0], preferred_element_type=F32) + bu_ref[0]
            g = jnp.minimum(g, SWIGLU_LIMIT)
            up = jnp.clip(up, -SWIGLU_LIMIT, SWIGLU_LIMIT)
            glu = g * jax.nn.sigmoid(SWIGLU_ALPHA * g)
            act = ((up + 1.0) * glu).astype(BF16)
            y_cur[rows, :] += jnp.dot(act, wd_ref[0], preferred_element_type=F32)

        for n_static in MOE_STATIC_SIZES:
            @pl.when(nblk == n_static)
            def _(n_static=n_static):
                rows_block(pl.ds(0, n_static * MOE_SUB))

        @pl.when(nblk < MOE_STATIC_SIZES[0])
        def _():
            def one(r, carry):
                rows_block(sub_rows(r))
                return carry

            lax.fori_loop(0, nblk, one, 0)

    @pl.when((nblk > 0) & (j == MOE_NJ - 1))
    def _():
        def write_back(r, carry):
            y_copy(r, cs_ref[c] + r).start()
            return carry

        lax.fori_loop(0, nblk, write_back, 0)

    @pl.when((nblk == 0) & (j == 0))
    def _():
        first = tail_ref[0] + MOE_CHUNK_SUBS * (c - nu_ref[0])
        count = jnp.clip(MOE_XROWS // MOE_SUB - first, 0, MOE_CHUNK_SUBS)

        @pl.when(count > 0)
        def _():
            zero_buf[...] = jnp.zeros_like(zero_buf)

            def start(i, carry):
                zero_copy(first + i).start()
                return carry

            def wait(i, carry):
                zero_copy(first + i).wait()
                return carry

            lax.fori_loop(0, count, start, 0)
            lax.fori_loop(0, count, wait, 0)


def moe_experts(x_sorted, chunk_tables, w_gate_up, b_gate_up, w_down, b_down):
    nj = MOE_NJ

    def used_j(c, j, nu):
        return jnp.where(c < nu[0], j, nj - 1)

    return pl.pallas_call(
        _moe_kernel,
        grid_spec=pltpu.PrefetchScalarGridSpec(
            num_scalar_prefetch=5,
            grid=(MOE_NC, nj),
            in_specs=[
                pl.BlockSpec(memory_space=pl.ANY),
                pl.BlockSpec((1, D_MODEL, MOE_TF),
                             lambda c, j, ce, cs, cn, nu, tl: (ce[c], 0, used_j(c, j, nu))),
                pl.BlockSpec((1, D_MODEL, MOE_TF),
                             lambda c, j, ce, cs, cn, nu, tl: (ce[c], 0, nj + used_j(c, j, nu))),
                pl.BlockSpec((1, MOE_TF, D_MODEL),
                             lambda c, j, ce, cs, cn, nu, tl: (ce[c], used_j(c, j, nu), 0)),
                pl.BlockSpec((1, 1, MOE_TF),
                             lambda c, j, ce, cs, cn, nu, tl: (ce[c], 0, used_j(c, j, nu))),
                pl.BlockSpec((1, 1, MOE_TF),
                             lambda c, j, ce, cs, cn, nu, tl: (ce[c], 0, nj + used_j(c, j, nu))),
                pl.BlockSpec((1, 1, D_MODEL), lambda c, j, ce, cs, cn, nu, tl: (ce[c], 0, 0)),
            ],
            out_specs=pl.BlockSpec(memory_space=pl.ANY),
            scratch_shapes=[
                pltpu.VMEM((MOE_R, D_MODEL), F32),
                pltpu.VMEM((MOE_R, D_MODEL), BF16),
                pltpu.VMEM((2, MOE_R, D_MODEL), F32),
                pltpu.VMEM((MOE_SUB, D_MODEL), F32),
                pltpu.SemaphoreType.DMA(()),
                pltpu.SemaphoreType.DMA((2,)),
                pltpu.SemaphoreType.DMA(()),
            ],
        ),
        out_shape=jax.ShapeDtypeStruct((MOE_XROWS, D_MODEL), F32),
        compiler_params=pltpu.CompilerParams(dimension_semantics=("arbitrary", "arbitrary"),
                                             vmem_limit_bytes=MOE_VMEM_LIMIT),
        name="moe_experts",
    )(*chunk_tables, x_sorted, w_gate_up, w_gate_up, w_down, b_gate_up, b_gate_up, b_down)


ROUTE_TM = 512


def _route_kernel(lg_ref, tri_ref, e_ref, rank_ref, gate_ref, cnt_ref, seen):
    @pl.when(pl.program_id(0) == 0)
    def _():
        seen[...] = jnp.zeros_like(seen)

    work = lg_ref[0:N_EXPERTS, :]
    eidx = lax.broadcasted_iota(jnp.int32, work.shape, 0)
    vals, idxs, hots = [], [], []
    for _ in range(TOP_K):
        m = jnp.max(work, axis=0, keepdims=True)
        idx = jnp.min(jnp.where(work == m, eidx, N_EXPERTS), axis=0, keepdims=True)
        hot = eidx == idx
        vals.append(m)
        idxs.append(idx)
        hots.append(hot)
        work = jnp.where(hot, -jnp.inf, work)
    ex = [jnp.exp(v - vals[0]) for v in vals]
    inv = 1.0 / (ex[0] + ex[1] + ex[2] + ex[3])
    picked = jnp.zeros(work.shape, F32)
    for hot in hots:
        picked = picked + hot.astype(F32)
    before = jnp.dot(picked.astype(BF16), tri_ref[...], preferred_element_type=F32) + seen[...]
    ranks = [jnp.sum(jnp.where(hot, before, 0.0), axis=0, keepdims=True) for hot in hots]
    seen[...] += jnp.sum(picked, axis=1, keepdims=True)
    e_ref[...] = jnp.concatenate(idxs, axis=0)
    rank_ref[...] = jnp.concatenate(ranks, axis=0).astype(jnp.int32)
    gate_ref[...] = jnp.concatenate([e * inv for e in ex], axis=0)
    cnt_ref[...] = seen[...]


def route_topk(logits_t):
    tm = ROUTE_TM
    tri = (jnp.arange(tm)[:, None] < jnp.arange(tm)[None, :]).astype(BF16)
    pick = pl.BlockSpec((TOP_K, tm), lambda i: (0, i))
    return pl.pallas_call(
        _route_kernel,
        grid=(T_ALL // tm,),
        in_specs=[pl.BlockSpec((ROUTER_PAD, tm), lambda i: (0, i)),
                  pl.BlockSpec((tm, tm), lambda i: (0, 0))],
        out_specs=[pick, pick, pick, pl.BlockSpec((N_EXPERTS, 1), lambda i: (0, 0))],
        out_shape=[jax.ShapeDtypeStruct((TOP_K, T_ALL), jnp.int32),
                   jax.ShapeDtypeStruct((TOP_K, T_ALL), jnp.int32),
                   jax.ShapeDtypeStruct((TOP_K, T_ALL), F32),
                   jax.ShapeDtypeStruct((N_EXPERTS, 1), F32)],
        scratch_shapes=[pltpu.VMEM((N_EXPERTS, 1), F32)],
        compiler_params=_params(("arbitrary",)),
        name="route_topk",
    )(logits_t, tri)


def route(logits_t):
    e_kt, rank, gate_kt, counts_f = route_topk(logits_t)
    counts = counts_f.reshape(N_EXPERTS).astype(jnp.int32)
    nsub = (counts + MOE_SUB - 1) // MOE_SUB
    sub_base = jnp.cumsum(nsub) - nsub
    nch = (counts + MOE_R - 1) // MOE_R
    ch_end = jnp.cumsum(nch)
    ch_base = ch_end - nch
    n_used = ch_end[-1]
    is_e = e_kt[..., None] == jnp.arange(N_EXPERTS, dtype=jnp.int32)
    pos = MOE_SUB * jnp.sum(jnp.where(is_e, sub_base, 0), axis=-1) + rank
    tok = jnp.broadcast_to(jnp.arange(T_ALL, dtype=jnp.int32)[None, :], pos.shape)
    pad_i = jnp.arange(MOE_SUB - 1, dtype=jnp.int32)
    pad_pos = (MOE_SUB * sub_base + counts)[:, None] + pad_i[None, :]
    pad_pos = jnp.where(pad_i[None, :] < (MOE_SUB * nsub - counts)[:, None], pad_pos, MOE_XROWS)
    tail_pos = MOE_SUB * jnp.sum(nsub) + jnp.arange(MOE_XROWS - TK_ALL, dtype=jnp.int32)
    tail_pos = jnp.minimum(tail_pos, MOE_XROWS)
    keys = jnp.concatenate([pos.reshape(-1), pad_pos.reshape(-1), tail_pos]).astype(jnp.int32)
    vals = jnp.concatenate([tok.reshape(-1), jnp.zeros((keys.shape[0] - TK_ALL,), jnp.int32)])
    row_tok = lax.sort_key_val(keys, vals)[1][:MOE_XROWS]
    c = jnp.arange(MOE_NC, dtype=jnp.int32)
    cc = jnp.minimum(c, n_used - 1)
    ce = jnp.minimum(jnp.sum(cc[:, None] >= ch_end[None, :], axis=1), N_EXPERTS - 1).astype(jnp.int32)
    kk = cc - ch_base[ce]
    cs = (sub_base[ce] + MOE_CHUNK_SUBS * kk).astype(jnp.int32)
    cn = jnp.where(c < n_used, jnp.minimum(MOE_CHUNK_SUBS, nsub[ce] - MOE_CHUNK_SUBS * kk), 0).astype(jnp.int32)
    tables = (ce, cs, cn, n_used.reshape(1).astype(jnp.int32), jnp.sum(nsub).reshape(1).astype(jnp.int32))
    return gate_kt.T, row_tok, pos, tables


def _combine_kernel(alpha, h_ref, yg_ref, gate_ref, g_ref, b_ref, o_ref):
    gate = gate_ref[...]
    f = yg_ref[0] * gate[:, 0:1]
    for k in range(1, TOP_K):
        f = f + yg_ref[k] * gate[:, k:k + 1]
    o_ref[...] = _layer_norm(alpha * h_ref[...] + f, g_ref[...], b_ref[...])


def combine(h, yg, gate, ln_g, ln_b, alpha, row_off, m, tm):
    blk_off = row_off // tm
    whole = lambda i: (0, 0)
    return pl.pallas_call(
        functools.partial(_combine_kernel, alpha),
        grid=(m // tm,),
        in_specs=[
            pl.BlockSpec((tm, D_MODEL), lambda i: (blk_off + i, 0)),
            pl.BlockSpec((TOP_K, tm, D_MODEL), lambda i: (0, blk_off + i, 0)),
            pl.BlockSpec((tm, TOP_K), lambda i: (blk_off + i, 0)),
            pl.BlockSpec((1, D_MODEL), whole),
            pl.BlockSpec((1, D_MODEL), whole),
        ],
        out_specs=pl.BlockSpec((tm, D_MODEL), lambda i: (i, 0)),
        out_shape=jax.ShapeDtypeStruct((m, D_MODEL), F32),
        compiler_params=_params(("parallel",)),
        name="combine",
    )(h, yg, gate, ln_g, ln_b)


def _retention_tables(chunk, group):
    log_gamma = jnp.log(1.0 - jnp.exp2(-5.0 - jnp.arange(N_RET_HEADS, dtype=F32)))
    n = chunk * group
    pos = jnp.arange(n) % chunk
    seq = jnp.arange(n) // chunk
    diff = (pos[:, None] - pos[None, :]).astype(F32)
    same = seq[:, None] == seq[None, :]
    lg = log_gamma[:, None, None]
    dec = jnp.where(same[None] & (diff >= 0)[None], jnp.exp(lg * jnp.maximum(diff, 0.0)[None]), 0.0)
    posf = pos.astype(F32)
    qd = jnp.exp(log_gamma[:, None] * (posf + 1.0))[..., None]
    kd = jnp.exp(log_gamma[:, None] * (chunk - 1.0 - posf))[..., None]
    cd = jnp.exp(log_gamma * chunk)
    return cd, dec, qd, kd


def _alibi_slopes():
    h = jnp.arange(1, N_SWA_HEADS + 1, dtype=F32)
    return jnp.exp2(-8.0 * h / N_SWA_HEADS)


def kernel(x_prompt, x_sample, state_ret, cache_swa_k, cache_swa_v, w_in, w_ret_o, w_swa_o, w_out, swa_sinks,
           ln1_g, ln1_b, w_router, b_router, w_gate_up, b_gate_up, w_down, b_down, ln2_g, ln2_b):
    alpha = (2.0 * DEPTH) ** 0.25
    kv_w = N_SWA_KV_HEADS * SWA_HEAD_DIM
    w = w_in[0]
    w_in_bf = w.astype(BF16)
    w_ret_o_bf = w_ret_o[0].astype(BF16)
    w_swa_o_bf = w_swa_o[0].astype(BF16)
    w_out_bf = w_out[0].astype(BF16)
    w_router_pad, b_router_pad = _router_split(w_router[0], b_router[0])
    ln1g, ln1b = ln1_g[0].reshape(1, D_MODEL), ln1_b[0].reshape(1, D_MODEL)
    ln2g, ln2b = ln2_g[0].reshape(1, D_MODEL), ln2_b[0].reshape(1, D_MODEL)
    slopes = _alibi_slopes()
    sinks = swa_sinks[0].astype(F32)

    xp = x_prompt.reshape(T_PROMPT, D_MODEL)
    xs = x_sample.reshape(T_SAMPLE, D_MODEL)

    u_p = in_proj(xp, w_in_bf, 1024, 1280)
    o_r_p, state_p, o_s_p = mixer_prompt(u_p, _retention_tables(RET_CHUNK, 1), slopes, sinks)
    kv_p = u_p.reshape(BATCH, SEQ, D_IN)[:, SEQ - WINDOW:, COL_KVS:COL_KVS + 2 * kv_w]
    k_cache_p = kv_p[..., :kv_w].reshape(1, BATCH, WINDOW, N_SWA_KV_HEADS, SWA_HEAD_DIM)
    v_cache_p = kv_p[..., kv_w:].reshape(1, BATCH, WINDOW, N_SWA_KV_HEADS, SWA_HEAD_DIM)
    merged_p = merge_branches(o_r_p, o_s_p, u_p, w_ret_o_bf, w_swa_o_bf, 512)

    u_s = in_proj(xs, w_in_bf, 512, 1280)
    q_t = (u_s[:, COL_QS:COL_QS + 2048]
           .reshape(DEC_BATCH, DEC_SEQ, N_SWA_KV_HEADS, SWA_GROUP, SWA_HEAD_DIM)
           .transpose(0, 2, 1, 3, 4).reshape(DEC_BATCH, N_SWA_KV_HEADS, SS_ROWS, SWA_HEAD_DIM))
    kv_new = u_s[:, COL_KVS:COL_KVS + 2 * kv_w].reshape(DEC_BATCH, DEC_SEQ, 2 * kv_w)
    o_r_s, state_s, o_s_t, k_cache_s, v_cache_s = mixer_sample(
        u_s, state_ret[0], _retention_tables(DEC_SEQ, RS_BB), q_t, kv_new,
        cache_swa_k[0].reshape(DEC_BATCH, WINDOW, kv_w), cache_swa_v[0].reshape(DEC_BATCH, WINDOW, kv_w),
        _swa_sample_tables(slopes, sinks))
    o_s_s = (o_s_t.reshape(DEC_BATCH, N_SWA_KV_HEADS, DEC_SEQ, SWA_GROUP, SWA_HEAD_DIM)
             .transpose(0, 2, 1, 3, 4).reshape(T_SAMPLE, 2048).astype(BF16))
    merged_s = merge_branches(o_r_s, o_s_s, u_s, w_ret_o_bf, w_swa_o_bf, 512)

    h, logits = post_mix(merged_p, merged_s, xp, xs, w_out_bf, ln1g, ln1b, w_router_pad, b_router_pad,
                         alpha, 512)

    gate, row_tok, pos, tables = route(logits)
    x_sorted = h.at[row_tok].get(mode="promise_in_bounds")
    y_rows = moe_experts(x_sorted, tables, w_gate_up[0], b_gate_up[0].reshape(N_EXPERTS, 1, 2 * D_FF),
                         w_down[0], b_down[0].reshape(N_EXPERTS, 1, D_MODEL))
    yg = y_rows.at[pos].get(mode="promise_in_bounds")
    y_p = combine(h, yg, gate, ln2g, ln2b, alpha, 0, T_PROMPT, 256)
    y_s = combine(h, yg, gate, ln2g, ln2b, alpha, T_PROMPT, T_SAMPLE, 256)

    return (y_p.reshape(BATCH, SEQ, D_MODEL), y_s.reshape(DEC_BATCH, DEC_SEQ, D_MODEL),
            state_p[None], k_cache_p, v_cache_p,
            state_s[None], k_cache_s.reshape(1, DEC_BATCH, WINDOW, N_SWA_KV_HEADS, SWA_HEAD_DIM),
            v_cache_s.reshape(1, DEC_BATCH, WINDOW, N_SWA_KV_HEADS, SWA_HEAD_DIM))
```

```python
import functools

import jax
import jax.numpy as jnp
from jax import lax
from jax.experimental import pallas as pl
from jax.experimental.pallas import tpu as pltpu

F32 = jnp.float32
BF16 = jnp.bfloat16

D_MODEL = 2048
BATCH = 4
SEQ = 2048
DEC_BATCH = 128
DEC_SEQ = 4
N_RET_HEADS = 8
RET_DK = 128
RET_DV = 256
RET_CHUNK = 128
N_SWA_HEADS = 32
N_SWA_KV_HEADS = 4
SWA_GROUP = 8
SWA_HEAD_DIM = 64
WINDOW = 128
N_EXPERTS = 32
TOP_K = 4
D_FF = 2048
SWIGLU_LIMIT = 7.0
SWIGLU_ALPHA = 1.702
LN_EPS = 1e-5
GN_EPS = 1e-6
NEG_INF = -1e30
DEPTH = 1

T_PROMPT = BATCH * SEQ
T_SAMPLE = DEC_BATCH * DEC_SEQ
T_ALL = T_PROMPT + T_SAMPLE
TK_ALL = T_ALL * TOP_K
D_IN = 12800

COL_QR, COL_KR, COL_VR, COL_GR, COL_QS, COL_KVS, COL_GATE_R, COL_GATE_S = (
    0, 1024, 2048, 4096, 6144, 8192, 8704, 10752)

MOE_SUB = 128
MOE_CHUNK_SUBS = 11
MOE_STATIC_SIZES = (7, 8, 9, 10, 11)
MOE_R = MOE_SUB * MOE_CHUNK_SUBS
MOE_NSUB_MAX = TK_ALL // MOE_SUB + N_EXPERTS
MOE_XROWS = (MOE_NSUB_MAX + MOE_CHUNK_SUBS - 1) * MOE_SUB
MOE_NC = TK_ALL // MOE_R + N_EXPERTS + 2
MOE_TF = 256
MOE_NJ = D_FF // MOE_TF
MOE_VMEM_LIMIT = 60 * 1024 * 1024

VMEM_LIMIT = 56 * 1024 * 1024


def _params(sem):
    return pltpu.CompilerParams(dimension_semantics=sem, vmem_limit_bytes=VMEM_LIMIT)


def _in_proj_kernel(x_ref, w_ref, o_ref):
    o_ref[...] = jnp.dot(x_ref[...].astype(BF16), w_ref[...], preferred_element_type=F32)


def in_proj(x2d, w_bf, tm, tn):
    m, k = x2d.shape
    n = w_bf.shape[1]
    return pl.pallas_call(
        _in_proj_kernel,
        grid=(m // tm, n // tn),
        in_specs=[pl.BlockSpec((tm, k), lambda i, j: (i, 0)),
                  pl.BlockSpec((k, tn), lambda i, j: (0, j))],
        out_specs=pl.BlockSpec((tm, tn), lambda i, j: (i, j)),
        out_shape=jax.ShapeDtypeStruct((m, n), F32),
        compiler_params=_params(("parallel", "parallel")),
        name="in_proj",
    )(x2d, w_bf)


def _group_norm_gate(o, g):
    mu = jnp.mean(o, axis=-1, keepdims=True)
    oc = o - mu
    var = jnp.mean(oc * oc, axis=-1, keepdims=True)
    return oc * lax.rsqrt(var + GN_EPS) * (g * jax.nn.sigmoid(g))


def _dot_nt(a, b):
    return lax.dot_general(a, b, (((1,), (1,)), ((), ())), preferred_element_type=F32)


def _layer_norm(x, g, b):
    mu = jnp.mean(x, axis=-1, keepdims=True)
    xc = x - mu
    var = jnp.mean(xc * xc, axis=-1, keepdims=True)
    return xc * lax.rsqrt(var + LN_EPS) * g + b


def _ret_prompt_kernel(cd_ref, dec_ref, qd_ref, kd_ref, q_ref, k_ref, v_ref, g_ref, o_ref, st_ref):
    c = pl.program_id(1)

    @pl.when(c == 0)
    def _():
        st_ref[...] = jnp.zeros_like(st_ref)

    heads = range(N_RET_HEADS)
    q = jnp.stack([q_ref[:, h * RET_DK:(h + 1) * RET_DK] for h in heads])
    k = jnp.stack([k_ref[:, h * RET_DK:(h + 1) * RET_DK] for h in heads]) * (RET_DK ** -0.5)
    v = jnp.stack([v_ref[:, h * RET_DV:(h + 1) * RET_DV] for h in heads]).astype(BF16)
    g = jnp.stack([g_ref[:, h * RET_DV:(h + 1) * RET_DV] for h in heads])
    state = st_ref[0]
    bmm = functools.partial(lax.dot_general, preferred_element_type=F32)
    scores = bmm(q.astype(BF16), k.astype(BF16), (((2,), (2,)), ((0,), (0,)))) * dec_ref[...]
    o = (bmm(scores.astype(BF16), v, (((2,), (1,)), ((0,), (0,))))
         + bmm((q * qd_ref[...]).astype(BF16), state.astype(BF16), (((2,), (1,)), ((0,), (0,)))))
    kt = jnp.swapaxes(k * kd_ref[...], 1, 2).astype(BF16)
    cd = jnp.stack([jnp.full((1, 1), cd_ref[h], F32) for h in heads])
    st_ref[0] = cd * state + bmm(kt, v, (((2,), (1,)), ((0,), (0,))))
    out = _group_norm_gate(o, g).astype(BF16)
    for h in heads:
        o_ref[:, h * RET_DV:(h + 1) * RET_DV] = out[h]


RS_BB = 32


def _ret_sample_kernel(cd_ref, dec_ref, qd_ref, kd_ref, q_ref, k_ref, v_ref, g_ref, st_ref,
                       o_ref, sto_ref):
    h = pl.program_id(1)
    q = q_ref[...]
    k = k_ref[...] * (RET_DK ** -0.5)
    v = v_ref[...].astype(BF16)
    scores = _dot_nt(q.astype(BF16), k.astype(BF16)) * dec_ref[0]
    o_intra = jnp.dot(scores.astype(BF16), v, preferred_element_type=F32)
    qq = (q * qd_ref[0]).astype(BF16)
    kt = (k * kd_ref[0]).T
    col_batch = lax.broadcasted_iota(jnp.int32, kt.shape, 1) // DEC_SEQ
    first_of_pair = lax.broadcasted_iota(jnp.int32, (8, RET_DV), 0) < DEC_SEQ
    cd = cd_ref[h]
    pieces = []
    for p in range(RS_BB // 2):
        q8 = qq[8 * p:8 * p + 8]
        s0 = st_ref[2 * p].astype(BF16)
        s1 = st_ref[2 * p + 1].astype(BF16)
        r0 = jnp.dot(q8, s0, preferred_element_type=F32)
        r1 = jnp.dot(q8, s1, preferred_element_type=F32)
        pieces.append(jnp.where(first_of_pair, r0, r1))
    for b in range(RS_BB):
        ktm = jnp.where(col_batch == b, kt, 0.0).astype(BF16)
        sto_ref[b] = cd * st_ref[b] + jnp.dot(ktm, v, preferred_element_type=F32)
    o = o_intra + jnp.concatenate(pieces, axis=0)
    o_ref[...] = _group_norm_gate(o, g_ref[...]).astype(BF16)


def _swa_prompt_kernel(sink_ref, bias_ref, q_ref, kvc_ref, kvp_ref, o_ref):
    kv_w = N_SWA_KV_HEADS * SWA_HEAD_DIM
    hd = SWA_HEAD_DIM
    qt = q_ref[...].T
    outs = []
    for kvh in range(N_SWA_KV_HEADS):
        lo = kvh * hd
        k_cat = jnp.concatenate([kvp_ref[:, lo:lo + hd], kvc_ref[:, lo:lo + hd]], axis=0).astype(BF16)
        v_cat = jnp.concatenate([kvp_ref[:, kv_w + lo:kv_w + lo + hd],
                                 kvc_ref[:, kv_w + lo:kv_w + lo + hd]], axis=0)
        vt = v_cat.T.astype(BF16)
        heads = range(kvh * SWA_GROUP, (kvh + 1) * SWA_GROUP)
        q8 = jnp.concatenate([qt[h * hd:(h + 1) * hd, :] for h in heads], axis=1)
        q8 = (q8 * (hd ** -0.5)).astype(BF16)
        s = jnp.dot(k_cat, q8, preferred_element_type=F32) + bias_ref[0, kvh]
        sink = sink_ref[kvh]
        m = jnp.maximum(jnp.max(s, axis=0, keepdims=True), sink)
        e = jnp.exp(s - m)
        den = jnp.sum(e, axis=0, keepdims=True) + jnp.exp(sink - m)
        p = (e * (1.0 / den)).astype(BF16)
        ot = jnp.dot(vt, p, preferred_element_type=F32)
        outs += [ot[:, g * WINDOW:(g + 1) * WINDOW] for g in range(SWA_GROUP)]
    o_ref[...] = jnp.concatenate(outs, axis=0).T.astype(BF16)


def _swa_prompt_tables(slopes, sinks):
    qi = jnp.arange(WINDOW)[:, None] + WINDOW
    ki = jnp.arange(2 * WINDOW)[None, :]
    dist = qi - ki
    in_window = (dist >= 0) & (dist < WINDOW)
    valid = jnp.stack([in_window & (ki >= WINDOW), in_window])
    pen = -(slopes[:, None, None] * dist.astype(F32)[None])
    bias = jnp.where(valid[:, None], pen[None], NEG_INF)
    bias = bias.reshape(2, N_SWA_KV_HEADS, SWA_GROUP, WINDOW, 2 * WINDOW).transpose(0, 1, 4, 2, 3)
    bias = bias.reshape(2, N_SWA_KV_HEADS, 2 * WINDOW, SWA_GROUP * WINDOW)
    sink_l = jnp.repeat(sinks.reshape(N_SWA_KV_HEADS, SWA_GROUP), WINDOW, axis=1)
    return sink_l.reshape(N_SWA_KV_HEADS, 1, SWA_GROUP * WINDOW), bias


def _mixer_prompt_kernel(cd_ref, dec_ref, qd_ref, kd_ref, sink_ref, bias_ref, qr_ref, kr_ref, vr_ref, gr_ref,
                         qs_ref, kvc_ref, kvp_ref, or_ref, st_ref, os_ref):
    _ret_prompt_kernel(cd_ref, dec_ref, qd_ref, kd_ref, qr_ref, kr_ref, vr_ref, gr_ref, or_ref, st_ref)
    _swa_prompt_kernel(sink_ref, bias_ref, qs_ref, kvc_ref, kvp_ref, os_ref)


def mixer_prompt(u, ret_tables, slopes, sinks):
    cd, dec, qd, kd = ret_tables
    sink_l, bias = _swa_prompt_tables(slopes, sinks)
    nb = SEQ // WINDOW
    lanes = SWA_GROUP * WINDOW
    kv_blk = COL_KVS // 512
    row = lambda b, c: b * nb + c
    whole3 = lambda b, c: (0, 0, 0)
    return pl.pallas_call(
        _mixer_prompt_kernel,
        grid=(BATCH, nb),
        in_specs=[
            pl.BlockSpec(memory_space=pltpu.SMEM),
            pl.BlockSpec((N_RET_HEADS, RET_CHUNK, RET_CHUNK), whole3),
            pl.BlockSpec((N_RET_HEADS, RET_CHUNK, 1), whole3),
            pl.BlockSpec((N_RET_HEADS, RET_CHUNK, 1), whole3),
            pl.BlockSpec((N_SWA_KV_HEADS, 1, lanes), whole3),
            pl.BlockSpec((1, N_SWA_KV_HEADS, 2 * WINDOW, lanes), lambda b, c: (jnp.minimum(c, 1), 0, 0, 0)),
            pl.BlockSpec((RET_CHUNK, 1024), lambda b, c: (row(b, c), COL_QR // 1024)),
            pl.BlockSpec((RET_CHUNK, 1024), lambda b, c: (row(b, c), COL_KR // 1024)),
            pl.BlockSpec((RET_CHUNK, 2048), lambda b, c: (row(b, c), COL_VR // 2048)),
            pl.BlockSpec((RET_CHUNK, 2048), lambda b, c: (row(b, c), COL_GR // 2048)),
            pl.BlockSpec((WINDOW, 2048), lambda b, c: (row(b, c), COL_QS // 2048)),
            pl.BlockSpec((WINDOW, 512), lambda b, c: (row(b, c), kv_blk)),
            pl.BlockSpec((WINDOW, 512), lambda b, c: (b * nb + jnp.maximum(c - 1, 0), kv_blk)),
        ],
        out_specs=[
            pl.BlockSpec((RET_CHUNK, 2048), lambda b, c: (row(b, c), 0)),
            pl.BlockSpec((1, N_RET_HEADS, RET_DK, RET_DV), lambda b, c: (b, 0, 0, 0)),
            pl.BlockSpec((WINDOW, 2048), lambda b, c: (row(b, c), 0)),
        ],
        out_shape=[
            jax.ShapeDtypeStruct((T_PROMPT, 2048), BF16),
            jax.ShapeDtypeStruct((BATCH, N_RET_HEADS, RET_DK, RET_DV), F32),
            jax.ShapeDtypeStruct((T_PROMPT, 2048), BF16),
        ],
        compiler_params=_params(("parallel", "arbitrary")),
        name="mixer_prompt",
    )(cd, dec, qd, kd, sink_l, bias, u, u, u, u, u, u, u)


SS_BB = RS_BB // N_RET_HEADS
SS_ROWS = DEC_SEQ * SWA_GROUP


def _bf16_round(x):
    return x.astype(BF16).astype(F32)


def _swa_sample_kernel(sink_ref, bias1_ref, bias2_ref, q_ref, kvn_ref, ck_ref, cv_ref,
                       o_ref, cko_ref, cvo_ref):
    kv_w = N_SWA_KV_HEADS * SWA_HEAD_DIM
    hd = SWA_HEAD_DIM
    scale = hd ** -0.5
    bb = q_ref.shape[0]
    heads = range(N_SWA_KV_HEADS)

    def per_pair(f):
        return jnp.concatenate([f(kvh) for kvh in heads], axis=0)

    def per_head_table(t):
        return per_pair(lambda kvh: jnp.broadcast_to(t[kvh][None], (bb,) + t.shape[1:]))

    q = per_pair(lambda kvh: q_ref[:, kvh])
    ck = per_pair(lambda kvh: ck_ref[:, :, kvh * hd:(kvh + 1) * hd]).astype(BF16)
    cv = per_pair(lambda kvh: cv_ref[:, :, kvh * hd:(kvh + 1) * hd]).astype(BF16)
    kn = _bf16_round(per_pair(lambda kvh: kvn_ref[:, :, kvh * hd:(kvh + 1) * hd]))
    vn = _bf16_round(per_pair(lambda kvh: kvn_ref[:, :, kv_w + kvh * hd:kv_w + (kvh + 1) * hd]))
    sink = per_head_table(sink_ref[...])
    s1 = lax.dot_general(q.astype(BF16), ck, (((2,), (2,)), ((0,), (0,))),
                         preferred_element_type=F32) * scale + per_head_table(bias1_ref[...])
    qr = _bf16_round(q)
    s2 = [jnp.sum(qr * kn[:, t2:t2 + 1, :], axis=-1, keepdims=True) * scale
          + per_head_table(bias2_ref[:, t2]) for t2 in range(DEC_SEQ)]
    m = jnp.maximum(jnp.max(s1, axis=-1, keepdims=True), sink)
    for st in s2:
        m = jnp.maximum(m, st)
    e1 = jnp.exp(s1 - m)
    e2 = [jnp.exp(st - m) for st in s2]
    den = jnp.sum(e1, axis=-1, keepdims=True) + jnp.exp(sink - m)
    for et in e2:
        den = den + et
    inv = 1.0 / den
    o = lax.dot_general((e1 * inv).astype(BF16), cv, (((2,), (1,)), ((0,), (0,))),
                        preferred_element_type=F32)
    for t2 in range(DEC_SEQ):
        o = o + _bf16_round(e2[t2] * inv) * vn[:, t2:t2 + 1, :]
    for kvh in heads:
        o_ref[:, kvh] = o[kvh * bb:(kvh + 1) * bb]
    keep = WINDOW - DEC_SEQ
    cko_ref[:, 0:keep, :] = ck_ref[:, DEC_SEQ:WINDOW, :]
    cko_ref[:, keep:WINDOW, :] = kvn_ref[:, :, 0:kv_w]
    cvo_ref[:, 0:keep, :] = cv_ref[:, DEC_SEQ:WINDOW, :]
    cvo_ref[:, keep:WINDOW, :] = kvn_ref[:, :, kv_w:2 * kv_w]


def _swa_sample_tables(slopes, sinks):
    g_slopes = jnp.tile(slopes.reshape(N_SWA_KV_HEADS, 1, SWA_GROUP), (1, DEC_SEQ, 1)).reshape(
        N_SWA_KV_HEADS, SS_ROWS)
    sink_rows = jnp.tile(sinks.reshape(N_SWA_KV_HEADS, 1, SWA_GROUP), (1, DEC_SEQ, 1)).reshape(
        N_SWA_KV_HEADS, SS_ROWS, 1)
    t_row = jnp.arange(SS_ROWS) // SWA_GROUP
    jcol = jnp.arange(WINDOW)
    dist1 = (t_row[:, None] + WINDOW - jcol[None, :])
    bias1 = jnp.where((dist1 < WINDOW)[None], -(g_slopes[:, :, None] * dist1.astype(F32)[None]), NEG_INF)
    t2 = jnp.arange(DEC_SEQ)
    dist2 = t_row[None, :] - t2[:, None]
    bias2 = jnp.where((dist2 >= 0)[None], -(g_slopes[:, None, :] * dist2.astype(F32)[None]), NEG_INF)
    return sink_rows, bias1, bias2[..., None]


def _mixer_sample_kernel(cd_ref, dec_ref, qd_ref, kd_ref, qr_ref, kr_ref, vr_ref, gr_ref, st_ref,
                         sink_ref, bias1_ref, bias2_ref, qs_ref, kvn_ref, ck_ref, cv_ref,
                         or_ref, sto_ref, os_ref, cko_ref, cvo_ref):
    _ret_sample_kernel(cd_ref, dec_ref, qd_ref, kd_ref, qr_ref, kr_ref, vr_ref, gr_ref, st_ref, or_ref, sto_ref)
    _swa_sample_kernel(sink_ref, bias1_ref, bias2_ref, qs_ref, kvn_ref, ck_ref, cv_ref, os_ref, cko_ref, cvo_ref)


def mixer_sample(u_s, state, ret_tables, q_t, kv_new, cache_k, cache_v, swa_tables):
    cd, dec, qd, kd = ret_tables
    sink_rows, bias1, bias2 = swa_tables
    kv_w = N_SWA_KV_HEADS * SWA_HEAD_DIM
    rows = RS_BB * DEC_SEQ
    per_head = lambda i, h: (h, 0, 0)
    whole3 = lambda i, h: (0, 0, 0)
    blk = lambda i, h: i * N_RET_HEADS + h
    st_spec = pl.BlockSpec((RS_BB, None, RET_DK, RET_DV), lambda i, h: (i, h, 0, 0))
    cache_spec = pl.BlockSpec((SS_BB, WINDOW, kv_w), lambda i, h: (blk(i, h), 0, 0))
    q_spec = pl.BlockSpec((SS_BB, N_SWA_KV_HEADS, SS_ROWS, SWA_HEAD_DIM), lambda i, h: (blk(i, h), 0, 0, 0))
    return pl.pallas_call(
        _mixer_sample_kernel,
        grid=(DEC_BATCH // RS_BB, N_RET_HEADS),
        in_specs=[
            pl.BlockSpec(memory_space=pltpu.SMEM),
            pl.BlockSpec((1, rows, rows), per_head),
            pl.BlockSpec((1, rows, 1), per_head),
            pl.BlockSpec((1, rows, 1), per_head),
            pl.BlockSpec((rows, RET_DK), lambda i, h: (i, COL_QR // RET_DK + h)),
            pl.BlockSpec((rows, RET_DK), lambda i, h: (i, COL_KR // RET_DK + h)),
            pl.BlockSpec((rows, RET_DV), lambda i, h: (i, COL_VR // RET_DV + h)),
            pl.BlockSpec((rows, RET_DV), lambda i, h: (i, COL_GR // RET_DV + h)),
            st_spec,
            pl.BlockSpec((N_SWA_KV_HEADS, SS_ROWS, 1), whole3),
            pl.BlockSpec((N_SWA_KV_HEADS, SS_ROWS, WINDOW), whole3),
            pl.BlockSpec((N_SWA_KV_HEADS, DEC_SEQ, SS_ROWS, 1), lambda i, h: (0, 0, 0, 0)),
            q_spec,
            pl.BlockSpec((SS_BB, DEC_SEQ, 2 * kv_w), lambda i, h: (blk(i, h), 0, 0)),
            cache_spec, cache_spec,
        ],
        out_specs=[pl.BlockSpec((rows, RET_DV), lambda i, h: (i, h)), st_spec, q_spec, cache_spec, cache_spec],
        out_shape=[
            jax.ShapeDtypeStruct((T_SAMPLE, 2048), BF16),
            jax.ShapeDtypeStruct((DEC_BATCH, N_RET_HEADS, RET_DK, RET_DV), F32),
            jax.ShapeDtypeStruct((DEC_BATCH, N_SWA_KV_HEADS, SS_ROWS, SWA_HEAD_DIM), F32),
            jax.ShapeDtypeStruct((DEC_BATCH, WINDOW, kv_w), F32),
            jax.ShapeDtypeStruct((DEC_BATCH, WINDOW, kv_w), F32),
        ],
        compiler_params=_params(("parallel", "parallel")),
        name="mixer_sample",
    )(cd, dec, qd, kd, u_s, u_s, u_s, u_s, state, sink_rows, bias1, bias2, q_t, kv_new, cache_k, cache_v)


def _merge_kernel(or_ref, os_ref, gr_ref, gs_ref, wr_ref, ws_ref, o_ref):
    a = jnp.dot(or_ref[...], wr_ref[...], preferred_element_type=F32)
    b = jnp.dot(os_ref[...], ws_ref[...], preferred_element_type=F32)
    merged = jax.nn.sigmoid(gr_ref[...]) * a + jax.nn.sigmoid(gs_ref[...]) * b
    o_ref[...] = merged.astype(BF16)


def merge_branches(o_r, o_s, u, w_ret_o_bf, w_swa_o_bf, tm):
    m = o_r.shape[0]
    resident = pl.BlockSpec((2048, D_MODEL), lambda i: (0, 0), pipeline_mode=pl.Buffered(1))
    return pl.pallas_call(
        _merge_kernel,
        grid=(m // tm,),
        in_specs=[
            pl.BlockSpec((tm, 2048), lambda i: (i, 0)),
            pl.BlockSpec((tm, 2048), lambda i: (i, 0)),
            pl.BlockSpec((pl.Element(tm), pl.Element(D_MODEL)), lambda i: (i * tm, COL_GATE_R)),
            pl.BlockSpec((pl.Element(tm), pl.Element(D_MODEL)), lambda i: (i * tm, COL_GATE_S)),
            resident, resident,
        ],
        out_specs=pl.BlockSpec((tm, D_MODEL), lambda i: (i, 0)),
        out_shape=jax.ShapeDtypeStruct((m, D_MODEL), BF16),
        compiler_params=_params(("parallel",)),
        name="merge_branches",
    )(o_r, o_s, u, u, w_ret_o_bf, w_swa_o_bf)


ROUTER_PAD = 128


def _post_mix_kernel(alpha, n_prompt_blocks, mp_ref, ms_ref, xp_ref, xs_ref, w_ref, g_ref, b_ref, wr_ref, br_ref,
                     h_ref, lg_ref):
    from_sample = pl.program_id(0) >= n_prompt_blocks
    merged = jnp.where(from_sample, ms_ref[...], mp_ref[...])
    x = jnp.where(from_sample, xs_ref[...], xp_ref[...])
    mix = jnp.dot(merged, w_ref[...], preferred_element_type=F32)
    h = _layer_norm(alpha * x + mix, g_ref[...], b_ref[...])
    h_ref[...] = h
    h_hi = h.astype(BF16)
    h_lo = (h - h_hi.astype(F32)).astype(BF16)
    hh = jnp.dot(h_hi, wr_ref[...], preferred_element_type=F32)
    lh = jnp.dot(h_lo, wr_ref[:, :ROUTER_PAD], preferred_element_type=F32)
    lg_ref[...] = (hh[:, :ROUTER_PAD] + hh[:, ROUTER_PAD:] + lh + br_ref[...]).T


def _router_split(w_router, b_router):
    w = jnp.pad(w_router, ((0, 0), (0, ROUTER_PAD - N_EXPERTS)))
    w_hi = w.astype(BF16)
    w_lo = (w - w_hi.astype(F32)).astype(BF16)
    b = jnp.pad(b_router, (0, ROUTER_PAD - N_EXPERTS)).reshape(1, ROUTER_PAD)
    return jnp.concatenate([w_hi, w_lo], axis=1), b


def post_mix(merged_p, merged_s, xp, xs, w_out_bf, ln_g, ln_b, w_router_pad, b_router_pad, alpha, tm):
    n_p = T_PROMPT // tm
    n_s = T_SAMPLE // tm
    prompt_row = lambda i: (jnp.minimum(i, n_p - 1), 0)
    sample_row = lambda i: (jnp.clip(i - n_p, 0, n_s - 1), 0)
    whole = lambda i: (0, 0)
    row = lambda i: (i, 0)
    return pl.pallas_call(
        functools.partial(_post_mix_kernel, alpha, n_p),
        grid=(n_p + n_s,),
        in_specs=[
            pl.BlockSpec((tm, 2048), prompt_row),
            pl.BlockSpec((tm, 2048), sample_row),
            pl.BlockSpec((tm, 2048), prompt_row),
            pl.BlockSpec((tm, 2048), sample_row),
            pl.BlockSpec((2048, 2048), whole),
            pl.BlockSpec((1, 2048), whole),
            pl.BlockSpec((1, 2048), whole),
            pl.BlockSpec((2048, 2 * ROUTER_PAD), whole),
            pl.BlockSpec((1, ROUTER_PAD), whole),
        ],
        out_specs=[pl.BlockSpec((tm, 2048), row), pl.BlockSpec((ROUTER_PAD, tm), lambda i: (0, i))],
        out_shape=[jax.ShapeDtypeStruct((T_ALL, D_MODEL), F32),
                   jax.ShapeDtypeStruct((ROUTER_PAD, T_ALL), F32)],
        compiler_params=_params(("parallel",)),
        name="post_mix",
    )(merged_p, merged_s, xp, xs, w_out_bf, ln_g, ln_b, w_router_pad, b_router_pad)


def _moe_kernel(ce_ref, cs_ref, cn_ref, nu_ref, tail_ref, x_hbm, wg_ref, wu_ref, wd_ref, bg_ref, bu_ref, bd_ref,
                y_hbm, x_f32, x_bf, y_acc, zero_buf, x_sem, y_sem, z_sem):
    c = pl.program_id(0)
    j = pl.program_id(1)
    nblk = cn_ref[c]
    slot = c % 2
    y_cur = y_acc.at[slot]

    def sub_rows(r):
        return pl.ds(pl.multiple_of(r * MOE_SUB, MOE_SUB), MOE_SUB)

    def x_copy(chunk):
        start = pl.multiple_of(cs_ref[chunk] * MOE_SUB, MOE_SUB)
        return pltpu.make_async_copy(x_hbm.at[pl.ds(start, MOE_R), :], x_f32, x_sem)

    def y_copy(r, sub):
        return pltpu.make_async_copy(y_cur.at[sub_rows(r), :], y_hbm.at[sub_rows(sub), :], y_sem.at[slot])

    def zero_copy(sub):
        return pltpu.make_async_copy(zero_buf, y_hbm.at[sub_rows(sub), :], z_sem)

    @pl.when(j == 0)
    def _():
        def drain(i, carry):
            y_copy(0, 0).wait()
            return carry

        lax.fori_loop(0, cn_ref[jnp.maximum(c - 2, 0)] * (c >= 2).astype(jnp.int32), drain, 0)

    @pl.when((nblk > 0) & (j == 0))
    def _():
        @pl.when(c == 0)
        def _():
            x_copy(0).start()

        x_copy(c).wait()

        def stage(r, carry):
            rows = sub_rows(r)
            x_bf[rows, :] = x_f32[rows, :].astype(BF16)
            y_cur[rows, :] = jnp.broadcast_to(bd_ref[0], (MOE_SUB, D_MODEL))
            return carry

        lax.fori_loop(0, nblk, stage, 0)

        @pl.when(c + 1 < nu_ref[0])
        def _():
            x_copy(c + 1).start()

    @pl.when(nblk > 0)
    def _():
        def rows_block(rows):
            xr = x_bf[rows, :]
            g = jnp.dot(xr, wg_ref[0], preferred_element_type=F32) + bg_ref[0]
            up = jnp.dot(xr, wu_ref[0], preferred_element_type=F32) + bu_ref[0]
            g = jnp.minimum(g, SWIGLU_LIMIT)
            up = jnp.clip(up, -SWIGLU_LIMIT, SWIGLU_LIMIT)
            glu = g * jax.nn.sigmoid(SWIGLU_ALPHA * g)
            act = ((up + 1.0) * glu).astype(BF16)
            y_cur[rows, :] += jnp.dot(act, wd_ref[0], preferred_element_type=F32)

        for n_static in MOE_STATIC_SIZES:
            @pl.when(nblk == n_static)
            def _(n_static=n_static):
                rows_block(pl.ds(0, n_static * MOE_SUB))

        @pl.when(nblk < MOE_STATIC_SIZES[0])
        def _():
            def one(r, carry):
                rows_block(sub_rows(r))
                return carry

            lax.fori_loop(0, nblk, one, 0)

    @pl.when((nblk > 0) & (j == MOE_NJ - 1))
    def _():
        def write_back(r, carry):
            y_copy(r, cs_ref[c] + r).start()
            return carry

        lax.fori_loop(0, nblk, write_back, 0)

    @pl.when((nblk == 0) & (j == 0))
    def _():
        first = tail_ref[0] + MOE_CHUNK_SUBS * (c - nu_ref[0])
        count = jnp.clip(MOE_XROWS // MOE_SUB - first, 0, MOE_CHUNK_SUBS)

        @pl.when(count > 0)
        def _():
            zero_buf[...] = jnp.zeros_like(zero_buf)

            def start(i, carry):
                zero_copy(first + i).start()
                return carry

            def wait(i, carry):
                zero_copy(first + i).wait()
                return carry

            lax.fori_loop(0, count, start, 0)
            lax.fori_loop(0, count, wait, 0)


def moe_experts(x_sorted, chunk_tables, w_gate_up, b_gate_up, w_down, b_down):
    nj = MOE_NJ

    def used_j(c, j, nu):
        return jnp.where(c < nu[0], j, nj - 1)

    return pl.pallas_call(
        _moe_kernel,
        grid_spec=pltpu.PrefetchScalarGridSpec(
            num_scalar_prefetch=5,
            grid=(MOE_NC, nj),
            in_specs=[
                pl.BlockSpec(memory_space=pl.ANY),
                pl.BlockSpec((1, D_MODEL, MOE_TF),
                             lambda c, j, ce, cs, cn, nu, tl: (ce[c], 0, used_j(c, j, nu))),
                pl.BlockSpec((1, D_MODEL, MOE_TF),
                             lambda c, j, ce, cs, cn, nu, tl: (ce[c], 0, nj + used_j(c, j, nu))),
                pl.BlockSpec((1, MOE_TF, D_MODEL),
                             lambda c, j, ce, cs, cn, nu, tl: (ce[c], used_j(c, j, nu), 0)),
                pl.BlockSpec((1, 1, MOE_TF),
                             lambda c, j, ce, cs, cn, nu, tl: (ce[c], 0, used_j(c, j, nu))),
                pl.BlockSpec((1, 1, MOE_TF),
                             lambda c, j, ce, cs, cn, nu, tl: (ce[c], 0, nj + used_j(c, j, nu))),
                pl.BlockSpec((1, 1, D_MODEL), lambda c, j, ce, cs, cn, nu, tl: (ce[c], 0, 0)),
            ],
            out_specs=pl.BlockSpec(memory_space=pl.ANY),
            scratch_shapes=[
                pltpu.VMEM((MOE_R, D_MODEL), F32),
                pltpu.VMEM((MOE_R, D_MODEL), BF16),
                pltpu.VMEM((2, MOE_R, D_MODEL), F32),
                pltpu.VMEM((MOE_SUB, D_MODEL), F32),
                pltpu.SemaphoreType.DMA(()),
                pltpu.SemaphoreType.DMA((2,)),
                pltpu.SemaphoreType.DMA(()),
            ],
        ),
        out_shape=jax.ShapeDtypeStruct((MOE_XROWS, D_MODEL), F32),
        compiler_params=pltpu.CompilerParams(dimension_semantics=("arbitrary", "arbitrary"),
                                             vmem_limit_bytes=MOE_VMEM_LIMIT),
        name="moe_experts",
    )(*chunk_tables, x_sorted, w_gate_up, w_gate_up, w_down, b_gate_up, b_gate_up, b_down)


ROUTE_TM = 512


def _route_kernel(lg_ref, tri_ref, e_ref, rank_ref, gate_ref, cnt_ref, seen):
    @pl.when(pl.program_id(0) == 0)
    def _():
        seen[...] = jnp.zeros_like(seen)

    work = lg_ref[0:N_EXPERTS, :]
    eidx = lax.broadcasted_iota(jnp.int32, work.shape, 0)
    vals, idxs, hots = [], [], []
    for _ in range(TOP_K):
        m = jnp.max(work, axis=0, keepdims=True)
        idx = jnp.min(jnp.where(work == m, eidx, N_EXPERTS), axis=0, keepdims=True)
        hot = eidx == idx
        vals.append(m)
        idxs.append(idx)
        hots.append(hot)
        work = jnp.where(hot, -jnp.inf, work)
    ex = [jnp.exp(v - vals[0]) for v in vals]
    inv = 1.0 / (ex[0] + ex[1] + ex[2] + ex[3])
    picked = jnp.zeros(work.shape, F32)
    for hot in hots:
        picked = picked + hot.astype(F32)
    before = jnp.dot(picked.astype(BF16), tri_ref[...], preferred_element_type=F32) + seen[...]
    ranks = [jnp.sum(jnp.where(hot, before, 0.0), axis=0, keepdims=True) for hot in hots]
    seen[...] += jnp.sum(picked, axis=1, keepdims=True)
    e_ref[...] = jnp.concatenate(idxs, axis=0)
    rank_ref[...] = jnp.concatenate(ranks, axis=0).astype(jnp.int32)
    gate_ref[...] = jnp.concatenate([e * inv for e in ex], axis=0)
    cnt_ref[...] = seen[...]


def route_topk(logits_t):
    tm = ROUTE_TM
    tri = (jnp.arange(tm)[:, None] < jnp.arange(tm)[None, :]).astype(BF16)
    pick = pl.BlockSpec((TOP_K, tm), lambda i: (0, i))
    return pl.pallas_call(
        _route_kernel,
        grid=(T_ALL // tm,),
        in_specs=[pl.BlockSpec((ROUTER_PAD, tm), lambda i: (0, i)),
                  pl.BlockSpec((tm, tm), lambda i: (0, 0))],
        out_specs=[pick, pick, pick, pl.BlockSpec((N_EXPERTS, 1), lambda i: (0, 0))],
        out_shape=[jax.ShapeDtypeStruct((TOP_K, T_ALL), jnp.int32),
                   jax.ShapeDtypeStruct((TOP_K, T_ALL), jnp.int32),
                   jax.ShapeDtypeStruct((TOP_K, T_ALL), F32),
                   jax.ShapeDtypeStruct((N_EXPERTS, 1), F32)],
        scratch_shapes=[pltpu.VMEM((N_EXPERTS, 1), F32)],
        compiler_params=_params(("arbitrary",)),
        name="route_topk",
    )(logits_t, tri)


def route(logits_t):
    e_kt, rank, gate_kt, counts_f = route_topk(logits_t)
    counts = counts_f.reshape(N_EXPERTS).astype(jnp.int32)
    nsub = (counts + MOE_SUB - 1) // MOE_SUB
    sub_base = jnp.cumsum(nsub) - nsub
    nch = (counts + MOE_R - 1) // MOE_R
    ch_end = jnp.cumsum(nch)
    ch_base = ch_end - nch
    n_used = ch_end[-1]
    is_e = e_kt[..., None] == jnp.arange(N_EXPERTS, dtype=jnp.int32)
    pos = MOE_SUB * jnp.sum(jnp.where(is_e, sub_base, 0), axis=-1) + rank
    tok = jnp.broadcast_to(jnp.arange(T_ALL, dtype=jnp.int32)[None, :], pos.shape)
    pad_i = jnp.arange(MOE_SUB - 1, dtype=jnp.int32)
    pad_pos = (MOE_SUB * sub_base + counts)[:, None] + pad_i[None, :]
    pad_pos = jnp.where(pad_i[None, :] < (MOE_SUB * nsub - counts)[:, None], pad_pos, MOE_XROWS)
    tail_pos = MOE_SUB * jnp.sum(nsub) + jnp.arange(MOE_XROWS - TK_ALL, dtype=jnp.int32)
    tail_pos = jnp.minimum(tail_pos, MOE_XROWS)
    keys = jnp.concatenate([pos.reshape(-1), pad_pos.reshape(-1), tail_pos]).astype(jnp.int32)
    vals = jnp.concatenate([tok.reshape(-1), jnp.zeros((keys.shape[0] - TK_ALL,), jnp.int32)])
    row_tok = lax.sort_key_val(keys, vals)[1][:MOE_XROWS]
    c = jnp.arange(MOE_NC, dtype=jnp.int32)
    cc = jnp.minimum(c, n_used - 1)
    ce = jnp.minimum(jnp.sum(cc[:, None] >= ch_end[None, :], axis=1), N_EXPERTS - 1).astype(jnp.int32)
    kk = cc - ch_base[ce]
    cs = (sub_base[ce] + MOE_CHUNK_SUBS * kk).astype(jnp.int32)
    cn = jnp.where(c < n_used, jnp.minimum(MOE_CHUNK_SUBS, nsub[ce] - MOE_CHUNK_SUBS * kk), 0).astype(jnp.int32)
    tables = (ce, cs, cn, n_used.reshape(1).astype(jnp.int32), jnp.sum(nsub).reshape(1).astype(jnp.int32))
    return gate_kt.T, row_tok, pos, tables


def _combine_kernel(alpha, h_ref, yg_ref, gate_ref, g_ref, b_ref, o_ref):
    gate = gate_ref[...]
    f = yg_ref[0] * gate[:, 0:1]
    for k in range(1, TOP_K):
        f = f + yg_ref[k] * gate[:, k:k + 1]
    o_ref[...] = _layer_norm(alpha * h_ref[...] + f, g_ref[...], b_ref[...])


def combine(h, yg, gate, ln_g, ln_b, alpha, row_off, m, tm):
    blk_off = row_off // tm
    whole = lambda i: (0, 0)
    return pl.pallas_call(
        functools.partial(_combine_kernel, alpha),
        grid=(m // tm,),
        in_specs=[
            pl.BlockSpec((tm, D_MODEL), lambda i: (blk_off + i, 0)),
            pl.BlockSpec((TOP_K, tm, D_MODEL), lambda i: (0, blk_off + i, 0)),
            pl.BlockSpec((tm, TOP_K), lambda i: (blk_off + i, 0)),
            pl.BlockSpec((1, D_MODEL), whole),
            pl.BlockSpec((1, D_MODEL), whole),
        ],
        out_specs=pl.BlockSpec((tm, D_MODEL), lambda i: (i, 0)),
        out_shape=jax.ShapeDtypeStruct((m, D_MODEL), F32),
        compiler_params=_params(("parallel",)),
        name="combine",
    )(h, yg, gate, ln_g, ln_b)


def _retention_tables(chunk, group):
    log_gamma = jnp.log(1.0 - jnp.exp2(-5.0 - jnp.arange(N_RET_HEADS, dtype=F32)))
    n = chunk * group
    pos = jnp.arange(n) % chunk
    seq = jnp.arange(n) // chunk
    diff = (pos[:, None] - pos[None, :]).astype(F32)
    same = seq[:, None] == seq[None, :]
    lg = log_gamma[:, None, None]
    dec = jnp.where(same[None] & (diff >= 0)[None], jnp.exp(lg * jnp.maximum(diff, 0.0)[None]), 0.0)
    posf = pos.astype(F32)
    qd = jnp.exp(log_gamma[:, None] * (posf + 1.0))[..., None]
    kd = jnp.exp(log_gamma[:, None] * (chunk - 1.0 - posf))[..., None]
    cd = jnp.exp(log_gamma * chunk)
    return cd, dec, qd, kd


def _alibi_slopes():
    h = jnp.arange(1, N_SWA_HEADS + 1, dtype=F32)
    return jnp.exp2(-8.0 * h / N_SWA_HEADS)


def kernel(x_prompt, x_sample, state_ret, cache_swa_k, cache_swa_v, w_in, w_ret_o, w_swa_o, w_out, swa_sinks,
           ln1_g, ln1_b, w_router, b_router, w_gate_up, b_gate_up, w_down, b_down, ln2_g, ln2_b):
    alpha = (2.0 * DEPTH) ** 0.25
    kv_w = N_SWA_KV_HEADS * SWA_HEAD_DIM
    w = w_in[0]
    w_in_bf = w.astype(BF16)
    w_ret_o_bf = w_ret_o[0].astype(BF16)
    w_swa_o_bf = w_swa_o[0].astype(BF16)
    w_out_bf = w_out[0].astype(BF16)
    w_router_pad, b_router_pad = _router_split(w_router[0], b_router[0])
    ln1g, ln1b = ln1_g[0].reshape(1, D_MODEL), ln1_b[0].reshape(1, D_MODEL)
    ln2g, ln2b = ln2_g[0].reshape(1, D_MODEL), ln2_b[0].reshape(1, D_MODEL)
    slopes = _alibi_slopes()
    sinks = swa_sinks[0].astype(F32)

    xp = x_prompt.reshape(T_PROMPT, D_MODEL)
    xs = x_sample.reshape(T_SAMPLE, D_MODEL)

    u_p = in_proj(xp, w_in_bf, 1024, 1280)
    o_r_p, state_p, o_s_p = mixer_prompt(u_p, _retention_tables(RET_CHUNK, 1), slopes, sinks)
    kv_p = u_p.reshape(BATCH, SEQ, D_IN)[:, SEQ - WINDOW:, COL_KVS:COL_KVS + 2 * kv_w]
    k_cache_p = kv_p[..., :kv_w].reshape(1, BATCH, WINDOW, N_SWA_KV_HEADS, SWA_HEAD_DIM)
    v_cache_p = kv_p[..., kv_w:].reshape(1, BATCH, WINDOW, N_SWA_KV_HEADS, SWA_HEAD_DIM)
    merged_p = merge_branches(o_r_p, o_s_p, u_p, w_ret_o_bf, w_swa_o_bf, 512)

    u_s = in_proj(xs, w_in_bf, 512, 1280)
    q_t = (u_s[:, COL_QS:COL_QS + 2048]
           .reshape(DEC_BATCH, DEC_SEQ, N_SWA_KV_HEADS, SWA_GROUP, SWA_HEAD_DIM)
           .transpose(0, 2, 1, 3, 4).reshape(DEC_BATCH, N_SWA_KV_HEADS, SS_ROWS, SWA_HEAD_DIM))
    kv_new = u_s[:, COL_KVS:COL_KVS + 2 * kv_w].reshape(DEC_BATCH, DEC_SEQ, 2 * kv_w)
    o_r_s, state_s, o_s_t, k_cache_s, v_cache_s = mixer_sample(
        u_s, state_ret[0], _retention_tables(DEC_SEQ, RS_BB), q_t, kv_new,
        cache_swa_k[0].reshape(DEC_BATCH, WINDOW, kv_w), cache_swa_v[0].reshape(DEC_BATCH, WINDOW, kv_w),
        _swa_sample_tables(slopes, sinks))
    o_s_s = (o_s_t.reshape(DEC_BATCH, N_SWA_KV_HEADS, DEC_SEQ, SWA_GROUP, SWA_HEAD_DIM)
             .transpose(0, 2, 1, 3, 4).reshape(T_SAMPLE, 2048).astype(BF16))
    merged_s = merge_branches(o_r_s, o_s_s, u_s, w_ret_o_bf, w_swa_o_bf, 512)

    h, logits = post_mix(merged_p, merged_s, xp, xs, w_out_bf, ln1g, ln1b, w_router_pad, b_router_pad,
                         alpha, 512)

    gate, row_tok, pos, tables = route(logits)
    x_sorted = h.at[row_tok].get(mode="promise_in_bounds")
    y_rows = moe_experts(x_sorted, tables, w_gate_up[0], b_gate_up[0].reshape(N_EXPERTS, 1, 2 * D_FF),
                         w_down[0], b_down[0].reshape(N_EXPERTS, 1, D_MODEL))
    yg = y_rows.at[pos].get(mode="promise_in_bounds")
    y_p = combine(h, yg, gate, ln2g, ln2b, alpha, 0, T_PROMPT, 256)
    y_s = combine(h, yg, gate, ln2g, ln2b, alpha, T_PROMPT, T_SAMPLE, 256)

    return (y_p.reshape(BATCH, SEQ, D_MODEL), y_s.reshape(DEC_BATCH, DEC_SEQ, D_MODEL),
            state_p[None], k_cache_p, v_cache_p,
            state_s[None], k_cache_s.reshape(1, DEC_BATCH, WINDOW, N_SWA_KV_HEADS, SWA_HEAD_DIM),
            v_cache_s.reshape(1, DEC_BATCH, WINDOW, N_SWA_KV_HEADS, SWA_HEAD_DIM))
```

```python
import functools

import jax
import jax.numpy as jnp
from jax import lax
from jax.experimental import pallas as pl
from jax.experimental.pallas import tpu as pltpu

F32 = jnp.float32
BF16 = jnp.bfloat16

D_MODEL = 2048
BATCH = 4
SEQ = 2048
DEC_BATCH = 128
DEC_SEQ = 4
N_RET_HEADS = 8
RET_DK = 128
RET_DV = 256
RET_CHUNK = 128
N_SWA_HEADS = 32
N_SWA_KV_HEADS = 4
SWA_GROUP = 8
SWA_HEAD_DIM = 64
WINDOW = 128
N_EXPERTS = 32
TOP_K = 4
D_FF = 2048
SWIGLU_LIMIT = 7.0
SWIGLU_ALPHA = 1.702
LN_EPS = 1e-5
GN_EPS = 1e-6
NEG_INF = -1e30
DEPTH = 1

T_PROMPT = BATCH * SEQ
T_SAMPLE = DEC_BATCH * DEC_SEQ
T_ALL = T_PROMPT + T_SAMPLE
TK_ALL = T_ALL * TOP_K
D_IN = 12800

COL_QR, COL_KR, COL_VR, COL_GR, COL_QS, COL_KVS, COL_GATE_R, COL_GATE_S = (
    0, 1024, 2048, 4096, 6144, 8192, 8704, 10752)

MOE_SUB = 64
MOE_CHUNK_SUBS = 22
MOE_STATIC_SIZES = tuple(range(13, 23))
MOE_R = MOE_SUB * MOE_CHUNK_SUBS
MOE_NSUB_MAX = TK_ALL // MOE_SUB + N_EXPERTS
MOE_XROWS = (MOE_NSUB_MAX + MOE_CHUNK_SUBS - 1) * MOE_SUB
MOE_NC = TK_ALL // MOE_R + N_EXPERTS + 2
MOE_TF = 256
MOE_NJ = D_FF // MOE_TF
MOE_VMEM_LIMIT = 60 * 1024 * 1024

VMEM_LIMIT = 56 * 1024 * 1024


def _params(sem):
    return pltpu.CompilerParams(dimension_semantics=sem, vmem_limit_bytes=VMEM_LIMIT)


def _in_proj_kernel(x_ref, w_ref, o_ref):
    o_ref[...] = jnp.dot(x_ref[...].astype(BF16), w_ref[...], preferred_element_type=F32)


def in_proj(x2d, w_bf, tm, tn):
    m, k = x2d.shape
    n = w_bf.shape[1]
    return pl.pallas_call(
        _in_proj_kernel,
        grid=(m // tm, n // tn),
        in_specs=[pl.BlockSpec((tm, k), lambda i, j: (i, 0)),
                  pl.BlockSpec((k, tn), lambda i, j: (0, j))],
        out_specs=pl.BlockSpec((tm, tn), lambda i, j: (i, j)),
        out_shape=jax.ShapeDtypeStruct((m, n), F32),
        compiler_params=_params(("parallel", "parallel")),
        name="in_proj",
    )(x2d, w_bf)


def _group_norm_gate(o, g):
    mu = jnp.mean(o, axis=-1, keepdims=True)
    oc = o - mu
    var = jnp.mean(oc * oc, axis=-1, keepdims=True)
    return oc * lax.rsqrt(var + GN_EPS) * (g * jax.nn.sigmoid(g))


def _dot_nt(a, b):
    return lax.dot_general(a, b, (((1,), (1,)), ((), ())), preferred_element_type=F32)


def _layer_norm(x, g, b):
    mu = jnp.mean(x, axis=-1, keepdims=True)
    xc = x - mu
    var = jnp.mean(xc * xc, axis=-1, keepdims=True)
    return xc * lax.rsqrt(var + LN_EPS) * g + b


def _ret_prompt_kernel(cd_ref, dec_ref, qd_ref, kd_ref, q_ref, k_ref, v_ref, g_ref, o_ref, st_ref):
    c = pl.program_id(1)

    @pl.when(c == 0)
    def _():
        st_ref[...] = jnp.zeros_like(st_ref)

    heads = range(N_RET_HEADS)
    q = jnp.stack([q_ref[:, h * RET_DK:(h + 1) * RET_DK] for h in heads])
    k = jnp.stack([k_ref[:, h * RET_DK:(h + 1) * RET_DK] for h in heads]) * (RET_DK ** -0.5)
    v = jnp.stack([v_ref[:, h * RET_DV:(h + 1) * RET_DV] for h in heads]).astype(BF16)
    g = jnp.stack([g_ref[:, h * RET_DV:(h + 1) * RET_DV] for h in heads])
    state = st_ref[0]
    bmm = functools.partial(lax.dot_general, preferred_element_type=F32)
    scores = bmm(q.astype(BF16), k.astype(BF16), (((2,), (2,)), ((0,), (0,)))) * dec_ref[...]
    o = (bmm(scores.astype(BF16), v, (((2,), (1,)), ((0,), (0,))))
         + bmm((q * qd_ref[...]).astype(BF16), state.astype(BF16), (((2,), (1,)), ((0,), (0,)))))
    kt = jnp.swapaxes(k * kd_ref[...], 1, 2).astype(BF16)
    cd = jnp.stack([jnp.full((1, 1), cd_ref[h], F32) for h in heads])
    st_ref[0] = cd * state + bmm(kt, v, (((2,), (1,)), ((0,), (0,))))
    out = _group_norm_gate(o, g).astype(BF16)
    for h in heads:
        o_ref[:, h * RET_DV:(h + 1) * RET_DV] = out[h]


RS_BB = 32


def _ret_sample_kernel(cd_ref, dec_ref, qd_ref, kd_ref, q_ref, k_ref, v_ref, g_ref, st_ref,
                       o_ref, sto_ref):
    h = pl.program_id(1)
    q = q_ref[...]
    k = k_ref[...] * (RET_DK ** -0.5)
    v = v_ref[...].astype(BF16)
    scores = _dot_nt(q.astype(BF16), k.astype(BF16)) * dec_ref[0]
    o_intra = jnp.dot(scores.astype(BF16), v, preferred_element_type=F32)
    qq = (q * qd_ref[0]).astype(BF16)
    kt = (k * kd_ref[0]).T
    col_batch = lax.broadcasted_iota(jnp.int32, kt.shape, 1) // DEC_SEQ
    first_of_pair = lax.broadcasted_iota(jnp.int32, (8, RET_DV), 0) < DEC_SEQ
    cd = cd_ref[h]
    pieces = []
    for p in range(RS_BB // 2):
        q8 = qq[8 * p:8 * p + 8]
        s0 = st_ref[2 * p].astype(BF16)
        s1 = st_ref[2 * p + 1].astype(BF16)
        r0 = jnp.dot(q8, s0, preferred_element_type=F32)
        r1 = jnp.dot(q8, s1, preferred_element_type=F32)
        pieces.append(jnp.where(first_of_pair, r0, r1))
    for b in range(RS_BB):
        ktm = jnp.where(col_batch == b, kt, 0.0).astype(BF16)
        sto_ref[b] = cd * st_ref[b] + jnp.dot(ktm, v, preferred_element_type=F32)
    o = o_intra + jnp.concatenate(pieces, axis=0)
    o_ref[...] = _group_norm_gate(o, g_ref[...]).astype(BF16)


def _swa_prompt_kernel(sink_ref, bias_ref, q_ref, kvc_ref, kvp_ref, o_ref):
    kv_w = N_SWA_KV_HEADS * SWA_HEAD_DIM
    hd = SWA_HEAD_DIM
    qt = q_ref[...].T
    outs = []
    for kvh in range(N_SWA_KV_HEADS):
        lo = kvh * hd
        k_cat = jnp.concatenate([kvp_ref[:, lo:lo + hd], kvc_ref[:, lo:lo + hd]], axis=0).astype(BF16)
        v_cat = jnp.concatenate([kvp_ref[:, kv_w + lo:kv_w + lo + hd],
                                 kvc_ref[:, kv_w + lo:kv_w + lo + hd]], axis=0)
        vt = v_cat.T.astype(BF16)
        heads = range(kvh * SWA_GROUP, (kvh + 1) * SWA_GROUP)
        q8 = jnp.concatenate([qt[h * hd:(h + 1) * hd, :] for h in heads], axis=1)
        q8 = (q8 * (hd ** -0.5)).astype(BF16)
        s = jnp.dot(k_cat, q8, preferred_element_type=F32) + bias_ref[0, kvh]
        sink = sink_ref[kvh]
        m = jnp.maximum(jnp.max(s, axis=0, keepdims=True), sink)
        e = jnp.exp(s - m)
        den = jnp.sum(e, axis=0, keepdims=True) + jnp.exp(sink - m)
        p = (e * (1.0 / den)).astype(BF16)
        ot = jnp.dot(vt, p, preferred_element_type=F32)
        outs += [ot[:, g * WINDOW:(g + 1) * WINDOW] for g in range(SWA_GROUP)]
    o_ref[...] = jnp.concatenate(outs, axis=0).T.astype(BF16)


def _swa_prompt_tables(slopes, sinks):
    qi = jnp.arange(WINDOW)[:, None] + WINDOW
    ki = jnp.arange(2 * WINDOW)[None, :]
    dist = qi - ki
    in_window = (dist >= 0) & (dist < WINDOW)
    valid = jnp.stack([in_window & (ki >= WINDOW), in_window])
    pen = -(slopes[:, None, None] * dist.astype(F32)[None])
    bias = jnp.where(valid[:, None], pen[None], NEG_INF)
    bias = bias.reshape(2, N_SWA_KV_HEADS, SWA_GROUP, WINDOW, 2 * WINDOW).transpose(0, 1, 4, 2, 3)
    bias = bias.reshape(2, N_SWA_KV_HEADS, 2 * WINDOW, SWA_GROUP * WINDOW)
    sink_l = jnp.repeat(sinks.reshape(N_SWA_KV_HEADS, SWA_GROUP), WINDOW, axis=1)
    return sink_l.reshape(N_SWA_KV_HEADS, 1, SWA_GROUP * WINDOW), bias


def _mixer_prompt_kernel(cd_ref, dec_ref, qd_ref, kd_ref, sink_ref, bias_ref, qr_ref, kr_ref, vr_ref, gr_ref,
                         qs_ref, kvc_ref, kvp_ref, or_ref, st_ref, os_ref):
    _ret_prompt_kernel(cd_ref, dec_ref, qd_ref, kd_ref, qr_ref, kr_ref, vr_ref, gr_ref, or_ref, st_ref)
    _swa_prompt_kernel(sink_ref, bias_ref, qs_ref, kvc_ref, kvp_ref, os_ref)


def mixer_prompt(u, ret_tables, slopes, sinks):
    cd, dec, qd, kd = ret_tables
    sink_l, bias = _swa_prompt_tables(slopes, sinks)
    nb = SEQ // WINDOW
    lanes = SWA_GROUP * WINDOW
    kv_blk = COL_KVS // 512
    row = lambda b, c: b * nb + c
    whole3 = lambda b, c: (0, 0, 0)
    return pl.pallas_call(
        _mixer_prompt_kernel,
        grid=(BATCH, nb),
        in_specs=[
            pl.BlockSpec(memory_space=pltpu.SMEM),
            pl.BlockSpec((N_RET_HEADS, RET_CHUNK, RET_CHUNK), whole3),
            pl.BlockSpec((N_RET_HEADS, RET_CHUNK, 1), whole3),
            pl.BlockSpec((N_RET_HEADS, RET_CHUNK, 1), whole3),
            pl.BlockSpec((N_SWA_KV_HEADS, 1, lanes), whole3),
            pl.BlockSpec((1, N_SWA_KV_HEADS, 2 * WINDOW, lanes), lambda b, c: (jnp.minimum(c, 1), 0, 0, 0)),
            pl.BlockSpec((RET_CHUNK, 1024), lambda b, c: (row(b, c), COL_QR // 1024)),
            pl.BlockSpec((RET_CHUNK, 1024), lambda b, c: (row(b, c), COL_KR // 1024)),
            pl.BlockSpec((RET_CHUNK, 2048), lambda b, c: (row(b, c), COL_VR // 2048)),
            pl.BlockSpec((RET_CHUNK, 2048), lambda b, c: (row(b, c), COL_GR // 2048)),
            pl.BlockSpec((WINDOW, 2048), lambda b, c: (row(b, c), COL_QS // 2048)),
            pl.BlockSpec((WINDOW, 512), lambda b, c: (row(b, c), kv_blk)),
            pl.BlockSpec((WINDOW, 512), lambda b, c: (b * nb + jnp.maximum(c - 1, 0), kv_blk)),
        ],
        out_specs=[
            pl.BlockSpec((RET_CHUNK, 2048), lambda b, c: (row(b, c), 0)),
            pl.BlockSpec((1, N_RET_HEADS, RET_DK, RET_DV), lambda b, c: (b, 0, 0, 0)),
            pl.BlockSpec((WINDOW, 2048), lambda b, c: (row(b, c), 0)),
        ],
        out_shape=[
            jax.ShapeDtypeStruct((T_PROMPT, 2048), BF16),
            jax.ShapeDtypeStruct((BATCH, N_RET_HEADS, RET_DK, RET_DV), F32),
            jax.ShapeDtypeStruct((T_PROMPT, 2048), BF16),
        ],
        compiler_params=_params(("parallel", "arbitrary")),
        name="mixer_prompt",
    )(cd, dec, qd, kd, sink_l, bias, u, u, u, u, u, u, u)


SS_BB = RS_BB // N_RET_HEADS
SS_ROWS = DEC_SEQ * SWA_GROUP


def _bf16_round(x):
    return x.astype(BF16).astype(F32)


def _swa_sample_kernel(sink_ref, bias1_ref, bias2_ref, q_ref, kvn_ref, ck_ref, cv_ref,
                       o_ref, cko_ref, cvo_ref):
    kv_w = N_SWA_KV_HEADS * SWA_HEAD_DIM
    hd = SWA_HEAD_DIM
    scale = hd ** -0.5
    bb = q_ref.shape[0]
    heads = range(N_SWA_KV_HEADS)

    def per_pair(f):
        return jnp.concatenate([f(kvh) for kvh in heads], axis=0)

    def per_head_table(t):
        return per_pair(lambda kvh: jnp.broadcast_to(t[kvh][None], (bb,) + t.shape[1:]))

    q = per_pair(lambda kvh: q_ref[:, kvh])
    ck = per_pair(lambda kvh: ck_ref[:, :, kvh * hd:(kvh + 1) * hd]).astype(BF16)
    cv = per_pair(lambda kvh: cv_ref[:, :, kvh * hd:(kvh + 1) * hd]).astype(BF16)
    kn = _bf16_round(per_pair(lambda kvh: kvn_ref[:, :, kvh * hd:(kvh + 1) * hd]))
    vn = _bf16_round(per_pair(lambda kvh: kvn_ref[:, :, kv_w + kvh * hd:kv_w + (kvh + 1) * hd]))
    sink = per_head_table(sink_ref[...])
    s1 = lax.dot_general(q.astype(BF16), ck, (((2,), (2,)), ((0,), (0,))),
                         preferred_element_type=F32) * scale + per_head_table(bias1_ref[...])
    qr = _bf16_round(q)
    s2 = [jnp.sum(qr * kn[:, t2:t2 + 1, :], axis=-1, keepdims=True) * scale
          + per_head_table(bias2_ref[:, t2]) for t2 in range(DEC_SEQ)]
    m = jnp.maximum(jnp.max(s1, axis=-1, keepdims=True), sink)
    for st in s2:
        m = jnp.maximum(m, st)
    e1 = jnp.exp(s1 - m)
    e2 = [jnp.exp(st - m) for st in s2]
    den = jnp.sum(e1, axis=-1, keepdims=True) + jnp.exp(sink - m)
    for et in e2:
        den = den + et
    inv = 1.0 / den
    o = lax.dot_general((e1 * inv).astype(BF16), cv, (((2,), (1,)), ((0,), (0,))),
                        preferred_element_type=F32)
    for t2 in range(DEC_SEQ):
        o = o + _bf16_round(e2[t2] * inv) * vn[:, t2:t2 + 1, :]
    for kvh in heads:
        o_ref[:, kvh] = o[kvh * bb:(kvh + 1) * bb]
    keep = WINDOW - DEC_SEQ
    cko_ref[:, 0:keep, :] = ck_ref[:, DEC_SEQ:WINDOW, :]
    cko_ref[:, keep:WINDOW, :] = kvn_ref[:, :, 0:kv_w]
    cvo_ref[:, 0:keep, :] = cv_ref[:, DEC_SEQ:WINDOW, :]
    cvo_ref[:, keep:WINDOW, :] = kvn_ref[:, :, kv_w:2 * kv_w]


def _swa_sample_tables(slopes, sinks):
    g_slopes = jnp.tile(slopes.reshape(N_SWA_KV_HEADS, 1, SWA_GROUP), (1, DEC_SEQ, 1)).reshape(
        N_SWA_KV_HEADS, SS_ROWS)
    sink_rows = jnp.tile(sinks.reshape(N_SWA_KV_HEADS, 1, SWA_GROUP), (1, DEC_SEQ, 1)).reshape(
        N_SWA_KV_HEADS, SS_ROWS, 1)
    t_row = jnp.arange(SS_ROWS) // SWA_GROUP
    jcol = jnp.arange(WINDOW)
    dist1 = (t_row[:, None] + WINDOW - jcol[None, :])
    bias1 = jnp.where((dist1 < WINDOW)[None], -(g_slopes[:, :, None] * dist1.astype(F32)[None]), NEG_INF)
    t2 = jnp.arange(DEC_SEQ)
    dist2 = t_row[None, :] - t2[:, None]
    bias2 = jnp.where((dist2 >= 0)[None], -(g_slopes[:, None, :] * dist2.astype(F32)[None]), NEG_INF)
    return sink_rows, bias1, bias2[..., None]


def _mixer_sample_kernel(cd_ref, dec_ref, qd_ref, kd_ref, qr_ref, kr_ref, vr_ref, gr_ref, st_ref,
                         sink_ref, bias1_ref, bias2_ref, qs_ref, kvn_ref, ck_ref, cv_ref,
                         or_ref, sto_ref, os_ref, cko_ref, cvo_ref):
    _ret_sample_kernel(cd_ref, dec_ref, qd_ref, kd_ref, qr_ref, kr_ref, vr_ref, gr_ref, st_ref, or_ref, sto_ref)
    _swa_sample_kernel(sink_ref, bias1_ref, bias2_ref, qs_ref, kvn_ref, ck_ref, cv_ref, os_ref, cko_ref, cvo_ref)


def mixer_sample(u_s, state, ret_tables, q_t, kv_new, cache_k, cache_v, swa_tables):
    cd, dec, qd, kd = ret_tables
    sink_rows, bias1, bias2 = swa_tables
    kv_w = N_SWA_KV_HEADS * SWA_HEAD_DIM
    rows = RS_BB * DEC_SEQ
    per_head = lambda i, h: (h, 0, 0)
    whole3 = lambda i, h: (0, 0, 0)
    blk = lambda i, h: i * N_RET_HEADS + h
    st_spec = pl.BlockSpec((RS_BB, None, RET_DK, RET_DV), lambda i, h: (i, h, 0, 0))
    cache_spec = pl.BlockSpec((SS_BB, WINDOW, kv_w), lambda i, h: (blk(i, h), 0, 0))
    q_spec = pl.BlockSpec((SS_BB, N_SWA_KV_HEADS, SS_ROWS, SWA_HEAD_DIM), lambda i, h: (blk(i, h), 0, 0, 0))
    return pl.pallas_call(
        _mixer_sample_kernel,
        grid=(DEC_BATCH // RS_BB, N_RET_HEADS),
        in_specs=[
            pl.BlockSpec(memory_space=pltpu.SMEM),
            pl.BlockSpec((1, rows, rows), per_head),
            pl.BlockSpec((1, rows, 1), per_head),
            pl.BlockSpec((1, rows, 1), per_head),
            pl.BlockSpec((rows, RET_DK), lambda i, h: (i, COL_QR // RET_DK + h)),
            pl.BlockSpec((rows, RET_DK), lambda i, h: (i, COL_KR // RET_DK + h)),
            pl.BlockSpec((rows, RET_DV), lambda i, h: (i, COL_VR // RET_DV + h)),
            pl.BlockSpec((rows, RET_DV), lambda i, h: (i, COL_GR // RET_DV + h)),
            st_spec,
            pl.BlockSpec((N_SWA_KV_HEADS, SS_ROWS, 1), whole3),
            pl.BlockSpec((N_SWA_KV_HEADS, SS_ROWS, WINDOW), whole3),
            pl.BlockSpec((N_SWA_KV_HEADS, DEC_SEQ, SS_ROWS, 1), lambda i, h: (0, 0, 0, 0)),
            q_spec,
            pl.BlockSpec((SS_BB, DEC_SEQ, 2 * kv_w), lambda i, h: (blk(i, h), 0, 0)),
            cache_spec, cache_spec,
        ],
        out_specs=[pl.BlockSpec((rows, RET_DV), lambda i, h: (i, h)), st_spec, q_spec, cache_spec, cache_spec],
        out_shape=[
            jax.ShapeDtypeStruct((T_SAMPLE, 2048), BF16),
            jax.ShapeDtypeStruct((DEC_BATCH, N_RET_HEADS, RET_DK, RET_DV), F32),
            jax.ShapeDtypeStruct((DEC_BATCH, N_SWA_KV_HEADS, SS_ROWS, SWA_HEAD_DIM), F32),
            jax.ShapeDtypeStruct((DEC_BATCH, WINDOW, kv_w), F32),
            jax.ShapeDtypeStruct((DEC_BATCH, WINDOW, kv_w), F32),
        ],
        compiler_params=_params(("parallel", "parallel")),
        name="mixer_sample",
    )(cd, dec, qd, kd, u_s, u_s, u_s, u_s, state, sink_rows, bias1, bias2, q_t, kv_new, cache_k, cache_v)


def _merge_kernel(or_ref, os_ref, gr_ref, gs_ref, wr_ref, ws_ref, o_ref):
    a = jnp.dot(or_ref[...], wr_ref[...], preferred_element_type=F32)
    b = jnp.dot(os_ref[...], ws_ref[...], preferred_element_type=F32)
    merged = jax.nn.sigmoid(gr_ref[...]) * a + jax.nn.sigmoid(gs_ref[...]) * b
    o_ref[...] = merged.astype(BF16)


def merge_branches(o_r, o_s, u, w_ret_o_bf, w_swa_o_bf, tm):
    m = o_r.shape[0]
    resident = pl.BlockSpec((2048, D_MODEL), lambda i: (0, 0), pipeline_mode=pl.Buffered(1))
    return pl.pallas_call(
        _merge_kernel,
        grid=(m // tm,),
        in_specs=[
            pl.BlockSpec((tm, 2048), lambda i: (i, 0)),
            pl.BlockSpec((tm, 2048), lambda i: (i, 0)),
            pl.BlockSpec((pl.Element(tm), pl.Element(D_MODEL)), lambda i: (i * tm, COL_GATE_R)),
            pl.BlockSpec((pl.Element(tm), pl.Element(D_MODEL)), lambda i: (i * tm, COL_GATE_S)),
            resident, resident,
        ],
        out_specs=pl.BlockSpec((tm, D_MODEL), lambda i: (i, 0)),
        out_shape=jax.ShapeDtypeStruct((m, D_MODEL), BF16),
        compiler_params=_params(("parallel",)),
        name="merge_branches",
    )(o_r, o_s, u, u, w_ret_o_bf, w_swa_o_bf)


ROUTER_PAD = 128


def _post_mix_kernel(alpha, n_prompt_blocks, mp_ref, ms_ref, xp_ref, xs_ref, w_ref, g_ref, b_ref, wr_ref, br_ref,
                     h_ref, lg_ref):
    from_sample = pl.program_id(0) >= n_prompt_blocks
    merged = jnp.where(from_sample, ms_ref[...], mp_ref[...])
    x = jnp.where(from_sample, xs_ref[...], xp_ref[...])
    mix = jnp.dot(merged, w_ref[...], preferred_element_type=F32)
    h = _layer_norm(alpha * x + mix, g_ref[...], b_ref[...])
    h_ref[...] = h
    h_hi = h.astype(BF16)
    h_lo = (h - h_hi.astype(F32)).astype(BF16)
    hh = jnp.dot(h_hi, wr_ref[...], preferred_element_type=F32)
    lh = jnp.dot(h_lo, wr_ref[:, :ROUTER_PAD], preferred_element_type=F32)
    lg_ref[...] = (hh[:, :ROUTER_PAD] + hh[:, ROUTER_PAD:] + lh + br_ref[...]).T


def _router_split(w_router, b_router):
    w = jnp.pad(w_router, ((0, 0), (0, ROUTER_PAD - N_EXPERTS)))
    w_hi = w.astype(BF16)
    w_lo = (w - w_hi.astype(F32)).astype(BF16)
    b = jnp.pad(b_router, (0, ROUTER_PAD - N_EXPERTS)).reshape(1, ROUTER_PAD)
    return jnp.concatenate([w_hi, w_lo], axis=1), b


def post_mix(merged_p, merged_s, xp, xs, w_out_bf, ln_g, ln_b, w_router_pad, b_router_pad, alpha, tm):
    n_p = T_PROMPT // tm
    n_s = T_SAMPLE // tm
    prompt_row = lambda i: (jnp.minimum(i, n_p - 1), 0)
    sample_row = lambda i: (jnp.clip(i - n_p, 0, n_s - 1), 0)
    whole = lambda i: (0, 0)
    row = lambda i: (i, 0)
    return pl.pallas_call(
        functools.partial(_post_mix_kernel, alpha, n_p),
        grid=(n_p + n_s,),
        in_specs=[
            pl.BlockSpec((tm, 2048), prompt_row),
            pl.BlockSpec((tm, 2048), sample_row),
            pl.BlockSpec((tm, 2048), prompt_row),
            pl.BlockSpec((tm, 2048), sample_row),
            pl.BlockSpec((2048, 2048), whole),
            pl.BlockSpec((1, 2048), whole),
            pl.BlockSpec((1, 2048), whole),
            pl.BlockSpec((2048, 2 * ROUTER_PAD), whole),
            pl.BlockSpec((1, ROUTER_PAD), whole),
        ],
        out_specs=[pl.BlockSpec((tm, 2048), row), pl.BlockSpec((ROUTER_PAD, tm), lambda i: (0, i))],
        out_shape=[jax.ShapeDtypeStruct((T_ALL, D_MODEL), F32),
                   jax.ShapeDtypeStruct((ROUTER_PAD, T_ALL), F32)],
        compiler_params=_params(("parallel",)),
        name="post_mix",
    )(merged_p, merged_s, xp, xs, w_out_bf, ln_g, ln_b, w_router_pad, b_router_pad)


def _moe_kernel(ce_ref, cs_ref, cn_ref, nu_ref, tail_ref, x_hbm, wg_ref, wu_ref, wd_ref, bg_ref, bu_ref, bd_ref,
                y_hbm, x_f32, x_bf, y_acc, zero_buf, x_sem, y_sem, z_sem):
    c = pl.program_id(0)
    j = pl.program_id(1)
    nblk = cn_ref[c]
    slot = c % 2
    y_cur = y_acc.at[slot]

    def sub_rows(r):
        return pl.ds(pl.multiple_of(r * MOE_SUB, MOE_SUB), MOE_SUB)

    def x_copy(chunk):
        start = pl.multiple_of(cs_ref[chunk] * MOE_SUB, MOE_SUB)
        return pltpu.make_async_copy(x_hbm.at[pl.ds(start, MOE_R), :], x_f32, x_sem)

    def y_copy(r, sub):
        return pltpu.make_async_copy(y_cur.at[sub_rows(r), :], y_hbm.at[sub_rows(sub), :], y_sem.at[slot])

    def zero_copy(sub):
        return pltpu.make_async_copy(zero_buf, y_hbm.at[sub_rows(sub), :], z_sem)

    @pl.when(j == 0)
    def _():
        def drain(i, carry):
            y_copy(0, 0).wait()
            return carry

        lax.fori_loop(0, cn_ref[jnp.maximum(c - 2, 0)] * (c >= 2).astype(jnp.int32), drain, 0)

    @pl.when((nblk > 0) & (j == 0))
    def _():
        @pl.when(c == 0)
        def _():
            x_copy(0).start()

        x_copy(c).wait()

        def stage(r, carry):
            rows = sub_rows(r)
            x_bf[rows, :] = x_f32[rows, :].astype(BF16)
            y_cur[rows, :] = jnp.broadcast_to(bd_ref[0], (MOE_SUB, D_MODEL))
            return carry

        lax.fori_loop(0, nblk, stage, 0)

        @pl.when(c + 1 < nu_ref[0])
        def _():
            x_copy(c + 1).start()

    @pl.when(nblk > 0)
    def _():
        def rows_block(rows):
            xr = x_bf[rows, :]
            g = jnp.dot(xr, wg_ref[0], preferred_element_type=F32) + bg_ref[0]
            up = jnp.dot(xr, wu_ref[0], preferred_element_type=F32) + bu_ref[0]
            g = jnp.minimum(g, SWIGLU_LIMIT)
            up = jnp.clip(up, -SWIGLU_LIMIT, SWIGLU_LIMIT)
            glu = g * jax.nn.sigmoid(SWIGLU_ALPHA * g)
            act = ((up + 1.0) * glu).astype(BF16)
            y_cur[rows, :] += jnp.dot(act, wd_ref[0], preferred_element_type=F32)

        for n_static in MOE_STATIC_SIZES:
            @pl.when(nblk == n_static)
            def _(n_static=n_static):
                rows_block(pl.ds(0, n_static * MOE_SUB))

        @pl.when(nblk < MOE_STATIC_SIZES[0])
        def _():
            def one(r, carry):
                rows_block(sub_rows(r))
                return carry

            lax.fori_loop(0, nblk, one, 0)

    @pl.when((nblk > 0) & (j == MOE_NJ - 1))
    def _():
        def write_back(r, carry):
            y_copy(r, cs_ref[c] + r).start()
            return carry

        lax.fori_loop(0, nblk, write_back, 0)

    @pl.when((nblk == 0) & (j == 0))
    def _():
        first = tail_ref[0] + MOE_CHUNK_SUBS * (c - nu_ref[0])
        count = jnp.clip(MOE_XROWS // MOE_SUB - first, 0, MOE_CHUNK_SUBS)

        @pl.when(count > 0)
        def _():
            zero_buf[...] = jnp.zeros_like(zero_buf)

            def start(i, carry):
                zero_copy(first + i).start()
                return carry

            def wait(i, carry):
                zero_copy(first + i).wait()
                return carry

            lax.fori_loop(0, count, start, 0)
            lax.fori_loop(0, count, wait, 0)


def moe_experts(x_sorted, chunk_tables, w_gate_up, b_gate_up, w_down, b_down):
    nj = MOE_NJ

    def used_j(c, j, nu):
        return jnp.where(c < nu[0], j, nj - 1)

    return pl.pallas_call(
        _moe_kernel,
        grid_spec=pltpu.PrefetchScalarGridSpec(
            num_scalar_prefetch=5,
            grid=(MOE_NC, nj),
            in_specs=[
                pl.BlockSpec(memory_space=pl.ANY),
                pl.BlockSpec((1, D_MODEL, MOE_TF),
                             lambda c, j, ce, cs, cn, nu, tl: (ce[c], 0, used_j(c, j, nu))),
                pl.BlockSpec((1, D_MODEL, MOE_TF),
                             lambda c, j, ce, cs, cn, nu, tl: (ce[c], 0, nj + used_j(c, j, nu))),
                pl.BlockSpec((1, MOE_TF, D_MODEL),
                             lambda c, j, ce, cs, cn, nu, tl: (ce[c], used_j(c, j, nu), 0)),
                pl.BlockSpec((1, 1, MOE_TF),
                             lambda c, j, ce, cs, cn, nu, tl: (ce[c], 0, used_j(c, j, nu))),
                pl.BlockSpec((1, 1, MOE_TF),
                             lambda c, j, ce, cs, cn, nu, tl: (ce[c], 0, nj + used_j(c, j, nu))),
                pl.BlockSpec((1, 1, D_MODEL), lambda c, j, ce, cs, cn, nu, tl: (ce[c], 0, 0)),
            ],
            out_specs=pl.BlockSpec(memory_space=pl.ANY),
            scratch_shapes=[
                pltpu.VMEM((MOE_R, D_MODEL), F32),
                pltpu.VMEM((MOE_R, D_MODEL), BF16),
                pltpu.VMEM((2, MOE_R, D_MODEL), F32),
                pltpu.VMEM((MOE_SUB, D_MODEL), F32),
                pltpu.SemaphoreType.DMA(()),
                pltpu.SemaphoreType.DMA((2,)),
                pltpu.SemaphoreType.DMA(()),
            ],
        ),
        out_shape=jax.ShapeDtypeStruct((MOE_XROWS, D_MODEL), F32),
        compiler_params=pltpu.CompilerParams(dimension_semantics=("arbitrary", "arbitrary"),
                                             vmem_limit_bytes=MOE_VMEM_LIMIT),
        name="moe_experts",
    )(*chunk_tables, x_sorted, w_gate_up, w_gate_up, w_down, b_gate_up, b_gate_up, b_down)


ROUTE_TM = 512


def _route_kernel(lg_ref, tri_ref, e_ref, rank_ref, gate_ref, cnt_ref, seen):
    @pl.when(pl.program_id(0) == 0)
    def _():
        seen[...] = jnp.zeros_like(seen)

    work = lg_ref[0:N_EXPERTS, :]
    eidx = lax.broadcasted_iota(jnp.int32, work.shape, 0)
    vals, idxs, hots = [], [], []
    for _ in range(TOP_K):
        m = jnp.max(work, axis=0, keepdims=True)
        idx = jnp.min(jnp.where(work == m, eidx, N_EXPERTS), axis=0, keepdims=True)
        hot = eidx == idx
        vals.append(m)
        idxs.append(idx)
        hots.append(hot)
        work = jnp.where(hot, -jnp.inf, work)
    ex = [jnp.exp(v - vals[0]) for v in vals]
    inv = 1.0 / (ex[0] + ex[1] + ex[2] + ex[3])
    picked = jnp.zeros(work.shape, F32)
    for hot in hots:
        picked = picked + hot.astype(F32)
    before = jnp.dot(picked.astype(BF16), tri_ref[...], preferred_element_type=F32) + seen[...]
    ranks = [jnp.sum(jnp.where(hot, before, 0.0), axis=0, keepdims=True) for hot in hots]
    seen[...] += jnp.sum(picked, axis=1, keepdims=True)
    e_ref[...] = jnp.concatenate(idxs, axis=0)
    rank_ref[...] = jnp.concatenate(ranks, axis=0).astype(jnp.int32)
    gate_ref[...] = jnp.concatenate([e * inv for e in ex], axis=0)
    cnt_ref[...] = seen[...]


def route_topk(logits_t):
    tm = ROUTE_TM
    tri = (jnp.arange(tm)[:, None] < jnp.arange(tm)[None, :]).astype(BF16)
    pick = pl.BlockSpec((TOP_K, tm), lambda i: (0, i))
    return pl.pallas_call(
        _route_kernel,
        grid=(T_ALL // tm,),
        in_specs=[pl.BlockSpec((ROUTER_PAD, tm), lambda i: (0, i)),
                  pl.BlockSpec((tm, tm), lambda i: (0, 0))],
        out_specs=[pick, pick, pick, pl.BlockSpec((N_EXPERTS, 1), lambda i: (0, 0))],
        out_shape=[jax.ShapeDtypeStruct((TOP_K, T_ALL), jnp.int32),
                   jax.ShapeDtypeStruct((TOP_K, T_ALL), jnp.int32),
                   jax.ShapeDtypeStruct((TOP_K, T_ALL), F32),
                   jax.ShapeDtypeStruct((N_EXPERTS, 1), F32)],
        scratch_shapes=[pltpu.VMEM((N_EXPERTS, 1), F32)],
        compiler_params=_params(("arbitrary",)),
        name="route_topk",
    )(logits_t, tri)


def route(logits_t):
    e_kt, rank, gate_kt, counts_f = route_topk(logits_t)
    counts = counts_f.reshape(N_EXPERTS).astype(jnp.int32)
    nsub = (counts + MOE_SUB - 1) // MOE_SUB
    sub_base = jnp.cumsum(nsub) - nsub
    nch = (counts + MOE_R - 1) // MOE_R
    ch_end = jnp.cumsum(nch)
    ch_base = ch_end - nch
    n_used = ch_end[-1]
    is_e = e_kt[..., None] == jnp.arange(N_EXPERTS, dtype=jnp.int32)
    pos = MOE_SUB * jnp.sum(jnp.where(is_e, sub_base, 0), axis=-1) + rank
    tok = jnp.broadcast_to(jnp.arange(T_ALL, dtype=jnp.int32)[None, :], pos.shape)
    pad_i = jnp.arange(MOE_SUB - 1, dtype=jnp.int32)
    pad_pos = (MOE_SUB * sub_base + counts)[:, None] + pad_i[None, :]
    pad_pos = jnp.where(pad_i[None, :] < (MOE_SUB * nsub - counts)[:, None], pad_pos, MOE_XROWS)
    tail_pos = MOE_SUB * jnp.sum(nsub) + jnp.arange(MOE_XROWS - TK_ALL, dtype=jnp.int32)
    tail_pos = jnp.minimum(tail_pos, MOE_XROWS)
    keys = jnp.concatenate([pos.reshape(-1), pad_pos.reshape(-1), tail_pos]).astype(jnp.int32)
    vals = jnp.concatenate([tok.reshape(-1), jnp.zeros((keys.shape[0] - TK_ALL,), jnp.int32)])
    row_tok = lax.sort_key_val(keys, vals)[1][:MOE_XROWS]
    c = jnp.arange(MOE_NC, dtype=jnp.int32)
    cc = jnp.minimum(c, n_used - 1)
    ce = jnp.minimum(jnp.sum(cc[:, None] >= ch_end[None, :], axis=1), N_EXPERTS - 1).astype(jnp.int32)
    kk = cc - ch_base[ce]
    cs = (sub_base[ce] + MOE_CHUNK_SUBS * kk).astype(jnp.int32)
    cn = jnp.where(c < n_used, jnp.minimum(MOE_CHUNK_SUBS, nsub[ce] - MOE_CHUNK_SUBS * kk), 0).astype(jnp.int32)
    tables = (ce, cs, cn, n_used.reshape(1).astype(jnp.int32), jnp.sum(nsub).reshape(1).astype(jnp.int32))
    return gate_kt.T, row_tok, pos, tables


def _combine_kernel(alpha, h_ref, yg_ref, gate_ref, g_ref, b_ref, o_ref):
    gate = gate_ref[...]
    f = yg_ref[0] * gate[:, 0:1]
    for k in range(1, TOP_K):
        f = f + yg_ref[k] * gate[:, k:k + 1]
    o_ref[...] = _layer_norm(alpha * h_ref[...] + f, g_ref[...], b_ref[...])


def combine(h, yg, gate, ln_g, ln_b, alpha, row_off, m, tm):
    blk_off = row_off // tm
    whole = lambda i: (0, 0)
    return pl.pallas_call(
        functools.partial(_combine_kernel, alpha),
        grid=(m // tm,),
        in_specs=[
            pl.BlockSpec((tm, D_MODEL), lambda i: (blk_off + i, 0)),
            pl.BlockSpec((TOP_K, tm, D_MODEL), lambda i: (0, blk_off + i, 0)),
            pl.BlockSpec((tm, TOP_K), lambda i: (blk_off + i, 0)),
            pl.BlockSpec((1, D_MODEL), whole),
            pl.BlockSpec((1, D_MODEL), whole),
        ],
        out_specs=pl.BlockSpec((tm, D_MODEL), lambda i: (i, 0)),
        out_shape=jax.ShapeDtypeStruct((m, D_MODEL), F32),
        compiler_params=_params(("parallel",)),
        name="combine",
    )(h, yg, gate, ln_g, ln_b)


def _retention_tables(chunk, group):
    log_gamma = jnp.log(1.0 - jnp.exp2(-5.0 - jnp.arange(N_RET_HEADS, dtype=F32)))
    n = chunk * group
    pos = jnp.arange(n) % chunk
    seq = jnp.arange(n) // chunk
    diff = (pos[:, None] - pos[None, :]).astype(F32)
    same = seq[:, None] == seq[None, :]
    lg = log_gamma[:, None, None]
    dec = jnp.where(same[None] & (diff >= 0)[None], jnp.exp(lg * jnp.maximum(diff, 0.0)[None]), 0.0)
    posf = pos.astype(F32)
    qd = jnp.exp(log_gamma[:, None] * (posf + 1.0))[..., None]
    kd = jnp.exp(log_gamma[:, None] * (chunk - 1.0 - posf))[..., None]
    cd = jnp.exp(log_gamma * chunk)
    return cd, dec, qd, kd


def _alibi_slopes():
    h = jnp.arange(1, N_SWA_HEADS + 1, dtype=F32)
    return jnp.exp2(-8.0 * h / N_SWA_HEADS)


def kernel(x_prompt, x_sample, state_ret, cache_swa_k, cache_swa_v, w_in, w_ret_o, w_swa_o, w_out, swa_sinks,
           ln1_g, ln1_b, w_router, b_router, w_gate_up, b_gate_up, w_down, b_down, ln2_g, ln2_b):
    alpha = (2.0 * DEPTH) ** 0.25
    kv_w = N_SWA_KV_HEADS * SWA_HEAD_DIM
    w = w_in[0]
    w_in_bf = w.astype(BF16)
    w_ret_o_bf = w_ret_o[0].astype(BF16)
    w_swa_o_bf = w_swa_o[0].astype(BF16)
    w_out_bf = w_out[0].astype(BF16)
    w_router_pad, b_router_pad = _router_split(w_router[0], b_router[0])
    ln1g, ln1b = ln1_g[0].reshape(1, D_MODEL), ln1_b[0].reshape(1, D_MODEL)
    ln2g, ln2b = ln2_g[0].reshape(1, D_MODEL), ln2_b[0].reshape(1, D_MODEL)
    slopes = _alibi_slopes()
    sinks = swa_sinks[0].astype(F32)

    xp = x_prompt.reshape(T_PROMPT, D_MODEL)
    xs = x_sample.reshape(T_SAMPLE, D_MODEL)

    u_p = in_proj(xp, w_in_bf, 1024, 1280)
    o_r_p, state_p, o_s_p = mixer_prompt(u_p, _retention_tables(RET_CHUNK, 1), slopes, sinks)
    kv_p = u_p.reshape(BATCH, SEQ, D_IN)[:, SEQ - WINDOW:, COL_KVS:COL_KVS + 2 * kv_w]
    k_cache_p = kv_p[..., :kv_w].reshape(1, BATCH, WINDOW, N_SWA_KV_HEADS, SWA_HEAD_DIM)
    v_cache_p = kv_p[..., kv_w:].reshape(1, BATCH, WINDOW, N_SWA_KV_HEADS, SWA_HEAD_DIM)
    merged_p = merge_branches(o_r_p, o_s_p, u_p, w_ret_o_bf, w_swa_o_bf, 512)

    u_s = in_proj(xs, w_in_bf, 512, 1280)
    q_t = (u_s[:, COL_QS:COL_QS + 2048]
           .reshape(DEC_BATCH, DEC_SEQ, N_SWA_KV_HEADS, SWA_GROUP, SWA_HEAD_DIM)
           .transpose(0, 2, 1, 3, 4).reshape(DEC_BATCH, N_SWA_KV_HEADS, SS_ROWS, SWA_HEAD_DIM))
    kv_new = u_s[:, COL_KVS:COL_KVS + 2 * kv_w].reshape(DEC_BATCH, DEC_SEQ, 2 * kv_w)
    o_r_s, state_s, o_s_t, k_cache_s, v_cache_s = mixer_sample(
        u_s, state_ret[0], _retention_tables(DEC_SEQ, RS_BB), q_t, kv_new,
        cache_swa_k[0].reshape(DEC_BATCH, WINDOW, kv_w), cache_swa_v[0].reshape(DEC_BATCH, WINDOW, kv_w),
        _swa_sample_tables(slopes, sinks))
    o_s_s = (o_s_t.reshape(DEC_BATCH, N_SWA_KV_HEADS, DEC_SEQ, SWA_GROUP, SWA_HEAD_DIM)
             .transpose(0, 2, 1, 3, 4).reshape(T_SAMPLE, 2048).astype(BF16))
    merged_s = merge_branches(o_r_s, o_s_s, u_s, w_ret_o_bf, w_swa_o_bf, 512)

    h, logits = post_mix(merged_p, merged_s, xp, xs, w_out_bf, ln1g, ln1b, w_router_pad, b_router_pad,
                         alpha, 512)

    gate, row_tok, pos, tables = route(logits)
    x_sorted = h.at[row_tok].get(mode="promise_in_bounds")
    y_rows = moe_experts(x_sorted, tables, w_gate_up[0], b_gate_up[0].reshape(N_EXPERTS, 1, 2 * D_FF),
                         w_down[0], b_down[0].reshape(N_EXPERTS, 1, D_MODEL))
    yg = y_rows.at[pos].get(mode="promise_in_bounds")
    y_p = combine(h, yg, gate, ln2g, ln2b, alpha, 0, T_PROMPT, 256)
    y_s = combine(h, yg, gate, ln2g, ln2b, alpha, T_PROMPT, T_SAMPLE, 256)

    return (y_p.reshape(BATCH, SEQ, D_MODEL), y_s.reshape(DEC_BATCH, DEC_SEQ, D_MODEL),
            state_p[None], k_cache_p, v_cache_p,
            state_s[None], k_cache_s.reshape(1, DEC_BATCH, WINDOW, N_SWA_KV_HEADS, SWA_HEAD_DIM),
            v_cache_s.reshape(1, DEC_BATCH, WINDOW, N_SWA_KV_HEADS, SWA_HEAD_DIM))
```

```python
import functools

import jax
import jax.numpy as jnp
from jax import lax
from jax.experimental import pallas as pl
from jax.experimental.pallas import tpu as pltpu

F32 = jnp.float32
BF16 = jnp.bfloat16

D_MODEL = 2048
BATCH = 4
SEQ = 2048
DEC_BATCH = 128
DEC_SEQ = 4
N_RET_HEADS = 8
RET_DK = 128
RET_DV = 256
RET_CHUNK = 128
N_SWA_HEADS = 32
N_SWA_KV_HEADS = 4
SWA_GROUP = 8
SWA_HEAD_DIM = 64
WINDOW = 128
N_EXPERTS = 32
TOP_K = 4
D_FF = 2048
SWIGLU_LIMIT = 7.0
SWIGLU_ALPHA = 1.702
LN_EPS = 1e-5
GN_EPS = 1e-6
NEG_INF = -1e30
DEPTH = 1

T_PROMPT = BATCH * SEQ
T_SAMPLE = DEC_BATCH * DEC_SEQ
T_ALL = T_PROMPT + T_SAMPLE
TK_ALL = T_ALL * TOP_K
D_IN = 12800

COL_QR, COL_KR, COL_VR, COL_GR, COL_QS, COL_KVS, COL_GATE_R, COL_GATE_S = (
    0, 1024, 2048, 4096, 6144, 8192, 8704, 10752)

MOE_SUB = 64
MOE_CHUNK_SUBS = 22
MOE_STATIC_SIZES = tuple(range(13, 23))
MOE_R = MOE_SUB * MOE_CHUNK_SUBS
MOE_NSUB_MAX = TK_ALL // MOE_SUB + N_EXPERTS
MOE_XROWS = (MOE_NSUB_MAX + MOE_CHUNK_SUBS - 1) * MOE_SUB
MOE_NC = TK_ALL // MOE_R + N_EXPERTS + 2
MOE_TF = 256
MOE_NJ = D_FF // MOE_TF
MOE_VMEM_LIMIT = 60 * 1024 * 1024

VMEM_LIMIT = 56 * 1024 * 1024


def _params(sem):
    return pltpu.CompilerParams(dimension_semantics=sem, vmem_limit_bytes=VMEM_LIMIT)


def _in_proj_kernel(x_ref, w_ref, o_ref):
    o_ref[...] = jnp.dot(x_ref[...].astype(BF16), w_ref[...], preferred_element_type=F32)


def in_proj(x2d, w_bf, tm, tn):
    m, k = x2d.shape
    n = w_bf.shape[1]
    return pl.pallas_call(
        _in_proj_kernel,
        grid=(m // tm, n // tn),
        in_specs=[pl.BlockSpec((tm, k), lambda i, j: (i, 0)),
                  pl.BlockSpec((k, tn), lambda i, j: (0, j))],
        out_specs=pl.BlockSpec((tm, tn), lambda i, j: (i, j)),
        out_shape=jax.ShapeDtypeStruct((m, n), F32),
        compiler_params=_params(("parallel", "parallel")),
        name="in_proj",
    )(x2d, w_bf)


def _group_norm_gate(o, g):
    mu = jnp.mean(o, axis=-1, keepdims=True)
    oc = o - mu
    var = jnp.mean(oc * oc, axis=-1, keepdims=True)
    return oc * lax.rsqrt(var + GN_EPS) * (g * jax.nn.sigmoid(g))


def _dot_nt(a, b):
    return lax.dot_general(a, b, (((1,), (1,)), ((), ())), preferred_element_type=F32)


def _layer_norm(x, g, b):
    mu = jnp.mean(x, axis=-1, keepdims=True)
    xc = x - mu
    var = jnp.mean(xc * xc, axis=-1, keepdims=True)
    return xc * lax.rsqrt(var + LN_EPS) * g + b


def _ret_prompt_kernel(cd_ref, dec_ref, qd_ref, kd_ref, q_ref, k_ref, v_ref, g_ref, o_ref, st_ref):
    c = pl.program_id(1)

    @pl.when(c == 0)
    def _():
        st_ref[...] = jnp.zeros_like(st_ref)

    heads = range(N_RET_HEADS)
    q = jnp.stack([q_ref[:, h * RET_DK:(h + 1) * RET_DK] for h in heads])
    k = jnp.stack([k_ref[:, h * RET_DK:(h + 1) * RET_DK] for h in heads]) * (RET_DK ** -0.5)
    v = jnp.stack([v_ref[:, h * RET_DV:(h + 1) * RET_DV] for h in heads]).astype(BF16)
    g = jnp.stack([g_ref[:, h * RET_DV:(h + 1) * RET_DV] for h in heads])
    state = st_ref[0]
    bmm = functools.partial(lax.dot_general, preferred_element_type=F32)
    scores = bmm(q.astype(BF16), k.astype(BF16), (((2,), (2,)), ((0,), (0,)))) * dec_ref[...]
    o = (bmm(scores.astype(BF16), v, (((2,), (1,)), ((0,), (0,))))
         + bmm((q * qd_ref[...]).astype(BF16), state.astype(BF16), (((2,), (1,)), ((0,), (0,)))))
    kt = jnp.swapaxes(k * kd_ref[...], 1, 2).astype(BF16)
    cd = jnp.stack([jnp.full((1, 1), cd_ref[h], F32) for h in heads])
    st_ref[0] = cd * state + bmm(kt, v, (((2,), (1,)), ((0,), (0,))))
    out = _group_norm_gate(o, g).astype(BF16)
    for h in heads:
        o_ref[:, h * RET_DV:(h + 1) * RET_DV] = out[h]


RS_BB = 32


def _ret_sample_kernel(cd_ref, dec_ref, qd_ref, kd_ref, q_ref, k_ref, v_ref, g_ref, st_ref,
                       o_ref, sto_ref):
    h = pl.program_id(1)
    q = q_ref[...]
    k = k_ref[...] * (RET_DK ** -0.5)
    v = v_ref[...].astype(BF16)
    scores = _dot_nt(q.astype(BF16), k.astype(BF16)) * dec_ref[0]
    o_intra = jnp.dot(scores.astype(BF16), v, preferred_element_type=F32)
    qq = (q * qd_ref[0]).astype(BF16)
    kt = (k * kd_ref[0]).T
    col_batch = lax.broadcasted_iota(jnp.int32, kt.shape, 1) // DEC_SEQ
    first_of_pair = lax.broadcasted_iota(jnp.int32, (8, RET_DV), 0) < DEC_SEQ
    cd = cd_ref[h]
    pieces = []
    for p in range(RS_BB // 2):
        q8 = qq[8 * p:8 * p + 8]
        s0 = st_ref[2 * p].astype(BF16)
        s1 = st_ref[2 * p + 1].astype(BF16)
        r0 = jnp.dot(q8, s0, preferred_element_type=F32)
        r1 = jnp.dot(q8, s1, preferred_element_type=F32)
        pieces.append(jnp.where(first_of_pair, r0, r1))
    for b in range(RS_BB):
        ktm = jnp.where(col_batch == b, kt, 0.0).astype(BF16)
        sto_ref[b] = cd * st_ref[b] + jnp.dot(ktm, v, preferred_element_type=F32)
    o = o_intra + jnp.concatenate(pieces, axis=0)
    o_ref[...] = _group_norm_gate(o, g_ref[...]).astype(BF16)


def _swa_prompt_kernel(sink_ref, bias_ref, q_ref, kvc_ref, kvp_ref, o_ref):
    kv_w = N_SWA_KV_HEADS * SWA_HEAD_DIM
    hd = SWA_HEAD_DIM
    qt = q_ref[...].T
    outs = []
    for kvh in range(N_SWA_KV_HEADS):
        lo = kvh * hd
        k_cat = jnp.concatenate([kvp_ref[:, lo:lo + hd], kvc_ref[:, lo:lo + hd]], axis=0).astype(BF16)
        v_cat = jnp.concatenate([kvp_ref[:, kv_w + lo:kv_w + lo + hd],
                                 kvc_ref[:, kv_w + lo:kv_w + lo + hd]], axis=0)
        vt = v_cat.T.astype(BF16)
        heads = range(kvh * SWA_GROUP, (kvh + 1) * SWA_GROUP)
        q8 = jnp.concatenate([qt[h * hd:(h + 1) * hd, :] for h in heads], axis=1)
        q8 = (q8 * (hd ** -0.5)).astype(BF16)
        s = jnp.dot(k_cat, q8, preferred_element_type=F32) + bias_ref[0, kvh]
        sink = sink_ref[kvh]
        m = jnp.maximum(jnp.max(s, axis=0, keepdims=True), sink)
        e = jnp.exp(s - m)
        den = jnp.sum(e, axis=0, keepdims=True) + jnp.exp(sink - m)
        p = (e * (1.0 / den)).astype(BF16)
        ot = jnp.dot(vt, p, preferred_element_type=F32)
        outs += [ot[:, g * WINDOW:(g + 1) * WINDOW] for g in range(SWA_GROUP)]
    o_ref[...] = jnp.concatenate(outs, axis=0).T.astype(BF16)


def _swa_prompt_tables(slopes, sinks):
    qi = jnp.arange(WINDOW)[:, None] + WINDOW
    ki = jnp.arange(2 * WINDOW)[None, :]
    dist = qi - ki
    in_window = (dist >= 0) & (dist < WINDOW)
    valid = jnp.stack([in_window & (ki >= WINDOW), in_window])
    pen = -(slopes[:, None, None] * dist.astype(F32)[None])
    bias = jnp.where(valid[:, None], pen[None], NEG_INF)
    bias = bias.reshape(2, N_SWA_KV_HEADS, SWA_GROUP, WINDOW, 2 * WINDOW).transpose(0, 1, 4, 2, 3)
    bias = bias.reshape(2, N_SWA_KV_HEADS, 2 * WINDOW, SWA_GROUP * WINDOW)
    sink_l = jnp.repeat(sinks.reshape(N_SWA_KV_HEADS, SWA_GROUP), WINDOW, axis=1)
    return sink_l.reshape(N_SWA_KV_HEADS, 1, SWA_GROUP * WINDOW), bias


def _mixer_prompt_kernel(cd_ref, dec_ref, qd_ref, kd_ref, sink_ref, bias_ref, qr_ref, kr_ref, vr_ref, gr_ref,
                         qs_ref, kvc_ref, kvp_ref, or_ref, st_ref, os_ref):
    _ret_prompt_kernel(cd_ref, dec_ref, qd_ref, kd_ref, qr_ref, kr_ref, vr_ref, gr_ref, or_ref, st_ref)
    _swa_prompt_kernel(sink_ref, bias_ref, qs_ref, kvc_ref, kvp_ref, os_ref)


def mixer_prompt(u, ret_tables, slopes, sinks):
    cd, dec, qd, kd = ret_tables
    sink_l, bias = _swa_prompt_tables(slopes, sinks)
    nb = SEQ // WINDOW
    lanes = SWA_GROUP * WINDOW
    kv_blk = COL_KVS // 512
    row = lambda b, c: b * nb + c
    whole3 = lambda b, c: (0, 0, 0)
    return pl.pallas_call(
        _mixer_prompt_kernel,
        grid=(BATCH, nb),
        in_specs=[
            pl.BlockSpec(memory_space=pltpu.SMEM),
            pl.BlockSpec((N_RET_HEADS, RET_CHUNK, RET_CHUNK), whole3),
            pl.BlockSpec((N_RET_HEADS, RET_CHUNK, 1), whole3),
            pl.BlockSpec((N_RET_HEADS, RET_CHUNK, 1), whole3),
            pl.BlockSpec((N_SWA_KV_HEADS, 1, lanes), whole3),
            pl.BlockSpec((1, N_SWA_KV_HEADS, 2 * WINDOW, lanes), lambda b, c: (jnp.minimum(c, 1), 0, 0, 0)),
            pl.BlockSpec((RET_CHUNK, 1024), lambda b, c: (row(b, c), COL_QR // 1024)),
            pl.BlockSpec((RET_CHUNK, 1024), lambda b, c: (row(b, c), COL_KR // 1024)),
            pl.BlockSpec((RET_CHUNK, 2048), lambda b, c: (row(b, c), COL_VR // 2048)),
            pl.BlockSpec((RET_CHUNK, 2048), lambda b, c: (row(b, c), COL_GR // 2048)),
            pl.BlockSpec((WINDOW, 2048), lambda b, c: (row(b, c), COL_QS // 2048)),
            pl.BlockSpec((WINDOW, 512), lambda b, c: (row(b, c), kv_blk)),
            pl.BlockSpec((WINDOW, 512), lambda b, c: (b * nb + jnp.maximum(c - 1, 0), kv_blk)),
        ],
        out_specs=[
            pl.BlockSpec((RET_CHUNK, 2048), lambda b, c: (row(b, c), 0)),
            pl.BlockSpec((1, N_RET_HEADS, RET_DK, RET_DV), lambda b, c: (b, 0, 0, 0)),
            pl.BlockSpec((WINDOW, 2048), lambda b, c: (row(b, c), 0)),
        ],
        out_shape=[
            jax.ShapeDtypeStruct((T_PROMPT, 2048), BF16),
            jax.ShapeDtypeStruct((BATCH, N_RET_HEADS, RET_DK, RET_DV), F32),
            jax.ShapeDtypeStruct((T_PROMPT, 2048), BF16),
        ],
        compiler_params=_params(("parallel", "arbitrary")),
        name="mixer_prompt",
    )(cd, dec, qd, kd, sink_l, bias, u, u, u, u, u, u, u)


SS_BB = RS_BB // N_RET_HEADS
SS_ROWS = DEC_SEQ * SWA_GROUP


def _bf16_round(x):
    return x.astype(BF16).astype(F32)


def _swa_sample_kernel(sink_ref, bias1_ref, bias2_ref, q_ref, kvn_ref, ck_ref, cv_ref,
                       o_ref, cko_ref, cvo_ref):
    kv_w = N_SWA_KV_HEADS * SWA_HEAD_DIM
    hd = SWA_HEAD_DIM
    scale = hd ** -0.5
    bb = q_ref.shape[0]
    heads = range(N_SWA_KV_HEADS)

    def per_pair(f):
        return jnp.concatenate([f(kvh) for kvh in heads], axis=0)

    def per_head_table(t):
        return per_pair(lambda kvh: jnp.broadcast_to(t[kvh][None], (bb,) + t.shape[1:]))

    q = per_pair(lambda kvh: q_ref[:, kvh])
    ck = per_pair(lambda kvh: ck_ref[:, :, kvh * hd:(kvh + 1) * hd]).astype(BF16)
    cv = per_pair(lambda kvh: cv_ref[:, :, kvh * hd:(kvh + 1) * hd]).astype(BF16)
    kn = _bf16_round(per_pair(lambda kvh: kvn_ref[:, :, kvh * hd:(kvh + 1) * hd]))
    vn = _bf16_round(per_pair(lambda kvh: kvn_ref[:, :, kv_w + kvh * hd:kv_w + (kvh + 1) * hd]))
    sink = per_head_table(sink_ref[...])
    s1 = lax.dot_general(q.astype(BF16), ck, (((2,), (2,)), ((0,), (0,))),
                         preferred_element_type=F32) * scale + per_head_table(bias1_ref[...])
    qr = _bf16_round(q)
    s2 = [jnp.sum(qr * kn[:, t2:t2 + 1, :], axis=-1, keepdims=True) * scale
          + per_head_table(bias2_ref[:, t2]) for t2 in range(DEC_SEQ)]
    m = jnp.maximum(jnp.max(s1, axis=-1, keepdims=True), sink)
    for st in s2:
        m = jnp.maximum(m, st)
    e1 = jnp.exp(s1 - m)
    e2 = [jnp.exp(st - m) for st in s2]
    den = jnp.sum(e1, axis=-1, keepdims=True) + jnp.exp(sink - m)
    for et in e2:
        den = den + et
    inv = 1.0 / den
    o = lax.dot_general((e1 * inv).astype(BF16), cv, (((2,), (1,)), ((0,), (0,))),
                        preferred_element_type=F32)
    for t2 in range(DEC_SEQ):
        o = o + _bf16_round(e2[t2] * inv) * vn[:, t2:t2 + 1, :]
    for kvh in heads:
        o_ref[:, kvh] = o[kvh * bb:(kvh + 1) * bb]
    keep = WINDOW - DEC_SEQ
    cko_ref[:, 0:keep, :] = ck_ref[:, DEC_SEQ:WINDOW, :]
    cko_ref[:, keep:WINDOW, :] = kvn_ref[:, :, 0:kv_w]
    cvo_ref[:, 0:keep, :] = cv_ref[:, DEC_SEQ:WINDOW, :]
    cvo_ref[:, keep:WINDOW, :] = kvn_ref[:, :, kv_w:2 * kv_w]


def _swa_sample_tables(slopes, sinks):
    g_slopes = jnp.tile(slopes.reshape(N_SWA_KV_HEADS, 1, SWA_GROUP), (1, DEC_SEQ, 1)).reshape(
        N_SWA_KV_HEADS, SS_ROWS)
    sink_rows = jnp.tile(sinks.reshape(N_SWA_KV_HEADS, 1, SWA_GROUP), (1, DEC_SEQ, 1)).reshape(
        N_SWA_KV_HEADS, SS_ROWS, 1)
    t_row = jnp.arange(SS_ROWS) // SWA_GROUP
    jcol = jnp.arange(WINDOW)
    dist1 = (t_row[:, None] + WINDOW - jcol[None, :])
    bias1 = jnp.where((dist1 < WINDOW)[None], -(g_slopes[:, :, None] * dist1.astype(F32)[None]), NEG_INF)
    t2 = jnp.arange(DEC_SEQ)
    dist2 = t_row[None, :] - t2[:, None]
    bias2 = jnp.where((dist2 >= 0)[None], -(g_slopes[:, None, :] * dist2.astype(F32)[None]), NEG_INF)
    return sink_rows, bias1, bias2[..., None]


def _mixer_sample_kernel(cd_ref, dec_ref, qd_ref, kd_ref, qr_ref, kr_ref, vr_ref, gr_ref, st_ref,
                         sink_ref, bias1_ref, bias2_ref, qs_ref, kvn_ref, ck_ref, cv_ref,
                         or_ref, sto_ref, os_ref, cko_ref, cvo_ref):
    _ret_sample_kernel(cd_ref, dec_ref, qd_ref, kd_ref, qr_ref, kr_ref, vr_ref, gr_ref, st_ref, or_ref, sto_ref)
    _swa_sample_kernel(sink_ref, bias1_ref, bias2_ref, qs_ref, kvn_ref, ck_ref, cv_ref, os_ref, cko_ref, cvo_ref)


def mixer_sample(u_s, state, ret_tables, q_t, kv_new, cache_k, cache_v, swa_tables):
    cd, dec, qd, kd = ret_tables
    sink_rows, bias1, bias2 = swa_tables
    kv_w = N_SWA_KV_HEADS * SWA_HEAD_DIM
    rows = RS_BB * DEC_SEQ
    per_head = lambda i, h: (h, 0, 0)
    whole3 = lambda i, h: (0, 0, 0)
    blk = lambda i, h: i * N_RET_HEADS + h
    st_spec = pl.BlockSpec((RS_BB, None, RET_DK, RET_DV), lambda i, h: (i, h, 0, 0))
    cache_spec = pl.BlockSpec((SS_BB, WINDOW, kv_w), lambda i, h: (blk(i, h), 0, 0))
    q_spec = pl.BlockSpec((SS_BB, N_SWA_KV_HEADS, SS_ROWS, SWA_HEAD_DIM), lambda i, h: (blk(i, h), 0, 0, 0))
    return pl.pallas_call(
        _mixer_sample_kernel,
        grid=(DEC_BATCH // RS_BB, N_RET_HEADS),
        in_specs=[
            pl.BlockSpec(memory_space=pltpu.SMEM),
            pl.BlockSpec((1, rows, rows), per_head),
            pl.BlockSpec((1, rows, 1), per_head),
            pl.BlockSpec((1, rows, 1), per_head),
            pl.BlockSpec((rows, RET_DK), lambda i, h: (i, COL_QR // RET_DK + h)),
            pl.BlockSpec((rows, RET_DK), lambda i, h: (i, COL_KR // RET_DK + h)),
            pl.BlockSpec((rows, RET_DV), lambda i, h: (i, COL_VR // RET_DV + h)),
            pl.BlockSpec((rows, RET_DV), lambda i, h: (i, COL_GR // RET_DV + h)),
            st_spec,
            pl.BlockSpec((N_SWA_KV_HEADS, SS_ROWS, 1), whole3),
            pl.BlockSpec((N_SWA_KV_HEADS, SS_ROWS, WINDOW), whole3),
            pl.BlockSpec((N_SWA_KV_HEADS, DEC_SEQ, SS_ROWS, 1), lambda i, h: (0, 0, 0, 0)),
            q_spec,
            pl.BlockSpec((SS_BB, DEC_SEQ, 2 * kv_w), lambda i, h: (blk(i, h), 0, 0)),
            cache_spec, cache_spec,
        ],
        out_specs=[pl.BlockSpec((rows, RET_DV), lambda i, h: (i, h)), st_spec, q_spec, cache_spec, cache_spec],
        out_shape=[
            jax.ShapeDtypeStruct((T_SAMPLE, 2048), BF16),
            jax.ShapeDtypeStruct((DEC_BATCH, N_RET_HEADS, RET_DK, RET_DV), F32),
            jax.ShapeDtypeStruct((DEC_BATCH, N_SWA_KV_HEADS, SS_ROWS, SWA_HEAD_DIM), F32),
            jax.ShapeDtypeStruct((DEC_BATCH, WINDOW, kv_w), F32),
            jax.ShapeDtypeStruct((DEC_BATCH, WINDOW, kv_w), F32),
        ],
        compiler_params=_params(("parallel", "parallel")),
        name="mixer_sample",
    )(cd, dec, qd, kd, u_s, u_s, u_s, u_s, state, sink_rows, bias1, bias2, q_t, kv_new, cache_k, cache_v)


def _merge_kernel(or_ref, os_ref, gr_ref, gs_ref, wr_ref, ws_ref, o_ref):
    a = jnp.dot(or_ref[...], wr_ref[...], preferred_element_type=F32)
    b = jnp.dot(os_ref[...], ws_ref[...], preferred_element_type=F32)
    merged = jax.nn.sigmoid(gr_ref[...]) * a + jax.nn.sigmoid(gs_ref[...]) * b
    o_ref[...] = merged.astype(BF16)


def merge_branches(o_r, o_s, u, w_ret_o_bf, w_swa_o_bf, tm):
    m = o_r.shape[0]
    resident = pl.BlockSpec((2048, D_MODEL), lambda i: (0, 0), pipeline_mode=pl.Buffered(1))
    return pl.pallas_call(
        _merge_kernel,
        grid=(m // tm,),
        in_specs=[
            pl.BlockSpec((tm, 2048), lambda i: (i, 0)),
            pl.BlockSpec((tm, 2048), lambda i: (i, 0)),
            pl.BlockSpec((pl.Element(tm), pl.Element(D_MODEL)), lambda i: (i * tm, COL_GATE_R)),
            pl.BlockSpec((pl.Element(tm), pl.Element(D_MODEL)), lambda i: (i * tm, COL_GATE_S)),
            resident, resident,
        ],
        out_specs=pl.BlockSpec((tm, D_MODEL), lambda i: (i, 0)),
        out_shape=jax.ShapeDtypeStruct((m, D_MODEL), BF16),
        compiler_params=_params(("parallel",)),
        name="merge_branches",
    )(o_r, o_s, u, u, w_ret_o_bf, w_swa_o_bf)


ROUTER_PAD = 128


def _post_mix_kernel(alpha, n_prompt_blocks, mp_ref, ms_ref, xp_ref, xs_ref, w_ref, g_ref, b_ref, wr_ref, br_ref,
                     h_ref, lg_ref):
    from_sample = pl.program_id(0) >= n_prompt_blocks
    merged = jnp.where(from_sample, ms_ref[...], mp_ref[...])
    x = jnp.where(from_sample, xs_ref[...], xp_ref[...])
    mix = jnp.dot(merged, w_ref[...], preferred_element_type=F32)
    h = _layer_norm(alpha * x + mix, g_ref[...], b_ref[...])
    h_ref[...] = h
    h_hi = h.astype(BF16)
    h_lo = (h - h_hi.astype(F32)).astype(BF16)
    hh = jnp.dot(h_hi, wr_ref[...], preferred_element_type=F32)
    lh = jnp.dot(h_lo, wr_ref[:, :ROUTER_PAD], preferred_element_type=F32)
    lg_ref[...] = (hh[:, :ROUTER_PAD] + hh[:, ROUTER_PAD:] + lh + br_ref[...]).T


def _router_split(w_router, b_router):
    w = jnp.pad(w_router, ((0, 0), (0, ROUTER_PAD - N_EXPERTS)))
    w_hi = w.astype(BF16)
    w_lo = (w - w_hi.astype(F32)).astype(BF16)
    b = jnp.pad(b_router, (0, ROUTER_PAD - N_EXPERTS)).reshape(1, ROUTER_PAD)
    return jnp.concatenate([w_hi, w_lo], axis=1), b


def post_mix(merged_p, merged_s, xp, xs, w_out_bf, ln_g, ln_b, w_router_pad, b_router_pad, alpha, tm):
    n_p = T_PROMPT // tm
    n_s = T_SAMPLE // tm
    prompt_row = lambda i: (jnp.minimum(i, n_p - 1), 0)
    sample_row = lambda i: (jnp.clip(i - n_p, 0, n_s - 1), 0)
    whole = lambda i: (0, 0)
    row = lambda i: (i, 0)
    return pl.pallas_call(
        functools.partial(_post_mix_kernel, alpha, n_p),
        grid=(n_p + n_s,),
        in_specs=[
            pl.BlockSpec((tm, 2048), prompt_row),
            pl.BlockSpec((tm, 2048), sample_row),
            pl.BlockSpec((tm, 2048), prompt_row),
            pl.BlockSpec((tm, 2048), sample_row),
            pl.BlockSpec((2048, 2048), whole),
            pl.BlockSpec((1, 2048), whole),
            pl.BlockSpec((1, 2048), whole),
            pl.BlockSpec((2048, 2 * ROUTER_PAD), whole),
            pl.BlockSpec((1, ROUTER_PAD), whole),
        ],
        out_specs=[pl.BlockSpec((tm, 2048), row), pl.BlockSpec((ROUTER_PAD, tm), lambda i: (0, i))],
        out_shape=[jax.ShapeDtypeStruct((T_ALL, D_MODEL), F32),
                   jax.ShapeDtypeStruct((ROUTER_PAD, T_ALL), F32)],
        compiler_params=_params(("parallel",)),
        name="post_mix",
    )(merged_p, merged_s, xp, xs, w_out_bf, ln_g, ln_b, w_router_pad, b_router_pad)


def _moe_kernel(ce_ref, cs_ref, cn_ref, nu_ref, tail_ref, x_hbm, wg_ref, wu_ref, wd_ref, bg_ref, bu_ref, bd_ref,
                y_hbm, x_f32, x_bf, y_acc, zero_buf, x_sem, y_sem, z_sem):
    c = pl.program_id(0)
    j = pl.program_id(1)
    nblk = cn_ref[c]
    slot = c % 2
    y_cur = y_acc.at[slot]

    def sub_rows(r):
        return pl.ds(pl.multiple_of(r * MOE_SUB, MOE_SUB), MOE_SUB)

    def x_copy(chunk):
        start = pl.multiple_of(cs_ref[chunk] * MOE_SUB, MOE_SUB)
        return pltpu.make_async_copy(x_hbm.at[pl.ds(start, MOE_R), :], x_f32, x_sem)

    def y_copy(r, sub):
        return pltpu.make_async_copy(y_cur.at[sub_rows(r), :], y_hbm.at[sub_rows(sub), :], y_sem.at[slot])

    def zero_copy(sub):
        return pltpu.make_async_copy(zero_buf, y_hbm.at[sub_rows(sub), :], z_sem)

    @pl.when(j == 0)
    def _():
        def drain(i, carry):
            y_copy(0, 0).wait()
            return carry

        lax.fori_loop(0, cn_ref[jnp.maximum(c - 2, 0)] * (c >= 2).astype(jnp.int32), drain, 0)

    @pl.when((nblk > 0) & (j == 0))
    def _():
        @pl.when(c == 0)
        def _():
            x_copy(0).start()

        x_copy(c).wait()

        def stage(r, carry):
            rows = sub_rows(r)
            x_bf[rows, :] = x_f32[rows, :].astype(BF16)
            y_cur[rows, :] = jnp.broadcast_to(bd_ref[0], (MOE_SUB, D_MODEL))
            return carry

        lax.fori_loop(0, nblk, stage, 0)

        @pl.when(c + 1 < nu_ref[0])
        def _():
            x_copy(c + 1).start()

    @pl.when(nblk > 0)
    def _():
        def rows_block(rows):
            xr = x_bf[rows, :]
            g = jnp.dot(xr, wg_ref[0], preferred_element_type=F32) + bg_ref[0]
            up = jnp.dot(xr, wu_ref[0], preferred_element_type=F32) + bu_ref[0]
            g = jnp.minimum(g, SWIGLU_LIMIT)
            up = jnp.clip(up, -SWIGLU_LIMIT, SWIGLU_LIMIT)
            glu = g * jax.nn.sigmoid(SWIGLU_ALPHA * g)
            act = ((up + 1.0) * glu).astype(BF16)
            y_cur[rows, :] += jnp.dot(act, wd_ref[0], preferred_element_type=F32)

        for n_static in MOE_STATIC_SIZES:
            @pl.when(nblk == n_static)
            def _(n_static=n_static):
                rows_block(pl.ds(0, n_static * MOE_SUB))

        @pl.when(nblk < MOE_STATIC_SIZES[0])
        def _():
            def one(r, carry):
                rows_block(sub_rows(r))
                return carry

            lax.fori_loop(0, nblk, one, 0)

    @pl.when((nblk > 0) & (j == MOE_NJ - 1))
    def _():
        def write_back(r, carry):
            y_copy(r, cs_ref[c] + r).start()
            return carry

        lax.fori_loop(0, nblk, write_back, 0)

    @pl.when((nblk == 0) & (j == 0))
    def _():
        first = tail_ref[0] + MOE_CHUNK_SUBS * (c - nu_ref[0])
        count = jnp.clip(MOE_XROWS // MOE_SUB - first, 0, MOE_CHUNK_SUBS)

        @pl.when(count > 0)
        def _():
            zero_buf[...] = jnp.zeros_like(zero_buf)

            def start(i, carry):
                zero_copy(first + i).start()
                return carry

            def wait(i, carry):
                zero_copy(first + i).wait()
                return carry

            lax.fori_loop(0, count, start, 0)
            lax.fori_loop(0, count, wait, 0)


def moe_experts(x_sorted, chunk_tables, w_gate_up, b_gate_up, w_down, b_down):
    nj = MOE_NJ

    def used_j(c, j, nu):
        return jnp.where(c < nu[0], j, nj - 1)

    return pl.pallas_call(
        _moe_kernel,
        grid_spec=pltpu.PrefetchScalarGridSpec(
            num_scalar_prefetch=5,
            grid=(MOE_NC, nj),
            in_specs=[
                pl.BlockSpec(memory_space=pl.ANY),
                pl.BlockSpec((1, D_MODEL, MOE_TF),
                             lambda c, j, ce, cs, cn, nu, tl: (ce[c], 0, used_j(c, j, nu))),
                pl.BlockSpec((1, D_MODEL, MOE_TF),
                             lambda c, j, ce, cs, cn, nu, tl: (ce[c], 0, nj + used_j(c, j, nu))),
                pl.BlockSpec((1, MOE_TF, D_MODEL),
                             lambda c, j, ce, cs, cn, nu, tl: (ce[c], used_j(c, j, nu), 0)),
                pl.BlockSpec((1, 1, MOE_TF),
                             lambda c, j, ce, cs, cn, nu, tl: (ce[c], 0, used_j(c, j, nu))),
                pl.BlockSpec((1, 1, MOE_TF),
                             lambda c, j, ce, cs, cn, nu, tl: (ce[c], 0, nj + used_j(c, j, nu))),
                pl.BlockSpec((1, 1, D_MODEL), lambda c, j, ce, cs, cn, nu, tl: (ce[c], 0, 0)),
            ],
            out_specs=pl.BlockSpec(memory_space=pl.ANY),
            scratch_shapes=[
                pltpu.VMEM((MOE_R, D_MODEL), F32),
                pltpu.VMEM((MOE_R, D_MODEL), BF16),
                pltpu.VMEM((2, MOE_R, D_MODEL), F32),
                pltpu.VMEM((MOE_SUB, D_MODEL), F32),
                pltpu.SemaphoreType.DMA(()),
                pltpu.SemaphoreType.DMA((2,)),
                pltpu.SemaphoreType.DMA(()),
            ],
        ),
        out_shape=jax.ShapeDtypeStruct((MOE_XROWS, D_MODEL), F32),
        compiler_params=pltpu.CompilerParams(dimension_semantics=("arbitrary", "arbitrary"),
                                             vmem_limit_bytes=MOE_VMEM_LIMIT),
        name="moe_experts",
    )(*chunk_tables, x_sorted, w_gate_up, w_gate_up, w_down, b_gate_up, b_gate_up, b_down)


ROUTE_TM = 512


def _route_kernel(lg_ref, tri_ref, e_ref, rank_ref, gate_ref, cnt_ref, seen):
    @pl.when(pl.program_id(0) == 0)
    def _():
        seen[...] = jnp.zeros_like(seen)

    work = lg_ref[0:N_EXPERTS, :]
    eidx = lax.broadcasted_iota(jnp.int32, work.shape, 0)
    vals, idxs, hots = [], [], []
    for _ in range(TOP_K):
        m = jnp.max(work, axis=0, keepdims=True)
        idx = jnp.min(jnp.where(work == m, eidx, N_EXPERTS), axis=0, keepdims=True)
        hot = eidx == idx
        vals.append(m)
        idxs.append(idx)
        hots.append(hot)
        work = jnp.where(hot, -jnp.inf, work)
    ex = [jnp.exp(v - vals[0]) for v in vals]
    inv = 1.0 / (ex[0] + ex[1] + ex[2] + ex[3])
    picked = jnp.zeros(work.shape, F32)
    for hot in hots:
        picked = picked + hot.astype(F32)
    before = jnp.dot(picked.astype(BF16), tri_ref[...], preferred_element_type=F32) + seen[...]
    ranks = [jnp.sum(jnp.where(hot, before, 0.0), axis=0, keepdims=True) for hot in hots]
    seen[...] += jnp.sum(picked, axis=1, keepdims=True)
    e_ref[...] = jnp.concatenate(idxs, axis=0)
    rank_ref[...] = jnp.concatenate(ranks, axis=0).astype(jnp.int32)
    gate_ref[...] = jnp.concatenate([e * inv for e in ex], axis=0)
    cnt_ref[...] = seen[...]


def route_topk(logits_t):
    tm = ROUTE_TM
    tri = (jnp.arange(tm)[:, None] < jnp.arange(tm)[None, :]).astype(BF16)
    pick = pl.BlockSpec((TOP_K, tm), lambda i: (0, i))
    return pl.pallas_call(
        _route_kernel,
        grid=(T_ALL // tm,),
        in_specs=[pl.BlockSpec((ROUTER_PAD, tm), lambda i: (0, i)),
                  pl.BlockSpec((tm, tm), lambda i: (0, 0))],
        out_specs=[pick, pick, pick, pl.BlockSpec((N_EXPERTS, 1), lambda i: (0, 0))],
        out_shape=[jax.ShapeDtypeStruct((TOP_K, T_ALL), jnp.int32),
                   jax.ShapeDtypeStruct((TOP_K, T_ALL), jnp.int32),
                   jax.ShapeDtypeStruct((TOP_K, T_ALL), F32),
                   jax.ShapeDtypeStruct((N_EXPERTS, 1), F32)],
        scratch_shapes=[pltpu.VMEM((N_EXPERTS, 1), F32)],
        compiler_params=_params(("arbitrary",)),
        name="route_topk",
    )(logits_t, tri)


def route(logits_t):
    e_kt, rank, gate_kt, counts_f = route_topk(logits_t)
    counts = counts_f.reshape(N_EXPERTS).astype(jnp.int32)
    nsub = (counts + MOE_SUB - 1) // MOE_SUB
    sub_base = jnp.cumsum(nsub) - nsub
    nch = (counts + MOE_R - 1) // MOE_R
    ch_end = jnp.cumsum(nch)
    ch_base = ch_end - nch
    n_used = ch_end[-1]
    is_e = e_kt[..., None] == jnp.arange(N_EXPERTS, dtype=jnp.int32)
    pos = MOE_SUB * jnp.sum(jnp.where(is_e, sub_base, 0), axis=-1) + rank
    tok = jnp.broadcast_to(jnp.arange(T_ALL, dtype=jnp.int32)[None, :], pos.shape)
    pad_i = jnp.arange(MOE_SUB - 1, dtype=jnp.int32)
    pad_pos = (MOE_SUB * sub_base + counts)[:, None] + pad_i[None, :]
    pad_pos = jnp.where(pad_i[None, :] < (MOE_SUB * nsub - counts)[:, None], pad_pos, MOE_XROWS)
    tail_pos = MOE_SUB * jnp.sum(nsub) + jnp.arange(MOE_XROWS - TK_ALL, dtype=jnp.int32)
    tail_pos = jnp.minimum(tail_pos, MOE_XROWS)
    keys = jnp.concatenate([pos.reshape(-1), pad_pos.reshape(-1), tail_pos]).astype(jnp.int32)
    vals = jnp.concatenate([tok.reshape(-1), jnp.zeros((keys.shape[0] - TK_ALL,), jnp.int32)])
    row_tok = lax.sort_key_val(keys, vals)[1][:MOE_XROWS]
    c = jnp.arange(MOE_NC, dtype=jnp.int32)
    cc = jnp.minimum(c, n_used - 1)
    ce = jnp.minimum(jnp.sum(cc[:, None] >= ch_end[None, :], axis=1), N_EXPERTS - 1).astype(jnp.int32)
    kk = cc - ch_base[ce]
    cs = (sub_base[ce] + MOE_CHUNK_SUBS * kk).astype(jnp.int32)
    cn = jnp.where(c < n_used, jnp.minimum(MOE_CHUNK_SUBS, nsub[ce] - MOE_CHUNK_SUBS * kk), 0).astype(jnp.int32)
    tables = (ce, cs, cn, n_used.reshape(1).astype(jnp.int32), jnp.sum(nsub).reshape(1).astype(jnp.int32))
    return gate_kt.T, row_tok, pos, tables


COMBINE_HALF = TOP_K // 2


def _weighted_rows(yg_ref, gate, k0):
    f = yg_ref[0] * gate[:, k0:k0 + 1]
    for k in range(1, COMBINE_HALF):
        f = f + yg_ref[k] * gate[:, k0 + k:k0 + k + 1]
    return f


def _combine_first_kernel(alpha, h_ref, yg_ref, gate_ref, o_ref):
    o_ref[...] = alpha * h_ref[...] + _weighted_rows(yg_ref, gate_ref[...], 0)


def _combine_last_kernel(part_ref, yg_ref, gate_ref, g_ref, b_ref, o_ref):
    acc = part_ref[...] + _weighted_rows(yg_ref, gate_ref[...], COMBINE_HALF)
    o_ref[...] = _layer_norm(acc, g_ref[...], b_ref[...])


def combine_first(h, yg, gate, alpha, tm):
    row = lambda i: (i, 0)
    return pl.pallas_call(
        functools.partial(_combine_first_kernel, alpha),
        grid=(T_ALL // tm,),
        in_specs=[
            pl.BlockSpec((tm, D_MODEL), row),
            pl.BlockSpec((COMBINE_HALF, tm, D_MODEL), lambda i: (0, i, 0)),
            pl.BlockSpec((tm, TOP_K), row),
        ],
        out_specs=pl.BlockSpec((tm, D_MODEL), row),
        out_shape=jax.ShapeDtypeStruct((T_ALL, D_MODEL), F32),
        compiler_params=_params(("parallel",)),
        name="combine_first",
    )(h, yg, gate)


def combine_last(part, yg, gate, ln_g, ln_b, row_off, m, tm):
    blk_off = row_off // tm
    whole = lambda i: (0, 0)
    return pl.pallas_call(
        _combine_last_kernel,
        grid=(m // tm,),
        in_specs=[
            pl.BlockSpec((tm, D_MODEL), lambda i: (blk_off + i, 0)),
            pl.BlockSpec((COMBINE_HALF, tm, D_MODEL), lambda i: (0, blk_off + i, 0)),
            pl.BlockSpec((tm, TOP_K), lambda i: (blk_off + i, 0)),
            pl.BlockSpec((1, D_MODEL), whole),
            pl.BlockSpec((1, D_MODEL), whole),
        ],
        out_specs=pl.BlockSpec((tm, D_MODEL), lambda i: (i, 0)),
        out_shape=jax.ShapeDtypeStruct((m, D_MODEL), F32),
        compiler_params=_params(("parallel",)),
        name="combine_last",
    )(part, yg, gate, ln_g, ln_b)


def _retention_tables(chunk, group):
    log_gamma = jnp.log(1.0 - jnp.exp2(-5.0 - jnp.arange(N_RET_HEADS, dtype=F32)))
    n = chunk * group
    pos = jnp.arange(n) % chunk
    seq = jnp.arange(n) // chunk
    diff = (pos[:, None] - pos[None, :]).astype(F32)
    same = seq[:, None] == seq[None, :]
    lg = log_gamma[:, None, None]
    dec = jnp.where(same[None] & (diff >= 0)[None], jnp.exp(lg * jnp.maximum(diff, 0.0)[None]), 0.0)
    posf = pos.astype(F32)
    qd = jnp.exp(log_gamma[:, None] * (posf + 1.0))[..., None]
    kd = jnp.exp(log_gamma[:, None] * (chunk - 1.0 - posf))[..., None]
    cd = jnp.exp(log_gamma * chunk)
    return cd, dec, qd, kd


def _alibi_slopes():
    h = jnp.arange(1, N_SWA_HEADS + 1, dtype=F32)
    return jnp.exp2(-8.0 * h / N_SWA_HEADS)


def kernel(x_prompt, x_sample, state_ret, cache_swa_k, cache_swa_v, w_in, w_ret_o, w_swa_o, w_out, swa_sinks,
           ln1_g, ln1_b, w_router, b_router, w_gate_up, b_gate_up, w_down, b_down, ln2_g, ln2_b):
    alpha = (2.0 * DEPTH) ** 0.25
    kv_w = N_SWA_KV_HEADS * SWA_HEAD_DIM
    w = w_in[0]
    w_in_bf = w.astype(BF16)
    w_ret_o_bf = w_ret_o[0].astype(BF16)
    w_swa_o_bf = w_swa_o[0].astype(BF16)
    w_out_bf = w_out[0].astype(BF16)
    w_router_pad, b_router_pad = _router_split(w_router[0], b_router[0])
    ln1g, ln1b = ln1_g[0].reshape(1, D_MODEL), ln1_b[0].reshape(1, D_MODEL)
    ln2g, ln2b = ln2_g[0].reshape(1, D_MODEL), ln2_b[0].reshape(1, D_MODEL)
    slopes = _alibi_slopes()
    sinks = swa_sinks[0].astype(F32)

    xp = x_prompt.reshape(T_PROMPT, D_MODEL)
    xs = x_sample.reshape(T_SAMPLE, D_MODEL)

    u_p = in_proj(xp, w_in_bf, 1024, 1280)
    o_r_p, state_p, o_s_p = mixer_prompt(u_p, _retention_tables(RET_CHUNK, 1), slopes, sinks)
    kv_p = u_p.reshape(BATCH, SEQ, D_IN)[:, SEQ - WINDOW:, COL_KVS:COL_KVS + 2 * kv_w]
    k_cache_p = kv_p[..., :kv_w].reshape(1, BATCH, WINDOW, N_SWA_KV_HEADS, SWA_HEAD_DIM)
    v_cache_p = kv_p[..., kv_w:].reshape(1, BATCH, WINDOW, N_SWA_KV_HEADS, SWA_HEAD_DIM)
    merged_p = merge_branches(o_r_p, o_s_p, u_p, w_ret_o_bf, w_swa_o_bf, 512)

    u_s = in_proj(xs, w_in_bf, 512, 1280)
    q_t = (u_s[:, COL_QS:COL_QS + 2048]
           .reshape(DEC_BATCH, DEC_SEQ, N_SWA_KV_HEADS, SWA_GROUP, SWA_HEAD_DIM)
           .transpose(0, 2, 1, 3, 4).reshape(DEC_BATCH, N_SWA_KV_HEADS, SS_ROWS, SWA_HEAD_DIM))
    kv_new = u_s[:, COL_KVS:COL_KVS + 2 * kv_w].reshape(DEC_BATCH, DEC_SEQ, 2 * kv_w)
    o_r_s, state_s, o_s_t, k_cache_s, v_cache_s = mixer_sample(
        u_s, state_ret[0], _retention_tables(DEC_SEQ, RS_BB), q_t, kv_new,
        cache_swa_k[0].reshape(DEC_BATCH, WINDOW, kv_w), cache_swa_v[0].reshape(DEC_BATCH, WINDOW, kv_w),
        _swa_sample_tables(slopes, sinks))
    o_s_s = (o_s_t.reshape(DEC_BATCH, N_SWA_KV_HEADS, DEC_SEQ, SWA_GROUP, SWA_HEAD_DIM)
             .transpose(0, 2, 1, 3, 4).reshape(T_SAMPLE, 2048).astype(BF16))
    merged_s = merge_branches(o_r_s, o_s_s, u_s, w_ret_o_bf, w_swa_o_bf, 512)

    h, logits = post_mix(merged_p, merged_s, xp, xs, w_out_bf, ln1g, ln1b, w_router_pad, b_router_pad,
                         alpha, 512)

    gate, row_tok, pos, tables = route(logits)
    x_sorted = h.at[row_tok].get(mode="promise_in_bounds")
    y_rows = moe_experts(x_sorted, tables, w_gate_up[0], b_gate_up[0].reshape(N_EXPERTS, 1, 2 * D_FF),
                         w_down[0], b_down[0].reshape(N_EXPERTS, 1, D_MODEL))
    yg_first = y_rows.at[pos[:COMBINE_HALF]].get(mode="promise_in_bounds")
    yg_last = y_rows.at[pos[COMBINE_HALF:]].get(mode="promise_in_bounds")
    part = combine_first(h, yg_first, gate, alpha, 256)
    y_p = combine_last(part, yg_last, gate, ln2g, ln2b, 0, T_PROMPT, 256)
    y_s = combine_last(part, yg_last, gate, ln2g, ln2b, T_PROMPT, T_SAMPLE, 256)

    return (y_p.reshape(BATCH, SEQ, D_MODEL), y_s.reshape(DEC_BATCH, DEC_SEQ, D_MODEL),
            state_p[None], k_cache_p, v_cache_p,
            state_s[None], k_cache_s.reshape(1, DEC_BATCH, WINDOW, N_SWA_KV_HEADS, SWA_HEAD_DIM),
            v_cache_s.reshape(1, DEC_BATCH, WINDOW, N_SWA_KV_HEADS, SWA_HEAD_DIM))
```

```python
import functools

import jax
import jax.numpy as jnp
from jax import lax
from jax.experimental import pallas as pl
from jax.experimental.pallas import tpu as pltpu

F32 = jnp.float32
BF16 = jnp.bfloat16

D_MODEL = 2048
BATCH = 4
SEQ = 2048
DEC_BATCH = 128
DEC_SEQ = 4
N_RET_HEADS = 8
RET_DK = 128
RET_DV = 256
RET_CHUNK = 128
N_SWA_HEADS = 32
N_SWA_KV_HEADS = 4
SWA_GROUP = 8
SWA_HEAD_DIM = 64
WINDOW = 128
N_EXPERTS = 32
TOP_K = 4
D_FF = 2048
SWIGLU_LIMIT = 7.0
SWIGLU_ALPHA = 1.702
LN_EPS = 1e-5
GN_EPS = 1e-6
NEG_INF = -1e30
DEPTH = 1

T_PROMPT = BATCH * SEQ
T_SAMPLE = DEC_BATCH * DEC_SEQ
T_ALL = T_PROMPT + T_SAMPLE
TK_ALL = T_ALL * TOP_K
D_IN = 12800

COL_QR, COL_KR, COL_VR, COL_GR, COL_QS, COL_KVS, COL_GATE_R, COL_GATE_S = (
    0, 1024, 2048, 4096, 6144, 8192, 8704, 10752)

MOE_SUB = 64
MOE_CHUNK_SUBS = 22
MOE_STATIC_SIZES = tuple(range(13, 23))
MOE_R = MOE_SUB * MOE_CHUNK_SUBS
MOE_NSUB_MAX = TK_ALL // MOE_SUB + N_EXPERTS
MOE_XROWS = (MOE_NSUB_MAX + MOE_CHUNK_SUBS - 1) * MOE_SUB
MOE_NC = TK_ALL // MOE_R + N_EXPERTS + 2
MOE_TF = 256
MOE_NJ = D_FF // MOE_TF
MOE_VMEM_LIMIT = 60 * 1024 * 1024

VMEM_LIMIT = 56 * 1024 * 1024


def _params(sem):
    return pltpu.CompilerParams(dimension_semantics=sem, vmem_limit_bytes=VMEM_LIMIT)


def _in_proj_kernel(x_ref, w_ref, o_ref):
    o_ref[...] = jnp.dot(x_ref[...].astype(BF16), w_ref[...], preferred_element_type=F32)


def in_proj(x2d, w_bf, tm, tn):
    m, k = x2d.shape
    n = w_bf.shape[1]
    return pl.pallas_call(
        _in_proj_kernel,
        grid=(m // tm, n // tn),
        in_specs=[pl.BlockSpec((tm, k), lambda i, j: (i, 0)),
                  pl.BlockSpec((k, tn), lambda i, j: (0, j))],
        out_specs=pl.BlockSpec((tm, tn), lambda i, j: (i, j)),
        out_shape=jax.ShapeDtypeStruct((m, n), F32),
        compiler_params=_params(("parallel", "parallel")),
        name="in_proj",
    )(x2d, w_bf)


def _group_norm_gate(o, g):
    mu = jnp.mean(o, axis=-1, keepdims=True)
    oc = o - mu
    var = jnp.mean(oc * oc, axis=-1, keepdims=True)
    return oc * lax.rsqrt(var + GN_EPS) * (g * jax.nn.sigmoid(g))


def _dot_nt(a, b):
    return lax.dot_general(a, b, (((1,), (1,)), ((), ())), preferred_element_type=F32)


def _layer_norm(x, g, b):
    mu = jnp.mean(x, axis=-1, keepdims=True)
    xc = x - mu
    var = jnp.mean(xc * xc, axis=-1, keepdims=True)
    return xc * lax.rsqrt(var + LN_EPS) * g + b


def _ret_prompt_kernel(cd_ref, dec_ref, qd_ref, kd_ref, q_ref, k_ref, v_ref, g_ref, o_ref, st_ref):
    c = pl.program_id(1)

    @pl.when(c == 0)
    def _():
        st_ref[...] = jnp.zeros_like(st_ref)

    heads = range(N_RET_HEADS)
    q = jnp.stack([q_ref[:, h * RET_DK:(h + 1) * RET_DK] for h in heads])
    k = jnp.stack([k_ref[:, h * RET_DK:(h + 1) * RET_DK] for h in heads]) * (RET_DK ** -0.5)
    v = jnp.stack([v_ref[:, h * RET_DV:(h + 1) * RET_DV] for h in heads]).astype(BF16)
    g = jnp.stack([g_ref[:, h * RET_DV:(h + 1) * RET_DV] for h in heads])
    state = st_ref[0]
    bmm = functools.partial(lax.dot_general, preferred_element_type=F32)
    scores = bmm(q.astype(BF16), k.astype(BF16), (((2,), (2,)), ((0,), (0,)))) * dec_ref[...]
    o = (bmm(scores.astype(BF16), v, (((2,), (1,)), ((0,), (0,))))
         + bmm((q * qd_ref[...]).astype(BF16), state.astype(BF16), (((2,), (1,)), ((0,), (0,)))))
    kt = jnp.swapaxes(k * kd_ref[...], 1, 2).astype(BF16)
    cd = jnp.stack([jnp.full((1, 1), cd_ref[h], F32) for h in heads])
    st_ref[0] = cd * state + bmm(kt, v, (((2,), (1,)), ((0,), (0,))))
    out = _group_norm_gate(o, g).astype(BF16)
    for h in heads:
        o_ref[:, h * RET_DV:(h + 1) * RET_DV] = out[h]


RS_BB = 32


def _ret_sample_kernel(cd_ref, dec_ref, qd_ref, kd_ref, q_ref, k_ref, v_ref, g_ref, st_ref,
                       o_ref, sto_ref):
    h = pl.program_id(1)
    q = q_ref[...]
    k = k_ref[...] * (RET_DK ** -0.5)
    v = v_ref[...].astype(BF16)
    scores = _dot_nt(q.astype(BF16), k.astype(BF16)) * dec_ref[0]
    o_intra = jnp.dot(scores.astype(BF16), v, preferred_element_type=F32)
    qq = (q * qd_ref[0]).astype(BF16)
    kt = (k * kd_ref[0]).T
    col_batch = lax.broadcasted_iota(jnp.int32, kt.shape, 1) // DEC_SEQ
    first_of_pair = lax.broadcasted_iota(jnp.int32, (8, RET_DV), 0) < DEC_SEQ
    cd = cd_ref[h]
    pieces = []
    for p in range(RS_BB // 2):
        q8 = qq[8 * p:8 * p + 8]
        s0 = st_ref[2 * p].astype(BF16)
        s1 = st_ref[2 * p + 1].astype(BF16)
        r0 = jnp.dot(q8, s0, preferred_element_type=F32)
        r1 = jnp.dot(q8, s1, preferred_element_type=F32)
        pieces.append(jnp.where(first_of_pair, r0, r1))
    for b in range(RS_BB):
        ktm = jnp.where(col_batch == b, kt, 0.0).astype(BF16)
        sto_ref[b] = cd * st_ref[b] + jnp.dot(ktm, v, preferred_element_type=F32)
    o = o_intra + jnp.concatenate(pieces, axis=0)
    o_ref[...] = _group_norm_gate(o, g_ref[...]).astype(BF16)


def _swa_prompt_kernel(sink_ref, bias_ref, q_ref, kvc_ref, kvp_ref, o_ref):
    kv_w = N_SWA_KV_HEADS * SWA_HEAD_DIM
    hd = SWA_HEAD_DIM
    qt = q_ref[...].T
    outs = []
    for kvh in range(N_SWA_KV_HEADS):
        lo = kvh * hd
        k_cat = jnp.concatenate([kvp_ref[:, lo:lo + hd], kvc_ref[:, lo:lo + hd]], axis=0).astype(BF16)
        v_cat = jnp.concatenate([kvp_ref[:, kv_w + lo:kv_w + lo + hd],
                                 kvc_ref[:, kv_w + lo:kv_w + lo + hd]], axis=0)
        vt = v_cat.T.astype(BF16)
        heads = range(kvh * SWA_GROUP, (kvh + 1) * SWA_GROUP)
        q8 = jnp.concatenate([qt[h * hd:(h + 1) * hd, :] for h in heads], axis=1)
        q8 = (q8 * (hd ** -0.5)).astype(BF16)
        s = jnp.dot(k_cat, q8, preferred_element_type=F32) + bias_ref[0, kvh]
        sink = sink_ref[kvh]
        m = jnp.maximum(jnp.max(s, axis=0, keepdims=True), sink)
        e = jnp.exp(s - m)
        den = jnp.sum(e, axis=0, keepdims=True) + jnp.exp(sink - m)
        p = (e * (1.0 / den)).astype(BF16)
        ot = jnp.dot(vt, p, preferred_element_type=F32)
        outs += [ot[:, g * WINDOW:(g + 1) * WINDOW] for g in range(SWA_GROUP)]
    o_ref[...] = jnp.concatenate(outs, axis=0).T.astype(BF16)


def _swa_prompt_tables(slopes, sinks):
    qi = jnp.arange(WINDOW)[:, None] + WINDOW
    ki = jnp.arange(2 * WINDOW)[None, :]
    dist = qi - ki
    in_window = (dist >= 0) & (dist < WINDOW)
    valid = jnp.stack([in_window & (ki >= WINDOW), in_window])
    pen = -(slopes[:, None, None] * dist.astype(F32)[None])
    bias = jnp.where(valid[:, None], pen[None], NEG_INF)
    bias = bias.reshape(2, N_SWA_KV_HEADS, SWA_GROUP, WINDOW, 2 * WINDOW).transpose(0, 1, 4, 2, 3)
    bias = bias.reshape(2, N_SWA_KV_HEADS, 2 * WINDOW, SWA_GROUP * WINDOW)
    sink_l = jnp.repeat(sinks.reshape(N_SWA_KV_HEADS, SWA_GROUP), WINDOW, axis=1)
    return sink_l.reshape(N_SWA_KV_HEADS, 1, SWA_GROUP * WINDOW), bias


def _mixer_prompt_kernel(cd_ref, dec_ref, qd_ref, kd_ref, sink_ref, bias_ref, qr_ref, kr_ref, vr_ref, gr_ref,
                         qs_ref, kvc_ref, kvp_ref, or_ref, st_ref, os_ref):
    _ret_prompt_kernel(cd_ref, dec_ref, qd_ref, kd_ref, qr_ref, kr_ref, vr_ref, gr_ref, or_ref, st_ref)
    _swa_prompt_kernel(sink_ref, bias_ref, qs_ref, kvc_ref, kvp_ref, os_ref)


def mixer_prompt(u, ret_tables, slopes, sinks):
    cd, dec, qd, kd = ret_tables
    sink_l, bias = _swa_prompt_tables(slopes, sinks)
    nb = SEQ // WINDOW
    lanes = SWA_GROUP * WINDOW
    kv_blk = COL_KVS // 512
    row = lambda b, c: b * nb + c
    whole3 = lambda b, c: (0, 0, 0)
    return pl.pallas_call(
        _mixer_prompt_kernel,
        grid=(BATCH, nb),
        in_specs=[
            pl.BlockSpec(memory_space=pltpu.SMEM),
            pl.BlockSpec((N_RET_HEADS, RET_CHUNK, RET_CHUNK), whole3),
            pl.BlockSpec((N_RET_HEADS, RET_CHUNK, 1), whole3),
            pl.BlockSpec((N_RET_HEADS, RET_CHUNK, 1), whole3),
            pl.BlockSpec((N_SWA_KV_HEADS, 1, lanes), whole3),
            pl.BlockSpec((1, N_SWA_KV_HEADS, 2 * WINDOW, lanes), lambda b, c: (jnp.minimum(c, 1), 0, 0, 0)),
            pl.BlockSpec((RET_CHUNK, 1024), lambda b, c: (row(b, c), COL_QR // 1024)),
            pl.BlockSpec((RET_CHUNK, 1024), lambda b, c: (row(b, c), COL_KR // 1024)),
            pl.BlockSpec((RET_CHUNK, 2048), lambda b, c: (row(b, c), COL_VR // 2048)),
            pl.BlockSpec((RET_CHUNK, 2048), lambda b, c: (row(b, c), COL_GR // 2048)),
            pl.BlockSpec((WINDOW, 2048), lambda b, c: (row(b, c), COL_QS // 2048)),
            pl.BlockSpec((WINDOW, 512), lambda b, c: (row(b, c), kv_blk)),
            pl.BlockSpec((WINDOW, 512), lambda b, c: (b * nb + jnp.maximum(c - 1, 0), kv_blk)),
        ],
        out_specs=[
            pl.BlockSpec((RET_CHUNK, 2048), lambda b, c: (row(b, c), 0)),
            pl.BlockSpec((1, N_RET_HEADS, RET_DK, RET_DV), lambda b, c: (b, 0, 0, 0)),
            pl.BlockSpec((WINDOW, 2048), lambda b, c: (row(b, c), 0)),
        ],
        out_shape=[
            jax.ShapeDtypeStruct((T_PROMPT, 2048), BF16),
            jax.ShapeDtypeStruct((BATCH, N_RET_HEADS, RET_DK, RET_DV), F32),
            jax.ShapeDtypeStruct((T_PROMPT, 2048), BF16),
        ],
        compiler_params=_params(("parallel", "arbitrary")),
        name="mixer_prompt",
    )(cd, dec, qd, kd, sink_l, bias, u, u, u, u, u, u, u)


SS_BB = RS_BB // N_RET_HEADS
SS_ROWS = DEC_SEQ * SWA_GROUP


def _bf16_round(x):
    return x.astype(BF16).astype(F32)


def _swa_sample_kernel(sink_ref, bias1_ref, bias2_ref, q_ref, kvn_ref, ck_ref, cv_ref,
                       o_ref, cko_ref, cvo_ref):
    kv_w = N_SWA_KV_HEADS * SWA_HEAD_DIM
    hd = SWA_HEAD_DIM
    scale = hd ** -0.5
    bb = q_ref.shape[0]
    heads = range(N_SWA_KV_HEADS)

    def per_pair(f):
        return jnp.concatenate([f(kvh) for kvh in heads], axis=0)

    def per_head_table(t):
        return per_pair(lambda kvh: jnp.broadcast_to(t[kvh][None], (bb,) + t.shape[1:]))

    q = per_pair(lambda kvh: q_ref[:, kvh])
    ck = per_pair(lambda kvh: ck_ref[:, :, kvh * hd:(kvh + 1) * hd]).astype(BF16)
    cv = per_pair(lambda kvh: cv_ref[:, :, kvh * hd:(kvh + 1) * hd]).astype(BF16)
    kn = _bf16_round(per_pair(lambda kvh: kvn_ref[:, :, kvh * hd:(kvh + 1) * hd]))
    vn = _bf16_round(per_pair(lambda kvh: kvn_ref[:, :, kv_w + kvh * hd:kv_w + (kvh + 1) * hd]))
    sink = per_head_table(sink_ref[...])
    s1 = lax.dot_general(q.astype(BF16), ck, (((2,), (2,)), ((0,), (0,))),
                         preferred_element_type=F32) * scale + per_head_table(bias1_ref[...])
    qr = _bf16_round(q)
    s2 = [jnp.sum(qr * kn[:, t2:t2 + 1, :], axis=-1, keepdims=True) * scale
          + per_head_table(bias2_ref[:, t2]) for t2 in range(DEC_SEQ)]
    m = jnp.maximum(jnp.max(s1, axis=-1, keepdims=True), sink)
    for st in s2:
        m = jnp.maximum(m, st)
    e1 = jnp.exp(s1 - m)
    e2 = [jnp.exp(st - m) for st in s2]
    den = jnp.sum(e1, axis=-1, keepdims=True) + jnp.exp(sink - m)
    for et in e2:
        den = den + et
    inv = 1.0 / den
    o = lax.dot_general((e1 * inv).astype(BF16), cv, (((2,), (1,)), ((0,), (0,))),
                        preferred_element_type=F32)
    for t2 in range(DEC_SEQ):
        o = o + _bf16_round(e2[t2] * inv) * vn[:, t2:t2 + 1, :]
    for kvh in heads:
        o_ref[:, kvh] = o[kvh * bb:(kvh + 1) * bb]
    keep = WINDOW - DEC_SEQ
    cko_ref[:, 0:keep, :] = ck_ref[:, DEC_SEQ:WINDOW, :]
    cko_ref[:, keep:WINDOW, :] = kvn_ref[:, :, 0:kv_w]
    cvo_ref[:, 0:keep, :] = cv_ref[:, DEC_SEQ:WINDOW, :]
    cvo_ref[:, keep:WINDOW, :] = kvn_ref[:, :, kv_w:2 * kv_w]


def _swa_sample_tables(slopes, sinks):
    g_slopes = jnp.tile(slopes.reshape(N_SWA_KV_HEADS, 1, SWA_GROUP), (1, DEC_SEQ, 1)).reshape(
        N_SWA_KV_HEADS, SS_ROWS)
    sink_rows = jnp.tile(sinks.reshape(N_SWA_KV_HEADS, 1, SWA_GROUP), (1, DEC_SEQ, 1)).reshape(
        N_SWA_KV_HEADS, SS_ROWS, 1)
    t_row = jnp.arange(SS_ROWS) // SWA_GROUP
    jcol = jnp.arange(WINDOW)
    dist1 = (t_row[:, None] + WINDOW - jcol[None, :])
    bias1 = jnp.where((dist1 < WINDOW)[None], -(g_slopes[:, :, None] * dist1.astype(F32)[None]), NEG_INF)
    t2 = jnp.arange(DEC_SEQ)
    dist2 = t_row[None, :] - t2[:, None]
    bias2 = jnp.where((dist2 >= 0)[None], -(g_slopes[:, None, :] * dist2.astype(F32)[None]), NEG_INF)
    return sink_rows, bias1, bias2[..., None]


def _mixer_sample_kernel(cd_ref, dec_ref, qd_ref, kd_ref, qr_ref, kr_ref, vr_ref, gr_ref, st_ref,
                         sink_ref, bias1_ref, bias2_ref, qs_ref, kvn_ref, ck_ref, cv_ref,
                         or_ref, sto_ref, os_ref, cko_ref, cvo_ref):
    _ret_sample_kernel(cd_ref, dec_ref, qd_ref, kd_ref, qr_ref, kr_ref, vr_ref, gr_ref, st_ref, or_ref, sto_ref)
    _swa_sample_kernel(sink_ref, bias1_ref, bias2_ref, qs_ref, kvn_ref, ck_ref, cv_ref, os_ref, cko_ref, cvo_ref)


def mixer_sample(u_s, state, ret_tables, q_t, kv_new, cache_k, cache_v, swa_tables):
    cd, dec, qd, kd = ret_tables
    sink_rows, bias1, bias2 = swa_tables
    kv_w = N_SWA_KV_HEADS * SWA_HEAD_DIM
    rows = RS_BB * DEC_SEQ
    per_head = lambda i, h: (h, 0, 0)
    whole3 = lambda i, h: (0, 0, 0)
    blk = lambda i, h: i * N_RET_HEADS + h
    st_spec = pl.BlockSpec((RS_BB, None, RET_DK, RET_DV), lambda i, h: (i, h, 0, 0))
    cache_spec = pl.BlockSpec((SS_BB, WINDOW, kv_w), lambda i, h: (blk(i, h), 0, 0))
    q_spec = pl.BlockSpec((SS_BB, N_SWA_KV_HEADS, SS_ROWS, SWA_HEAD_DIM), lambda i, h: (blk(i, h), 0, 0, 0))
    return pl.pallas_call(
        _mixer_sample_kernel,
        grid=(DEC_BATCH // RS_BB, N_RET_HEADS),
        in_specs=[
            pl.BlockSpec(memory_space=pltpu.SMEM),
            pl.BlockSpec((1, rows, rows), per_head),
            pl.BlockSpec((1, rows, 1), per_head),
            pl.BlockSpec((1, rows, 1), per_head),
            pl.BlockSpec((rows, RET_DK), lambda i, h: (i, COL_QR // RET_DK + h)),
            pl.BlockSpec((rows, RET_DK), lambda i, h: (i, COL_KR // RET_DK + h)),
            pl.BlockSpec((rows, RET_DV), lambda i, h: (i, COL_VR // RET_DV + h)),
            pl.BlockSpec((rows, RET_DV), lambda i, h: (i, COL_GR // RET_DV + h)),
            st_spec,
            pl.BlockSpec((N_SWA_KV_HEADS, SS_ROWS, 1), whole3),
            pl.BlockSpec((N_SWA_KV_HEADS, SS_ROWS, WINDOW), whole3),
            pl.BlockSpec((N_SWA_KV_HEADS, DEC_SEQ, SS_ROWS, 1), lambda i, h: (0, 0, 0, 0)),
            q_spec,
            pl.BlockSpec((SS_BB, DEC_SEQ, 2 * kv_w), lambda i, h: (blk(i, h), 0, 0)),
            cache_spec, cache_spec,
        ],
        out_specs=[pl.BlockSpec((rows, RET_DV), lambda i, h: (i, h)), st_spec, q_spec, cache_spec, cache_spec],
        out_shape=[
            jax.ShapeDtypeStruct((T_SAMPLE, 2048), BF16),
            jax.ShapeDtypeStruct((DEC_BATCH, N_RET_HEADS, RET_DK, RET_DV), F32),
            jax.ShapeDtypeStruct((DEC_BATCH, N_SWA_KV_HEADS, SS_ROWS, SWA_HEAD_DIM), F32),
            jax.ShapeDtypeStruct((DEC_BATCH, WINDOW, kv_w), F32),
            jax.ShapeDtypeStruct((DEC_BATCH, WINDOW, kv_w), F32),
        ],
        compiler_params=_params(("parallel", "parallel")),
        name="mixer_sample",
    )(cd, dec, qd, kd, u_s, u_s, u_s, u_s, state, sink_rows, bias1, bias2, q_t, kv_new, cache_k, cache_v)


def _merge_kernel(or_ref, os_ref, gr_ref, gs_ref, wr_ref, ws_ref, o_ref):
    a = jnp.dot(or_ref[...], wr_ref[...], preferred_element_type=F32)
    b = jnp.dot(os_ref[...], ws_ref[...], preferred_element_type=F32)
    merged = jax.nn.sigmoid(gr_ref[...]) * a + jax.nn.sigmoid(gs_ref[...]) * b
    o_ref[...] = merged.astype(BF16)


def merge_branches(o_r, o_s, u, w_ret_o_bf, w_swa_o_bf, tm):
    m = o_r.shape[0]
    resident = pl.BlockSpec((2048, D_MODEL), lambda i: (0, 0), pipeline_mode=pl.Buffered(1))
    return pl.pallas_call(
        _merge_kernel,
        grid=(m // tm,),
        in_specs=[
            pl.BlockSpec((tm, 2048), lambda i: (i, 0)),
            pl.BlockSpec((tm, 2048), lambda i: (i, 0)),
            pl.BlockSpec((pl.Element(tm), pl.Element(D_MODEL)), lambda i: (i * tm, COL_GATE_R)),
            pl.BlockSpec((pl.Element(tm), pl.Element(D_MODEL)), lambda i: (i * tm, COL_GATE_S)),
            resident, resident,
        ],
        out_specs=pl.BlockSpec((tm, D_MODEL), lambda i: (i, 0)),
        out_shape=jax.ShapeDtypeStruct((m, D_MODEL), BF16),
        compiler_params=_params(("parallel",)),
        name="merge_branches",
    )(o_r, o_s, u, u, w_ret_o_bf, w_swa_o_bf)


ROUTER_PAD = 128


def _post_mix_kernel(alpha, n_prompt_blocks, mp_ref, ms_ref, xp_ref, xs_ref, w_ref, g_ref, b_ref, wr_ref, br_ref,
                     h_ref, lg_ref):
    from_sample = pl.program_id(0) >= n_prompt_blocks
    merged = jnp.where(from_sample, ms_ref[...], mp_ref[...])
    x = jnp.where(from_sample, xs_ref[...], xp_ref[...])
    mix = jnp.dot(merged, w_ref[...], preferred_element_type=F32)
    h = _layer_norm(alpha * x + mix, g_ref[...], b_ref[...])
    h_ref[...] = h
    h_hi = h.astype(BF16)
    h_lo = (h - h_hi.astype(F32)).astype(BF16)
    hh = jnp.dot(h_hi, wr_ref[...], preferred_element_type=F32)
    lh = jnp.dot(h_lo, wr_ref[:, :ROUTER_PAD], preferred_element_type=F32)
    lg_ref[...] = (hh[:, :ROUTER_PAD] + hh[:, ROUTER_PAD:] + lh + br_ref[...]).T


def _router_split(w_router, b_router):
    w = jnp.pad(w_router, ((0, 0), (0, ROUTER_PAD - N_EXPERTS)))
    w_hi = w.astype(BF16)
    w_lo = (w - w_hi.astype(F32)).astype(BF16)
    b = jnp.pad(b_router, (0, ROUTER_PAD - N_EXPERTS)).reshape(1, ROUTER_PAD)
    return jnp.concatenate([w_hi, w_lo], axis=1), b


def post_mix(merged_p, merged_s, xp, xs, w_out_bf, ln_g, ln_b, w_router_pad, b_router_pad, alpha, tm):
    n_p = T_PROMPT // tm
    n_s = T_SAMPLE // tm
    prompt_row = lambda i: (jnp.minimum(i, n_p - 1), 0)
    sample_row = lambda i: (jnp.clip(i - n_p, 0, n_s - 1), 0)
    whole = lambda i: (0, 0)
    row = lambda i: (i, 0)
    return pl.pallas_call(
        functools.partial(_post_mix_kernel, alpha, n_p),
        grid=(n_p + n_s,),
        in_specs=[
            pl.BlockSpec((tm, 2048), prompt_row),
            pl.BlockSpec((tm, 2048), sample_row),
            pl.BlockSpec((tm, 2048), prompt_row),
            pl.BlockSpec((tm, 2048), sample_row),
            pl.BlockSpec((2048, 2048), whole),
            pl.BlockSpec((1, 2048), whole),
            pl.BlockSpec((1, 2048), whole),
            pl.BlockSpec((2048, 2 * ROUTER_PAD), whole),
            pl.BlockSpec((1, ROUTER_PAD), whole),
        ],
        out_specs=[pl.BlockSpec((tm, 2048), row), pl.BlockSpec((ROUTER_PAD, tm), lambda i: (0, i))],
        out_shape=[jax.ShapeDtypeStruct((T_ALL, D_MODEL), F32),
                   jax.ShapeDtypeStruct((ROUTER_PAD, T_ALL), F32)],
        compiler_params=_params(("parallel",)),
        name="post_mix",
    )(merged_p, merged_s, xp, xs, w_out_bf, ln_g, ln_b, w_router_pad, b_router_pad)


def _moe_kernel(ce_ref, cs_ref, cn_ref, nu_ref, tail_ref, x_hbm, wg_ref, wu_ref, wd_ref, bg_ref, bu_ref, bd_ref,
                y_hbm, x_f32, x_bf, y_acc, zero_buf, x_sem, y_sem, z_sem):
    c = pl.program_id(0)
    j = pl.program_id(1)
    nblk = cn_ref[c]
    slot = c % 2
    y_cur = y_acc.at[slot]

    def sub_rows(r):
        return pl.ds(pl.multiple_of(r * MOE_SUB, MOE_SUB), MOE_SUB)

    def x_copy(chunk):
        start = pl.multiple_of(cs_ref[chunk] * MOE_SUB, MOE_SUB)
        return pltpu.make_async_copy(x_hbm.at[pl.ds(start, MOE_R), :], x_f32, x_sem)

    def y_copy(r, sub):
        return pltpu.make_async_copy(y_cur.at[sub_rows(r), :], y_hbm.at[sub_rows(sub), :], y_sem.at[slot])

    def zero_copy(sub):
        return pltpu.make_async_copy(zero_buf, y_hbm.at[sub_rows(sub), :], z_sem)

    @pl.when(j == 0)
    def _():
        def drain(i, carry):
            y_copy(0, 0).wait()
            return carry

        lax.fori_loop(0, cn_ref[jnp.maximum(c - 2, 0)] * (c >= 2).astype(jnp.int32), drain, 0)

    @pl.when((nblk > 0) & (j == 0))
    def _():
        @pl.when(c == 0)
        def _():
            x_copy(0).start()

        x_copy(c).wait()

        def stage(r, carry):
            rows = sub_rows(r)
            x_bf[rows, :] = x_f32[rows, :].astype(BF16)
            y_cur[rows, :] = jnp.broadcast_to(bd_ref[0], (MOE_SUB, D_MODEL))
            return carry

        lax.fori_loop(0, nblk, stage, 0)

        @pl.when(c + 1 < nu_ref[0])
        def _():
            x_copy(c + 1).start()

    @pl.when(nblk > 0)
    def _():
        def rows_block(rows):
            xr = x_bf[rows, :]
            g = jnp.dot(xr, wg_ref[0], preferred_element_type=F32) + bg_ref[0]
            up = jnp.dot(xr, wu_ref[0], preferred_element_type=F32) + bu_ref[0]
            g = jnp.minimum(g, SWIGLU_LIMIT)
            up = jnp.clip(up, -SWIGLU_LIMIT, SWIGLU_LIMIT)
            glu = g * jax.nn.sigmoid(SWIGLU_ALPHA * g)
            act = ((up + 1.0) * glu).astype(BF16)
            y_cur[rows, :] += jnp.dot(act, wd_ref[0], preferred_element_type=F32)

        for n_static in MOE_STATIC_SIZES:
            @pl.when(nblk == n_static)
            def _(n_static=n_static):
                rows_block(pl.ds(0, n_static * MOE_SUB))

        @pl.when(nblk < MOE_STATIC_SIZES[0])
        def _():
            def one(r, carry):
                rows_block(sub_rows(r))
                return carry

            lax.fori_loop(0, nblk, one, 0)

    @pl.when((nblk > 0) & (j == MOE_NJ - 1))
    def _():
        def write_back(r, carry):
            y_copy(r, cs_ref[c] + r).start()
            return carry

        lax.fori_loop(0, nblk, write_back, 0)

    @pl.when((nblk == 0) & (j == 0))
    def _():
        first = tail_ref[0] + MOE_CHUNK_SUBS * (c - nu_ref[0])
        count = jnp.clip(MOE_XROWS // MOE_SUB - first, 0, MOE_CHUNK_SUBS)

        @pl.when(count > 0)
        def _():
            zero_buf[...] = jnp.zeros_like(zero_buf)

            def start(i, carry):
                zero_copy(first + i).start()
                return carry

            def wait(i, carry):
                zero_copy(first + i).wait()
                return carry

            lax.fori_loop(0, count, start, 0)
            lax.fori_loop(0, count, wait, 0)


def moe_experts(x_sorted, chunk_tables, w_gate_up, b_gate_up, w_down, b_down):
    nj = MOE_NJ

    def used_j(c, j, nu):
        return jnp.where(c < nu[0], j, nj - 1)

    return pl.pallas_call(
        _moe_kernel,
        grid_spec=pltpu.PrefetchScalarGridSpec(
            num_scalar_prefetch=5,
            grid=(MOE_NC, nj),
            in_specs=[
                pl.BlockSpec(memory_space=pl.ANY),
                pl.BlockSpec((1, D_MODEL, MOE_TF),
                             lambda c, j, ce, cs, cn, nu, tl: (ce[c], 0, used_j(c, j, nu))),
                pl.BlockSpec((1, D_MODEL, MOE_TF),
                             lambda c, j, ce, cs, cn, nu, tl: (ce[c], 0, nj + used_j(c, j, nu))),
                pl.BlockSpec((1, MOE_TF, D_MODEL),
                             lambda c, j, ce, cs, cn, nu, tl: (ce[c], used_j(c, j, nu), 0)),
                pl.BlockSpec((1, 1, MOE_TF),
                             lambda c, j, ce, cs, cn, nu, tl: (ce[c], 0, used_j(c, j, nu))),
                pl.BlockSpec((1, 1, MOE_TF),
                             lambda c, j, ce, cs, cn, nu, tl: (ce[c], 0, nj + used_j(c, j, nu))),
                pl.BlockSpec((1, 1, D_MODEL), lambda c, j, ce, cs, cn, nu, tl: (ce[c], 0, 0)),
            ],
            out_specs=pl.BlockSpec(memory_space=pl.ANY),
            scratch_shapes=[
                pltpu.VMEM((MOE_R, D_MODEL), F32),
                pltpu.VMEM((MOE_R, D_MODEL), BF16),
                pltpu.VMEM((2, MOE_R, D_MODEL), F32),
                pltpu.VMEM((MOE_SUB, D_MODEL), F32),
                pltpu.SemaphoreType.DMA(()),
                pltpu.SemaphoreType.DMA((2,)),
                pltpu.SemaphoreType.DMA(()),
            ],
        ),
        out_shape=jax.ShapeDtypeStruct((MOE_XROWS, D_MODEL), F32),
        compiler_params=pltpu.CompilerParams(dimension_semantics=("arbitrary", "arbitrary"),
                                             vmem_limit_bytes=MOE_VMEM_LIMIT),
        name="moe_experts",
    )(*chunk_tables, x_sorted, w_gate_up, w_gate_up, w_down, b_gate_up, b_gate_up, b_down)


ROUTE_TM = 512


def _route_kernel(lg_ref, tri_ref, e_ref, rank_ref, gate_ref, cnt_ref, seen):
    @pl.when(pl.program_id(0) == 0)
    def _():
        seen[...] = jnp.zeros_like(seen)

    work = lg_ref[0:N_EXPERTS, :]
    eidx = lax.broadcasted_iota(jnp.int32, work.shape, 0)
    vals, idxs, hots = [], [], []
    for _ in range(TOP_K):
        m = jnp.max(work, axis=0, keepdims=True)
        idx = jnp.min(jnp.where(work == m, eidx, N_EXPERTS), axis=0, keepdims=True)
        hot = eidx == idx
        vals.append(m)
        idxs.append(idx)
        hots.append(hot)
        work = jnp.where(hot, -jnp.inf, work)
    ex = [jnp.exp(v - vals[0]) for v in vals]
    inv = 1.0 / (ex[0] + ex[1] + ex[2] + ex[3])
    picked = jnp.zeros(work.shape, F32)
    for hot in hots:
        picked = picked + hot.astype(F32)
    before = jnp.dot(picked.astype(BF16), tri_ref[...], preferred_element_type=F32) + seen[...]
    ranks = [jnp.sum(jnp.where(hot, before, 0.0), axis=0, keepdims=True) for hot in hots]
    seen[...] += jnp.sum(picked, axis=1, keepdims=True)
    e_ref[...] = jnp.concatenate(idxs, axis=0)
    rank_ref[...] = jnp.concatenate(ranks, axis=0).astype(jnp.int32)
    gate_ref[...] = jnp.concatenate([e * inv for e in ex], axis=0)
    cnt_ref[...] = seen[...]


def route_topk(logits_t):
    tm = ROUTE_TM
    tri = (jnp.arange(tm)[:, None] < jnp.arange(tm)[None, :]).astype(BF16)
    pick = pl.BlockSpec((TOP_K, tm), lambda i: (0, i))
    return pl.pallas_call(
        _route_kernel,
        grid=(T_ALL // tm,),
        in_specs=[pl.BlockSpec((ROUTER_PAD, tm), lambda i: (0, i)),
                  pl.BlockSpec((tm, tm), lambda i: (0, 0))],
        out_specs=[pick, pick, pick, pl.BlockSpec((N_EXPERTS, 1), lambda i: (0, 0))],
        out_shape=[jax.ShapeDtypeStruct((TOP_K, T_ALL), jnp.int32),
                   jax.ShapeDtypeStruct((TOP_K, T_ALL), jnp.int32),
                   jax.ShapeDtypeStruct((TOP_K, T_ALL), F32),
                   jax.ShapeDtypeStruct((N_EXPERTS, 1), F32)],
        scratch_shapes=[pltpu.VMEM((N_EXPERTS, 1), F32)],
        compiler_params=_params(("arbitrary",)),
        name="route_topk",
    )(logits_t, tri)


def route(logits_t):
    e_kt, rank, gate_kt, counts_f = route_topk(logits_t)
    counts = counts_f.reshape(N_EXPERTS).astype(jnp.int32)
    nsub = (counts + MOE_SUB - 1) // MOE_SUB
    sub_base = jnp.cumsum(nsub) - nsub
    nch = (counts + MOE_R - 1) // MOE_R
    ch_end = jnp.cumsum(nch)
    ch_base = ch_end - nch
    n_used = ch_end[-1]
    is_e = e_kt[..., None] == jnp.arange(N_EXPERTS, dtype=jnp.int32)
    pos = MOE_SUB * jnp.sum(jnp.where(is_e, sub_base, 0), axis=-1) + rank
    tok = jnp.broadcast_to(jnp.arange(T_ALL, dtype=jnp.int32)[None, :], pos.shape)
    pad_i = jnp.arange(MOE_SUB - 1, dtype=jnp.int32)
    pad_pos = (MOE_SUB * sub_base + counts)[:, None] + pad_i[None, :]
    pad_pos = jnp.where(pad_i[None, :] < (MOE_SUB * nsub - counts)[:, None], pad_pos, MOE_XROWS)
    tail_pos = MOE_SUB * jnp.sum(nsub) + jnp.arange(MOE_XROWS - TK_ALL, dtype=jnp.int32)
    tail_pos = jnp.minimum(tail_pos, MOE_XROWS)
    keys = jnp.concatenate([pos.reshape(-1), pad_pos.reshape(-1), tail_pos]).astype(jnp.int32)
    vals = jnp.concatenate([tok.reshape(-1), jnp.zeros((keys.shape[0] - TK_ALL,), jnp.int32)])
    row_tok = lax.sort_key_val(keys, vals)[1][:MOE_XROWS]
    c = jnp.arange(MOE_NC, dtype=jnp.int32)
    cc = jnp.minimum(c, n_used - 1)
    ce = jnp.minimum(jnp.sum(cc[:, None] >= ch_end[None, :], axis=1), N_EXPERTS - 1).astype(jnp.int32)
    kk = cc - ch_base[ce]
    cs = (sub_base[ce] + MOE_CHUNK_SUBS * kk).astype(jnp.int32)
    cn = jnp.where(c < n_used, jnp.minimum(MOE_CHUNK_SUBS, nsub[ce] - MOE_CHUNK_SUBS * kk), 0).astype(jnp.int32)
    tables = (ce, cs, cn, n_used.reshape(1).astype(jnp.int32), jnp.sum(nsub).reshape(1).astype(jnp.int32))
    return gate_kt.T, row_tok, pos, tables


def _combine_kernel(alpha, n_prompt_blocks, h_ref, yg_ref, gate_ref, g_ref, b_ref, op_ref, os_ref):
    gate = gate_ref[...]
    f = yg_ref[0] * gate[:, 0:1]
    for k in range(1, TOP_K):
        f = f + yg_ref[k] * gate[:, k:k + 1]
    y = _layer_norm(alpha * h_ref[...] + f, g_ref[...], b_ref[...])
    is_prompt = pl.program_id(0) < n_prompt_blocks

    @pl.when(is_prompt)
    def _():
        op_ref[...] = y

    @pl.when(jnp.logical_not(is_prompt))
    def _():
        os_ref[...] = y


def combine(h, yg, gate, ln_g, ln_b, alpha, tm):
    n_p = T_PROMPT // tm
    n_s = T_SAMPLE // tm
    whole = lambda i: (0, 0)
    row = lambda i: (i, 0)
    return pl.pallas_call(
        functools.partial(_combine_kernel, alpha, n_p),
        grid=(n_p + n_s,),
        in_specs=[
            pl.BlockSpec((tm, D_MODEL), row),
            pl.BlockSpec((TOP_K, tm, D_MODEL), lambda i: (0, i, 0)),
            pl.BlockSpec((tm, TOP_K), row),
            pl.BlockSpec((1, D_MODEL), whole),
            pl.BlockSpec((1, D_MODEL), whole),
        ],
        out_specs=[pl.BlockSpec((tm, D_MODEL), lambda i: (jnp.minimum(i, n_p - 1), 0)),
                   pl.BlockSpec((tm, D_MODEL), lambda i: (jnp.clip(i - n_p, 0, n_s - 1), 0))],
        out_shape=[jax.ShapeDtypeStruct((T_PROMPT, D_MODEL), F32),
                   jax.ShapeDtypeStruct((T_SAMPLE, D_MODEL), F32)],
        compiler_params=_params(("arbitrary",)),
        name="combine",
    )(h, yg, gate, ln_g, ln_b)


def _retention_tables(chunk, group):
    log_gamma = jnp.log(1.0 - jnp.exp2(-5.0 - jnp.arange(N_RET_HEADS, dtype=F32)))
    n = chunk * group
    pos = jnp.arange(n) % chunk
    seq = jnp.arange(n) // chunk
    diff = (pos[:, None] - pos[None, :]).astype(F32)
    same = seq[:, None] == seq[None, :]
    lg = log_gamma[:, None, None]
    dec = jnp.where(same[None] & (diff >= 0)[None], jnp.exp(lg * jnp.maximum(diff, 0.0)[None]), 0.0)
    posf = pos.astype(F32)
    qd = jnp.exp(log_gamma[:, None] * (posf + 1.0))[..., None]
    kd = jnp.exp(log_gamma[:, None] * (chunk - 1.0 - posf))[..., None]
    cd = jnp.exp(log_gamma * chunk)
    return cd, dec, qd, kd


def _alibi_slopes():
    h = jnp.arange(1, N_SWA_HEADS + 1, dtype=F32)
    return jnp.exp2(-8.0 * h / N_SWA_HEADS)


def kernel(x_prompt, x_sample, state_ret, cache_swa_k, cache_swa_v, w_in, w_ret_o, w_swa_o, w_out, swa_sinks,
           ln1_g, ln1_b, w_router, b_router, w_gate_up, b_gate_up, w_down, b_down, ln2_g, ln2_b):
    alpha = (2.0 * DEPTH) ** 0.25
    kv_w = N_SWA_KV_HEADS * SWA_HEAD_DIM
    w = w_in[0]
    w_in_bf = w.astype(BF16)
    w_ret_o_bf = w_ret_o[0].astype(BF16)
    w_swa_o_bf = w_swa_o[0].astype(BF16)
    w_out_bf = w_out[0].astype(BF16)
    w_router_pad, b_router_pad = _router_split(w_router[0], b_router[0])
    ln1g, ln1b = ln1_g[0].reshape(1, D_MODEL), ln1_b[0].reshape(1, D_MODEL)
    ln2g, ln2b = ln2_g[0].reshape(1, D_MODEL), ln2_b[0].reshape(1, D_MODEL)
    slopes = _alibi_slopes()
    sinks = swa_sinks[0].astype(F32)

    xp = x_prompt.reshape(T_PROMPT, D_MODEL)
    xs = x_sample.reshape(T_SAMPLE, D_MODEL)

    u_p = in_proj(xp, w_in_bf, 1024, 1280)
    o_r_p, state_p, o_s_p = mixer_prompt(u_p, _retention_tables(RET_CHUNK, 1), slopes, sinks)
    kv_p = u_p.reshape(BATCH, SEQ, D_IN)[:, SEQ - WINDOW:, COL_KVS:COL_KVS + 2 * kv_w]
    k_cache_p = kv_p[..., :kv_w].reshape(1, BATCH, WINDOW, N_SWA_KV_HEADS, SWA_HEAD_DIM)
    v_cache_p = kv_p[..., kv_w:].reshape(1, BATCH, WINDOW, N_SWA_KV_HEADS, SWA_HEAD_DIM)
    merged_p = merge_branches(o_r_p, o_s_p, u_p, w_ret_o_bf, w_swa_o_bf, 512)

    u_s = in_proj(xs, w_in_bf, 512, 1280)
    q_t = (u_s[:, COL_QS:COL_QS + 2048]
           .reshape(DEC_BATCH, DEC_SEQ, N_SWA_KV_HEADS, SWA_GROUP, SWA_HEAD_DIM)
           .transpose(0, 2, 1, 3, 4).reshape(DEC_BATCH, N_SWA_KV_HEADS, SS_ROWS, SWA_HEAD_DIM))
    kv_new = u_s[:, COL_KVS:COL_KVS + 2 * kv_w].reshape(DEC_BATCH, DEC_SEQ, 2 * kv_w)
    o_r_s, state_s, o_s_t, k_cache_s, v_cache_s = mixer_sample(
        u_s, state_ret[0], _retention_tables(DEC_SEQ, RS_BB), q_t, kv_new,
        cache_swa_k[0].reshape(DEC_BATCH, WINDOW, kv_w), cache_swa_v[0].reshape(DEC_BATCH, WINDOW, kv_w),
        _swa_sample_tables(slopes, sinks))
    o_s_s = (o_s_t.reshape(DEC_BATCH, N_SWA_KV_HEADS, DEC_SEQ, SWA_GROUP, SWA_HEAD_DIM)
             .transpose(0, 2, 1, 3, 4).reshape(T_SAMPLE, 2048).astype(BF16))
    merged_s = merge_branches(o_r_s, o_s_s, u_s, w_ret_o_bf, w_swa_o_bf, 512)

    h, logits = post_mix(merged_p, merged_s, xp, xs, w_out_bf, ln1g, ln1b, w_router_pad, b_router_pad,
                         alpha, 512)

    gate, row_tok, pos, tables = route(logits)
    x_sorted = h.at[row_tok].get(mode="promise_in_bounds")
    y_rows = moe_experts(x_sorted, tables, w_gate_up[0], b_gate_up[0].reshape(N_EXPERTS, 1, 2 * D_FF),
                         w_down[0], b_down[0].reshape(N_EXPERTS, 1, D_MODEL))
    yg = y_rows.at[pos].get(mode="promise_in_bounds")
    y_p, y_s = combine(h, yg, gate, ln2g, ln2b, alpha, 256)

    return (y_p.reshape(BATCH, SEQ, D_MODEL), y_s.reshape(DEC_BATCH, DEC_SEQ, D_MODEL),
            state_p[None], k_cache_p, v_cache_p,
            state_s[None], k_cache_s.reshape(1, DEC_BATCH, WINDOW, N_SWA_KV_HEADS, SWA_HEAD_DIM),
            v_cache_s.reshape(1, DEC_BATCH, WINDOW, N_SWA_KV_HEADS, SWA_HEAD_DIM))
```
